```python
import math
import jax, jax.numpy as jnp
from jax import lax
import numpy as np

D_MODEL = 4096
BATCH = 2
SEQ = 4096
DEPTH = 4

DA_WIDTH = D_MODEL // 2
DA_HEADS = 16
DA_HEAD_DIM = DA_WIDTH // (2 * DA_HEADS)
SC_WIDTH = D_MODEL - DA_WIDTH
CONV_W = 3
EV_SPLITS = (DA_WIDTH, DA_WIDTH, DA_WIDTH, SC_WIDTH, SC_WIDTH, SC_WIDTH)
EV_IN = sum(EV_SPLITS)
RW_WIDTH = D_MODEL // 2
RW_HEAD_DIM = 64
RW_HEADS = RW_WIDTH // RW_HEAD_DIM
W_LORA = 96
A_LORA = 96
V_LORA = 64
G_LORA = 256
RW_SPLITS = (RW_WIDTH, RW_WIDTH, RW_WIDTH, W_LORA, A_LORA, V_LORA, G_LORA)
RW_IN = sum(RW_SPLITS)
LNX_EPS = 64e-5
SA_WIDTH = D_MODEL - RW_WIDTH
SA_HEAD_DIM = 128
SA_HEADS = SA_WIDTH // SA_HEAD_DIM
IDX_HEADS = 16
IDX_DIM = 64
TOPK_MAX = 256
SA_SPLITS = (SA_WIDTH, SA_HEAD_DIM, SA_HEAD_DIM, IDX_HEADS * IDX_DIM, IDX_DIM, IDX_HEADS)
OD_IN = RW_IN + sum(SA_SPLITS)
MIX_OUT = D_MODEL
FFN_HIDDEN = -(-8 * D_MODEL // (3 * 256)) * 256
ROPE_THETA = 10000.0
Q_BLOCK = 128
EPS = 1e-6

kernel_name = 'hybrid_diffattn_shortconv_rwkv7_dsa'


def _split(z, sizes):
    out, off = [], 0
    for s in sizes:
        out.append(z[..., off:off + s])
        off += s
    return out


def _rms(x, g):
    xf = x.astype(jnp.float32)
    y = xf * lax.rsqrt(jnp.mean(xf * xf, axis=-1, keepdims=True) + EPS)
    return (y * g.astype(jnp.float32)).astype(x.dtype)


def _rope(x, pos):
    half = x.shape[-1] // 2
    inv = ROPE_THETA ** (-jnp.arange(half, dtype=jnp.float32) / half)
    ang = pos.astype(jnp.float32)[:, None] * inv[None, :]
    shp = (pos.shape[0],) + (1,) * (x.ndim - 3) + (half,)
    cos, sin = jnp.cos(ang).reshape(shp), jnp.sin(ang).reshape(shp)
    xf = x.astype(jnp.float32)
    x1, x2 = xf[..., :half], xf[..., half:]
    return jnp.concatenate([x1 * cos - x2 * sin, x2 * cos + x1 * sin], axis=-1).astype(x.dtype)


def _diff_attention(q, k, v, lam):
    B, T, H, _, Dh = q.shape
    scale = Dh ** -0.5
    k_pos = jnp.arange(T)

    def block(i):
        start = i * Q_BLOCK
        qb = lax.dynamic_slice_in_dim(q, start, Q_BLOCK, axis=1)
        s = jnp.einsum('bqhcd,bshcd->bhcqs', qb, k, preferred_element_type=jnp.float32) * scale
        q_pos = start + jnp.arange(Q_BLOCK)
        s = jnp.where(k_pos[None, :] <= q_pos[:, None], s, -jnp.inf)
        p = jax.nn.softmax(s, axis=-1)
        a = p[:, :, 0] - lam * p[:, :, 1]
        return jnp.einsum('bhqs,bshd->bqhd', a.astype(v.dtype), v)

    out = lax.map(block, jnp.arange(T // Q_BLOCK))
    return out.transpose(1, 0, 2, 3, 4).reshape(B, T, H, v.shape[-1])


def _dsa_attention(q, k, v, q_idx, k_idx, w_idx, k_sel):
    B, T, H, D = q.shape
    scale = D ** -0.5
    k_pos = jnp.arange(T)
    gather = jax.vmap(lambda a, idx: a[idx])

    def block(i):
        start = i * Q_BLOCK
        qb = lax.dynamic_slice_in_dim(q, start, Q_BLOCK, axis=1)
        qib = lax.dynamic_slice_in_dim(q_idx, start, Q_BLOCK, axis=1)
        wib = lax.dynamic_slice_in_dim(w_idx, start, Q_BLOCK, axis=1)
        q_pos = start + jnp.arange(Q_BLOCK)
        rel = jax.nn.relu(jnp.einsum('bqhd,bsd->bqhs', qib, k_idx, preferred_element_type=jnp.float32))
        score = jnp.einsum('bqhs,bqh->bqs', rel, wib.astype(jnp.float32))
        score = jnp.where((k_pos[None, :] <= q_pos[:, None])[None], score, -jnp.inf)
        _, sel = lax.top_k(score, k_sel)
        valid = sel <= q_pos[None, :, None]
        ks, vs = gather(k, sel), gather(v, sel)
        s = jnp.einsum('bqhd,bqkd->bhqk', qb, ks, preferred_element_type=jnp.float32) * scale
        s = jnp.where(valid[:, None], s, -jnp.inf)
        p = jax.nn.softmax(s, axis=-1)
        return jnp.einsum('bhqk,bqkd->bqhd', p.astype(vs.dtype), vs)

    out = lax.map(block, jnp.arange(T // Q_BLOCK))
    return out.transpose(1, 0, 2, 3, 4).reshape(B, T, H, D)


def _rwkv7_scan(r, decay, k, v, a_vec, b_vec):
    B, T, H, N = r.shape

    def step(S, inp):
        r_t, w_t, k_t, v_t, a_t, b_t = inp
        sa = jnp.einsum('bhvk,bhk->bhv', S, a_t)
        S = S * w_t[:, :, None, :] + sa[..., None] * b_t[:, :, None, :] + v_t[..., None] * k_t[:, :, None, :]
        return S, jnp.einsum('bhvk,bhk->bhv', S, r_t)

    xs = tuple(jnp.moveaxis(t, 1, 0) for t in (r, decay, k, v, a_vec, b_vec))
    _, ys = lax.scan(step, jnp.zeros((B, H, N, N), jnp.float32), xs)
    return jnp.moveaxis(ys, 0, 1)


def _even_mixer(h, w_in, w_out, q_norm, k_norm, lam_p, subln, conv_w, pos, lam_init):
    B, T, _ = h.shape
    q, k, v, gb, gc, u = _split(h @ w_in, EV_SPLITS)
    q = _rope(_rms(q.reshape(B, T, DA_HEADS, 2, DA_HEAD_DIM), q_norm), pos)
    k = _rope(_rms(k.reshape(B, T, DA_HEADS, 2, DA_HEAD_DIM), k_norm), pos)
    lp = lam_p.astype(jnp.float32)
    lam = jnp.exp(jnp.sum(lp[0] * lp[1])) - jnp.exp(jnp.sum(lp[2] * lp[3])) + lam_init
    o = _diff_attention(q, k, v.reshape(B, T, DA_HEADS, 2 * DA_HEAD_DIM), lam)
    o = _rms(o, subln) * (1.0 - lam_init)
    cu = gc * u
    cup = jnp.pad(cu, ((0, 0), (CONV_W - 1, 0), (0, 0)))
    conv = cup[:, 0:T] * conv_w[0]
    for j in range(1, CONV_W):
        conv = conv + cup[:, j:j + T] * conv_w[j]
    y = gb * conv
    mix = jnp.concatenate([o.reshape(B, T, DA_WIDTH), y.astype(o.dtype)], axis=-1) @ w_out
    return mix, v


def _odd_mixer(h, w_in, w_out, mu, w0, w2, a0, a2, v0, v2, g2, k_k, k_a, r_k, lnx_w, lnx_b,
               q_norm, k_norm, idxk_norm, v_first, pos, k_sel):
    B, T, _ = h.shape
    f32 = jnp.float32
    z = h @ w_in
    zr, zd = z[..., :RW_IN], z[..., RW_IN:]
    zr_prev = jnp.pad(zr, ((0, 0), (1, 0), (0, 0)))[:, :-1]
    zr = (zr + (zr_prev - zr) * mu).astype(f32)
    r, k, v, wd, ad, vd, gd = _split(zr, RW_SPLITS)
    logw = -jax.nn.softplus(-(w0 + jnp.tanh(wd) @ w2)) - 0.5
    decay = jnp.exp(-jnp.exp(logw))
    a = jax.nn.sigmoid(a0 + ad @ a2)
    v = v + (v_first.astype(f32) - v) * jax.nn.sigmoid(v0 + vd @ v2)
    g = jax.nn.sigmoid(gd) @ g2
    heads = lambda t: t.reshape(B, T, RW_HEADS, RW_HEAD_DIM)
    per_head = lambda p: p.reshape(RW_HEADS, RW_HEAD_DIM)
    kk = heads(k * k_k)
    kk = kk / jnp.maximum(jnp.sqrt(jnp.sum(kk * kk, axis=-1, keepdims=True)), 1e-12)
    k = k * (1.0 + (a - 1.0) * k_a)
    rh, kh, vh = heads(r), heads(k), heads(v)
    y = _rwkv7_scan(rh, heads(decay), kh, vh, -kk, kk * heads(a))
    mean = jnp.mean(y, axis=-1, keepdims=True)
    var = jnp.mean((y - mean) ** 2, axis=-1, keepdims=True)
    y = (y - mean) * lax.rsqrt(var + LNX_EPS) * per_head(lnx_w) + per_head(lnx_b)
    y = y + jnp.sum(rh * kh * r_k, axis=-1, keepdims=True) * vh
    rw_out = (y.reshape(B, T, RW_WIDTH) * g).astype(h.dtype)
    qd, kd, vdd, qi, ki, wi = _split(zd, SA_SPLITS)
    q = _rope(_rms(qd.reshape(B, T, SA_HEADS, SA_HEAD_DIM), q_norm), pos)
    kd = _rope(_rms(kd, k_norm), pos)
    qi = _rope(qi.reshape(B, T, IDX_HEADS, IDX_DIM), pos)
    ki = _rope(_rms(ki, idxk_norm), pos)
    wi = wi * (IDX_HEADS ** -0.5 * IDX_DIM ** -0.5)
    sa_out = _dsa_attention(q, kd, vdd, qi, ki, wi, k_sel).reshape(B, T, SA_WIDTH)
    return jnp.concatenate([rw_out, sa_out.astype(h.dtype)], axis=-1) @ w_out


def setup_inputs(seed: int = 0) -> dict:
    key = jax.random.key(seed)
    keys = jax.random.split(key, 64)
    it = iter(range(64))
    nk = lambda: keys[next(it)]
    nrm = lambda shape, scale: jax.random.normal(nk(), shape, jnp.float32) * scale
    gain = lambda shape: 1.0 + 0.02 * jax.random.normal(nk(), shape, jnp.float32)
    ne, no = (DEPTH + 1) // 2, DEPTH // 2
    F = FFN_HIDDEN
    return {
        'x': nrm((BATCH, SEQ, D_MODEL), 1.0),
        'mix_norm': gain((DEPTH, D_MODEL)),
        'ffn_norm': gain((DEPTH, D_MODEL)),
        'ffn_gate': nrm((DEPTH, D_MODEL, F), D_MODEL ** -0.5),
        'ffn_up': nrm((DEPTH, D_MODEL, F), D_MODEL ** -0.5),
        'ffn_down': nrm((DEPTH, F, D_MODEL), F ** -0.5),
        'ev_w_in': nrm((ne, D_MODEL, EV_IN), D_MODEL ** -0.5),
        'ev_w_out': nrm((ne, MIX_OUT, D_MODEL), MIX_OUT ** -0.5),
        'da_q_norm': gain((ne, DA_HEAD_DIM)),
        'da_k_norm': gain((ne, DA_HEAD_DIM)),
        'da_lambda': nrm((ne, 4, DA_HEAD_DIM), 0.1),
        'da_subln': gain((ne, 2 * DA_HEAD_DIM)),
        'sc_conv': nrm((ne, CONV_W, SC_WIDTH), CONV_W ** -0.5),
        'od_w_in': nrm((no, D_MODEL, OD_IN), D_MODEL ** -0.5),
        'od_w_out': nrm((no, MIX_OUT, D_MODEL), MIX_OUT ** -0.5),
        'rw_mu': jax.random.uniform(nk(), (no, RW_IN), jnp.float32, 0.0, 1.0),
        'rw_w0': jax.random.uniform(nk(), (no, RW_WIDTH), jnp.float32, -6.0, 0.5),
        'rw_w2': nrm((no, W_LORA, RW_WIDTH), 0.5 * W_LORA ** -0.5),
        'rw_a0': nrm((no, RW_WIDTH), 0.1),
        'rw_a2': nrm((no, A_LORA, RW_WIDTH), 0.5 * A_LORA ** -0.5),
        'rw_v0': gain((no, RW_WIDTH)),
        'rw_v2': nrm((no, V_LORA, RW_WIDTH), 0.5 * V_LORA ** -0.5),
        'rw_g2': nrm((no, G_LORA, RW_WIDTH), G_LORA ** -0.5),
        'rw_k_k': 0.85 + nrm((no, RW_WIDTH), 0.05),
        'rw_k_a': 1.0 + nrm((no, RW_WIDTH), 0.05),
        'rw_r_k': nrm((no, RW_HEADS, RW_HEAD_DIM), 0.1),
        'rw_lnx_w': gain((no, RW_WIDTH)),
        'rw_lnx_b': nrm((no, RW_WIDTH), 0.02),
        'sa_q_norm': gain((no, SA_HEAD_DIM)),
        'sa_k_norm': gain((no, SA_HEAD_DIM)),
        'idx_k_norm': gain((no, IDX_DIM)),
    }


def reference(x, mix_norm, ffn_norm, ffn_gate, ffn_up, ffn_down,
              ev_w_in, ev_w_out, da_q_norm, da_k_norm, da_lambda, da_subln, sc_conv,
              od_w_in, od_w_out, rw_mu, rw_w0, rw_w2, rw_a0, rw_a2, rw_v0, rw_v2, rw_g2,
              rw_k_k, rw_k_a, rw_r_k, rw_lnx_w, rw_lnx_b, sa_q_norm, sa_k_norm, idx_k_norm):
    B, T, _ = x.shape
    pos = jnp.arange(T, dtype=jnp.int32)
    k_sel = min(TOPK_MAX, T // 4)
    v_first = None
    for i in range(DEPTH):
        h = _rms(x, mix_norm[i])
        if i % 2 == 0:
            e = i // 2
            lam_init = 0.8 - 0.6 * math.exp(-0.3 * i)
            mix, v_attn = _even_mixer(h, ev_w_in[e], ev_w_out[e], da_q_norm[e], da_k_norm[e],
                                      da_lambda[e], da_subln[e], sc_conv[e], pos, lam_init)
            if v_first is None:
                v_first = v_attn
        else:
            o = i // 2
            mix = _odd_mixer(h, od_w_in[o], od_w_out[o], rw_mu[o], rw_w0[o], rw_w2[o], rw_a0[o],
                             rw_a2[o], rw_v0[o], rw_v2[o], rw_g2[o], rw_k_k[o], rw_k_a[o], rw_r_k[o],
                             rw_lnx_w[o], rw_lnx_b[o], sa_q_norm[o], sa_k_norm[o], idx_k_norm[o],
                             v_first, pos, k_sel)
        x = x + mix.astype(x.dtype)
        h = _rms(x, ffn_norm[i])
        ffn = (jax.nn.silu(h @ ffn_gate[i]) * (h @ ffn_up[i])) @ ffn_down[i]
        x = x + ffn.astype(x.dtype)
    return x
```

```python
import functools
import math

import jax
import jax.numpy as jnp
from jax import lax
from jax.experimental import pallas as pl
from jax.experimental.pallas import tpu as pltpu

F32 = jnp.float32
BF16 = jnp.bfloat16
I32 = jnp.int32

D_MODEL = 4096
DEPTH = 4
DA_WIDTH = 2048
DA_HEADS = 16
DA_HEAD_DIM = 64
SC_WIDTH = 2048
CONV_W = 3
RW_WIDTH = 2048
RW_HEAD_DIM = 64
RW_HEADS = 32
W_LORA, A_LORA, V_LORA, G_LORA = 96, 96, 64, 256
LNX_EPS = 64e-5
SA_WIDTH = 2048
SA_HEAD_DIM = 128
SA_HEADS = 16
IDX_HEADS = 16
IDX_DIM = 64
TOPK_MAX = 256
FFN_HIDDEN = 11008
ROPE_THETA = 10000.0
EPS = 1e-6

LANES = 128
SUBLANES = 8
V7X_VMEM_BYTES = 64 * 1024 * 1024
VMEM_LIMIT = (V7X_VMEM_BYTES * 3) // 4

FFN_PAD = 11264
OD_R, OD_K, OD_V = 0, 2048, 4096
OD_WD, OD_AD, OD_VD, OD_GD = 6144, 6272, 6400, 6528
OD_RW_END = 6784
OD_Q, OD_KD, OD_VDD, OD_QI, OD_KI, OD_WI = 6784, 8832, 8960, 9088, 10112, 10240
OD_PAD = 10752
NEG = -1e30
INT_MIN = -2 ** 31
NEG_INF_KEY = -2139095041


def _cparams(sem):
    return pltpu.CompilerParams(dimension_semantics=sem, vmem_limit_bytes=VMEM_LIMIT)


def _mm_body(*refs, nk, mode):
    if mode == "swiglu":
        a_ref, b_ref, b2_ref, o_ref, acc_ref, acc2_ref = refs
    elif mode == "resid":
        a_ref, b_ref, r_ref, o_ref, acc_ref = refs
    else:
        a_ref, b_ref, o_ref, acc_ref = refs
    k = pl.program_id(2)

    @pl.when(k == 0)
    def _init():
        acc_ref[...] = jnp.zeros_like(acc_ref)
        if mode == "swiglu":
            acc2_ref[...] = jnp.zeros_like(acc2_ref)

    a = a_ref[...]
    acc_ref[...] += jnp.dot(a, b_ref[...], preferred_element_type=F32)
    if mode == "swiglu":
        acc2_ref[...] += jnp.dot(a, b2_ref[...], preferred_element_type=F32)

    @pl.when(k == nk - 1)
    def _fin():
        if mode == "swiglu":
            g = acc_ref[...]
            o_ref[...] = (g * (1.0 / (1.0 + jnp.exp(-g))) * acc2_ref[...]).astype(o_ref.dtype)
        elif mode == "resid":
            o_ref[...] = r_ref[...] + acc_ref[...]
        else:
            o_ref[...] = acc_ref[...].astype(o_ref.dtype)


def _matmul(a, b, *, b2=None, resid=None, out_dtype=F32, tm=1024, tn=1024, tk=1024, name="mm"):
    M, K = a.shape
    N = b.shape[1]
    tm, tn, tk = min(tm, M), min(tn, N), min(tk, K)
    assert M % tm == 0 and N % tn == 0 and K % tk == 0, (a.shape, b.shape, tm, tn, tk)
    nk = K // tk
    mode = "swiglu" if b2 is not None else ("resid" if resid is not None else "plain")
    in_specs = [pl.BlockSpec((tm, tk), lambda i, j, k: (i, k)),
                pl.BlockSpec((tk, tn), lambda i, j, k: (k, j))]
    args = [a, b]
    scratch = [pltpu.VMEM((tm, tn), F32)]
    if b2 is not None:
        in_specs.append(pl.BlockSpec((tk, tn), lambda i, j, k: (k, j)))
        args.append(b2)
        scratch.append(pltpu.VMEM((tm, tn), F32))
    if resid is not None:
        in_specs.append(pl.BlockSpec((tm, tn), lambda i, j, k: (i, j)))
        args.append(resid)
    return pl.pallas_call(
        functools.partial(_mm_body, nk=nk, mode=mode),
        grid=(M // tm, N // tn, nk),
        in_specs=in_specs,
        out_specs=pl.BlockSpec((tm, tn), lambda i, j, k: (i, j)),
        out_shape=jax.ShapeDtypeStruct((M, N), out_dtype),
        scratch_shapes=scratch,
        compiler_params=_cparams(("parallel", "parallel", "arbitrary")),
        name=name,
    )(*args)


def _rms_body(x_ref, g_ref, o_ref):
    x = x_ref[...]
    ms = jnp.mean(x * x, axis=-1, keepdims=True)
    o_ref[...] = (x * lax.rsqrt(ms + EPS) * g_ref[...]).astype(o_ref.dtype)


def _rmsnorm(x, g, *, tr=256):
    M, D = x.shape
    tr = min(tr, M)
    return pl.pallas_call(
        _rms_body,
        grid=(M // tr,),
        in_specs=[pl.BlockSpec((tr, D), lambda i: (i, 0)),
                  pl.BlockSpec((1, D), lambda i: (0, 0))],
        out_specs=pl.BlockSpec((tr, D), lambda i: (i, 0)),
        out_shape=jax.ShapeDtypeStruct((M, D), BF16),
        compiler_params=_cparams(("parallel",)),
        name="rmsnorm",
    )(x, g.reshape(1, D))


def _rope_tables(T, gs):
    half = gs // 2
    inv = ROPE_THETA ** (-jnp.arange(half, dtype=F32) / half)
    ang = jnp.arange(T, dtype=jnp.int32).astype(F32)[:, None] * inv[None, :]
    cos, sin = jnp.cos(ang), jnp.sin(ang)
    cosg = jnp.concatenate([cos, cos], axis=1)
    sing = jnp.concatenate([-sin, sin], axis=1)
    reps = LANES // gs
    return jnp.tile(cosg, (1, reps)), jnp.tile(sing, (1, reps))


def _group_ones(gs):
    r = jnp.arange(LANES)
    return (r[:, None] // gs == r[None, :] // gs).astype(BF16)


def _seg_body(x_ref, g_ref, cos_ref, sin_ref, bd_ref, o_ref, *, gs, do_norm, do_rope, scale):
    x = x_ref[...]
    if do_norm:
        x2 = x * x
        hi = x2.astype(BF16)
        lo = (x2 - hi.astype(F32)).astype(BF16)
        bd = bd_ref[...]
        ssum = jnp.dot(hi, bd, preferred_element_type=F32) + jnp.dot(lo, bd, preferred_element_type=F32)
        x = x * lax.rsqrt(ssum * (1.0 / gs) + EPS) * g_ref[...]
    if do_rope:
        if gs == 64:
            lane = lax.broadcasted_iota(I32, x.shape, 1)
            rot = jnp.where((lane & 63) < 32, pltpu.roll(x, 96, 1), pltpu.roll(x, 32, 1))
        else:
            rot = pltpu.roll(x, 64, 1)
        x = x * cos_ref[...] + rot * sin_ref[...]
    if scale != 1.0:
        x = x * scale
    o_ref[...] = x.astype(o_ref.dtype)


def _segment(z, col_off, width, gain, tables, T, *, gs, do_norm, do_rope, scale=1.0, out_dtype=BF16, tr=512):
    M = z.shape[0]
    tr = min(tr, T)
    nrb = T // tr
    cb = col_off // LANES
    cos, sin = tables
    if gain is None:
        gain = jnp.ones((gs,), F32)
    gt = jnp.tile(gain.astype(F32), LANES // gs).reshape(1, LANES)
    return pl.pallas_call(
        functools.partial(_seg_body, gs=gs, do_norm=do_norm, do_rope=do_rope, scale=scale),
        grid=(M // tr, width // LANES),
        in_specs=[pl.BlockSpec((tr, LANES), lambda i, j: (i, cb + j)),
                  pl.BlockSpec((1, LANES), lambda i, j: (0, 0)),
                  pl.BlockSpec((tr, LANES), lambda i, j: (i % nrb, 0)),
                  pl.BlockSpec((tr, LANES), lambda i, j: (i % nrb, 0)),
                  pl.BlockSpec((LANES, LANES), lambda i, j: (0, 0))],
        out_specs=pl.BlockSpec((tr, LANES), lambda i, j: (i, j)),
        out_shape=jax.ShapeDtypeStruct((M, width), out_dtype),
        compiler_params=_cparams(("parallel", "parallel")),
        name="segment",
    )(z, gt, cos, sin, _group_ones(gs))


def _online_update(s, mask, vs, m, l, acc):
    s = jnp.where(mask, s, NEG)
    mn = jnp.maximum(m, jnp.max(s, axis=-1, keepdims=True))
    p = jnp.where(mask, jnp.exp(s - mn), 0.0)
    alpha = jnp.exp(m - mn)
    l = alpha * l + jnp.sum(p, axis=-1, keepdims=True)
    acc = alpha * acc + jnp.dot(p.astype(BF16), vs, preferred_element_type=F32)
    return mn, l, acc


def _dot_nt(a, b):
    return lax.dot_general(a, b, (((1,), (1,)), ((), ())), preferred_element_type=F32)


def _dattn_body(lam_ref, sub_ref, q_ref, k_ref, v_ref, o_ref, *, tq, tk, lam_init):
    i = pl.program_id(2)
    lp = lam_ref[...]
    lam = (jnp.exp(jnp.sum(lp[0:1] * lp[1:2], axis=-1, keepdims=True))
           - jnp.exp(jnp.sum(lp[2:3] * lp[3:4], axis=-1, keepdims=True)) + lam_init)
    q = q_ref[...]
    lane = lax.broadcasted_iota(I32, q.shape, 1)
    zero = jnp.zeros_like(q)
    q1 = jnp.where(lane < DA_HEAD_DIM, q, zero)
    q2 = jnp.where(lane >= DA_HEAD_DIM, q, zero)
    row = i * tq + lax.broadcasted_iota(I32, (tq, tk), 0)
    col = lax.broadcasted_iota(I32, (tq, tk), 1)

    def body(j, carry):
        m1, l1, a1, m2, l2, a2 = carry
        off = pl.multiple_of(j * tk, tk)
        ks = k_ref[pl.ds(off, tk), :]
        vs = v_ref[pl.ds(off, tk), :]
        mask = (col + off) <= row
        m1, l1, a1 = _online_update(_dot_nt(q1, ks), mask, vs, m1, l1, a1)
        m2, l2, a2 = _online_update(_dot_nt(q2, ks), mask, vs, m2, l2, a2)
        return m1, l1, a1, m2, l2, a2

    mi = jnp.full((tq, 1), NEG, F32)
    li = jnp.zeros((tq, 1), F32)
    ai = jnp.zeros((tq, LANES), F32)
    nkt = (i * tq + tq + tk - 1) // tk
    m1, l1, a1, m2, l2, a2 = lax.fori_loop(0, nkt, body, (mi, li, ai, mi, li, ai))
    o = a1 * (1.0 / l1) - lam * (a2 * (1.0 / l2))
    ms = jnp.mean(o * o, axis=-1, keepdims=True)
    o = o * lax.rsqrt(ms + EPS) * sub_ref[...] * (1.0 - lam_init)
    o_ref[...] = o.astype(o_ref.dtype)


def _diff_attention(qh, kh, vh, lam_p, subln, lam_init, B, T, *, tq=256, tk=512):
    M = qh.shape[0]
    tq, tk = min(tq, T), min(tk, T)
    nq = T // tq
    return pl.pallas_call(
        functools.partial(_dattn_body, tq=tq, tk=tk, lam_init=lam_init),
        grid=(B, DA_HEADS, nq),
        in_specs=[pl.BlockSpec((4, DA_HEAD_DIM), lambda b, h, i: (0, 0)),
                  pl.BlockSpec((1, LANES), lambda b, h, i: (0, 0)),
                  pl.BlockSpec((tq, LANES), lambda b, h, i: (b * nq + i, h)),
                  pl.BlockSpec((T, LANES), lambda b, h, i: (b, h)),
                  pl.BlockSpec((T, LANES), lambda b, h, i: (b, h))],
        out_specs=pl.BlockSpec((tq, LANES), lambda b, h, i: (b * nq + i, h)),
        out_shape=jax.ShapeDtypeStruct((M, DA_WIDTH), BF16),
        compiler_params=_cparams(("parallel", "parallel", "arbitrary")),
        name="diff_attention",
    )(lam_p.astype(F32), subln.astype(F32).reshape(1, LANES), qh, kh, vh)


def _conv_body(gb_ref, gc_ref, u_ref, w_ref, o_ref, sh_ref, *, T):
    cu = gc_ref[...] * u_ref[...]
    sh_ref[0:SUBLANES, :] = jnp.zeros((SUBLANES, LANES), F32)
    sh_ref[SUBLANES:SUBLANES + T, :] = cu
    w = w_ref[...]
    conv = (sh_ref[SUBLANES - 2:SUBLANES - 2 + T, :] * w[0:1]
            + sh_ref[SUBLANES - 1:SUBLANES - 1 + T, :] * w[1:2]
            + cu * w[2:3])
    o_ref[...] = (gb_ref[...] * conv).astype(o_ref.dtype)


def _short_conv(z, conv_w, B, T):
    M = z.shape[0]
    nc = SC_WIDTH // LANES
    base = 3 * DA_WIDTH // LANES
    return pl.pallas_call(
        functools.partial(_conv_body, T=T),
        grid=(B, nc),
        in_specs=[pl.BlockSpec((T, LANES), lambda b, c: (b, base + c)),
                  pl.BlockSpec((T, LANES), lambda b, c: (b, base + nc + c)),
                  pl.BlockSpec((T, LANES), lambda b, c: (b, base + 2 * nc + c)),
                  pl.BlockSpec((CONV_W, LANES), lambda b, c: (0, c))],
        out_specs=pl.BlockSpec((T, LANES), lambda b, c: (b, c)),
        out_shape=jax.ShapeDtypeStruct((M, SC_WIDTH), BF16),
        scratch_shapes=[pltpu.VMEM((T + SUBLANES, LANES), F32)],
        compiler_params=_cparams(("parallel", "parallel")),
        name="short_conv",
    )(z, z, z, conv_w.astype(F32))


def _sigmoid(x):
    return 1.0 / (1.0 + jnp.exp(-x))


def _rwprep_body(z_ref, zh_ref, vf_ref, mu_ref, w0_ref, a0_ref, v0_ref, w2_ref, a2_ref, v2_ref, g2_ref,
                 r_o, ld_o, k_o, v_o, a_o, g_o, sh_ref, *, tr, nrb):
    i = pl.program_id(0)
    first = (i % nrb) == 0
    sh_ref[SUBLANES - 1:SUBLANES, :] = jnp.where(first, 0.0, zh_ref[SUBLANES - 1:SUBLANES, :])
    sh_ref[SUBLANES:SUBLANES + tr, :] = z_ref[...]

    def shifted(lo, hi):
        zc = z_ref[:, lo:hi]
        zp = sh_ref[SUBLANES - 1:SUBLANES - 1 + tr, lo:hi]
        return zc + (zp - zc) * mu_ref[:, lo:hi]

    r_o[...] = shifted(OD_R, OD_K)
    k_o[...] = shifted(OD_K, OD_V)
    wd = shifted(OD_WD, OD_AD)
    lw = w0_ref[...] + jnp.dot(jnp.tanh(wd).astype(BF16), w2_ref[...], preferred_element_type=F32)
    nlw = -lw
    softplus = jnp.maximum(nlw, 0.0) + jnp.log(1.0 + jnp.exp(-jnp.abs(nlw)))
    ld_o[...] = -jnp.exp(-softplus - 0.5)
    ad = shifted(OD_AD, OD_VD)
    a_o[...] = _sigmoid(a0_ref[...] + jnp.dot(ad.astype(BF16), a2_ref[...], preferred_element_type=F32))
    vd = shifted(OD_VD, OD_GD)
    v = shifted(OD_V, OD_WD)
    vg = _sigmoid(v0_ref[...] + jnp.dot(vd.astype(BF16), v2_ref[...], preferred_element_type=F32))
    v_o[...] = v + (vf_ref[...] - v) * vg
    gd = shifted(OD_GD, OD_RW_END)
    g_o[...] = jnp.dot(_sigmoid(gd).astype(BF16), g2_ref[...], preferred_element_type=F32)


def _pad_rows(w, rows):
    return jnp.pad(w, ((0, rows - w.shape[0]), (0, 0))).astype(BF16)


def _rwkv_prep(z, z_first, mu_p, w0, a0, v0, w2, a2, v2, g2, T, *, tr=128):
    M = z.shape[0]
    tr = min(tr, T)
    nrb = T // tr
    W = OD_RW_END
    hb = tr // SUBLANES
    row = lambda a: a.astype(F32).reshape(1, RW_WIDTH)
    full = lambda shape: pl.BlockSpec(shape, lambda i: (0, 0))
    out = jax.ShapeDtypeStruct((M, RW_WIDTH), F32)
    ospec = pl.BlockSpec((tr, RW_WIDTH), lambda i: (i, 0))
    return pl.pallas_call(
        functools.partial(_rwprep_body, tr=tr, nrb=nrb),
        grid=(M // tr,),
        in_specs=[pl.BlockSpec((tr, W), lambda i: (i, 0)),
                  pl.BlockSpec((SUBLANES, W), lambda i: (jnp.maximum(i * hb - 1, 0), 0)),
                  pl.BlockSpec((tr, RW_WIDTH), lambda i: (i, 2 * DA_WIDTH // RW_WIDTH)),
                  full((1, W)), full((1, RW_WIDTH)), full((1, RW_WIDTH)), full((1, RW_WIDTH)),
                  full((LANES, RW_WIDTH)), full((LANES, RW_WIDTH)), full((LANES, RW_WIDTH)),
                  full((G_LORA, RW_WIDTH))],
        out_specs=[ospec] * 6,
        out_shape=[out] * 6,
        scratch_shapes=[pltpu.VMEM((tr + SUBLANES, W), F32)],
        compiler_params=_cparams(("parallel",)),
        name="rwkv_prep",
    )(z, z, z_first, mu_p, row(w0), row(a0), row(v0),
      _pad_rows(w2, LANES), _pad_rows(a2, LANES), _pad_rows(v2, LANES), g2.astype(BF16))


def _split3(x):
    hi = x.astype(BF16)
    r1 = x - hi.astype(F32)
    mid = r1.astype(BF16)
    lo = (r1 - mid.astype(F32)).astype(BF16)
    return hi, mid, lo


def _mm(a, b):
    return jnp.dot(a.astype(BF16), b.astype(BF16), preferred_element_type=F32)


def _mm_tn(a, b):
    return jnp.dot(a.T.astype(BF16), b.astype(BF16), preferred_element_type=F32)


def _rwkv_body(r_ref, ld_ref, k_ref, v_ref, a_ref, g_ref, kk_ref, ka_ref, rk_ref, lw_ref, lb_ref,
               o_ref, st_ref, *, hb, L):
    c = pl.program_id(2)

    @pl.when(c == 0)
    def _init():
        st_ref[...] = jnp.zeros_like(st_ref)

    N = RW_HEAD_DIM
    ri = lax.broadcasted_iota(I32, (L, L), 0)
    ci = lax.broadcasted_iota(I32, (L, L), 1)
    incl = ci <= ri
    strict = ci < ri
    tri = incl.astype(BF16)
    eye_l = (ci == ri).astype(F32)
    blk16 = (ri // 16) == (ci // 16)
    off16 = ((ri // 16) == (ci // 16) + 1) & ((ri // 32) == (ci // 32))
    off32 = (ri // 32) > (ci // 32)
    rn = lax.broadcasted_iota(I32, (N, N), 0)
    cn = lax.broadcasted_iota(I32, (N, N), 1)
    eye_n = rn == cn

    for h in range(hb):
        sl = slice(h * N, (h + 1) * N)
        r = r_ref[:, sl]
        ld = ld_ref[:, sl]
        k = k_ref[:, sl]
        v = v_ref[:, sl]
        a = a_ref[:, sl]
        kk = k * kk_ref[:, sl]
        kk = kk / jnp.maximum(jnp.sqrt(jnp.sum(kk * kk, axis=-1, keepdims=True)), 1e-12)
        k2 = k * (1.0 + (a - 1.0) * ka_ref[:, sl])
        av = -kk
        bv = kk * a
        hi, mid, lo = _split3(ld)
        cum = (jnp.dot(tri, hi, preferred_element_type=F32) + jnp.dot(tri, mid, preferred_element_type=F32)
               + jnp.dot(tri, lo, preferred_element_type=F32))
        clast = cum[L - 1:L, :]
        e_neg = jnp.exp(-cum)
        e_l = jnp.exp(clast - cum)
        p_l = jnp.exp(clast)
        at = av * jnp.exp(cum - ld)
        rt = r * jnp.exp(cum)
        bt = bv * e_neg
        kt = k2 * e_neg
        bh = bv * e_l
        kh = k2 * e_l
        atb, rtb, btb, ktb = at.astype(BF16), rt.astype(BF16), bt.astype(BF16), kt.astype(BF16)
        mab = jnp.where(strict, _dot_nt(atb, btb), 0.0)
        mak = jnp.where(strict, _dot_nt(atb, ktb), 0.0)
        mrb = jnp.where(incl, _dot_nt(rtb, btb), 0.0)
        mrk = jnp.where(incl, _dot_nt(rtb, ktb), 0.0)
        nd = jnp.where(blk16, mab, 0.0)
        n2 = _mm(nd, nd)
        n4 = _mm(n2, n2)
        n8 = _mm(n4, n4)
        t = eye_l + nd
        t = t + _mm(t, n2)
        t = t + _mm(t, n4)
        t = t + _mm(t, n8)
        t = t + _mm(_mm(t, jnp.where(off16, mab, 0.0)), t)
        t = t + _mm(_mm(t, jnp.where(off32, mab, 0.0)), t)
        wm = _mm(t, at)
        ul = _mm(t, _mm(mak, v))
        qe = rt + _mm(mrb, wm)
        yl = _mm(mrb, ul) + _mm(mrk, v)
        gm = _mm_tn(bh, wm) + jnp.where(eye_n, p_l, 0.0)
        hm = _mm_tn(bh, ul) + _mm_tn(kh, v)
        st = st_ref[h]
        y = _mm(qe, st) + yl
        st_ref[h] = _mm(gm, st) + hm
        mean = jnp.mean(y, axis=-1, keepdims=True)
        d = y - mean
        var = jnp.mean(d * d, axis=-1, keepdims=True)
        yn = d * lax.rsqrt(var + LNX_EPS) * lw_ref[:, sl] + lb_ref[:, sl]
        yn = yn + jnp.sum(r * k2 * rk_ref[:, sl], axis=-1, keepdims=True) * v
        o_ref[:, sl] = (yn * g_ref[:, sl]).astype(o_ref.dtype)


def _rwkv(r, ld, k, v, a, g, k_k, k_a, r_k, lnx_w, lnx_b, B, T, *, hb=8, L=64):
    M = r.shape[0]
    L = min(L, T)
    nc = T // L
    W = hb * RW_HEAD_DIM
    blk = pl.BlockSpec((L, W), lambda b, hg, c: (b * nc + c, hg))
    par = pl.BlockSpec((1, W), lambda b, hg, c: (0, hg))
    row = lambda p: p.astype(F32).reshape(1, RW_WIDTH)
    return pl.pallas_call(
        functools.partial(_rwkv_body, hb=hb, L=L),
        grid=(B, RW_HEADS // hb, nc),
        in_specs=[blk] * 6 + [par] * 5,
        out_specs=blk,
        out_shape=jax.ShapeDtypeStruct((M, RW_WIDTH), BF16),
        scratch_shapes=[pltpu.VMEM((hb, RW_HEAD_DIM, RW_HEAD_DIM), F32)],
        compiler_params=_cparams(("parallel", "parallel", "arbitrary")),
        name="rwkv7_chunk",
    )(r, ld, k, v, a, g, row(k_k), row(k_a), row(r_k), row(lnx_w), row(lnx_b))


def _dsa_body(qd_ref, qi_ref, wi_ref, kd_ref, vd_ref, ki_ref, o_ref,
              keys_ref, cut_ref, m_ref, l_ref, acc_ref, *, tq, tk, ksel, T):
    i = pl.program_id(1)
    nkt = (i * tq + tq + tk - 1) // tk
    row = i * tq + lax.broadcasted_iota(I32, (tq, tk), 0)
    col = lax.broadcasted_iota(I32, (tq, tk), 1)
    lane = lax.broadcasted_iota(I32, (tq, LANES), 1)
    low_half = lane < IDX_DIM
    wi = wi_ref[...]

    def score_tile(j, carry):
        off = pl.multiple_of(j * tk, tk)
        kt = ki_ref[pl.ds(off, tk), :]
        acc = jnp.zeros((tq, tk), F32)
        for h in range(IDX_HEADS):
            qt = qi_ref[:, (h // 2) * LANES:(h // 2 + 1) * LANES]
            qm = jnp.where(low_half if h % 2 == 0 else jnp.logical_not(low_half), qt, jnp.zeros_like(qt))
            acc = acc + jnp.maximum(_dot_nt(qm, kt), 0.0) * wi[:, h:h + 1]
        acc = acc + 0.0
        sc = jnp.where((col + off) <= row, acc, -jnp.inf)
        bits = pltpu.bitcast(sc, I32)
        keys_ref[:, pl.ds(off, tk)] = bits ^ ((bits >> 31) & 0x7FFFFFFF)
        return carry

    lax.fori_loop(0, nkt, score_tile, 0)

    def count(pred):
        def body(j, c):
            off = pl.multiple_of(j * tk, tk)
            hit = jnp.where(pred(keys_ref[:, pl.ds(off, tk)], col + off), 1.0, 0.0)
            for s in range(tk // LANES):
                c = c + hit[:, s * LANES:(s + 1) * LANES]
            return c
        c = lax.fori_loop(0, nkt, body, jnp.zeros((tq, LANES), F32))
        return jnp.sum(c, axis=-1, keepdims=True)

    def bit_step(b, thr):
        cand = thr + jnp.left_shift(jnp.int32(1), 31 - b)
        cnt = count(lambda key, _: key >= cand)
        return jnp.where(cnt >= ksel, cand, thr)

    thr = lax.fori_loop(0, 32, bit_step, jnp.full((tq, 1), INT_MIN, I32))

    n_gt = count(lambda key, _: key > thr)
    n_ge = count(lambda key, _: key >= thr)
    need = (n_ge > ksel) & (thr > NEG_INF_KEY)
    quota = ksel - n_gt
    cut_ref[...] = jnp.full((tq, LANES), T, I32)

    @pl.when(jnp.max(jnp.where(need, 1.0, 0.0)) > 0.0)
    def _ties():
        def pos_step(b, p):
            cand = p + jnp.left_shift(jnp.int32(1), (T.bit_length() - 1) - b)
            cnt = count(lambda key, c: (key == thr) & (c < cand))
            return jnp.where(cnt < quota, cand, p)
        p = lax.fori_loop(0, T.bit_length(), pos_step, jnp.zeros((tq, 1), I32))
        cut_ref[...] = jnp.broadcast_to(jnp.where(need, p, T), (tq, LANES))

    cut = cut_ref[:, 0:1]

    m_ref[...] = jnp.full(m_ref.shape, NEG, F32)
    l_ref[...] = jnp.zeros(l_ref.shape, F32)
    acc_ref[...] = jnp.zeros(acc_ref.shape, F32)

    def attend(j, carry):
        off = pl.multiple_of(j * tk, tk)
        ks = kd_ref[pl.ds(off, tk), :]
        vs = vd_ref[pl.ds(off, tk), :]
        key = keys_ref[:, pl.ds(off, tk)]
        pos = col + off
        sel = ((key > thr) | ((key == thr) & (pos <= cut))) & (pos <= row)
        for h in range(SA_HEADS):
            q = qd_ref[:, h * LANES:(h + 1) * LANES]
            m, l, acc = _online_update(_dot_nt(q, ks), sel, vs, m_ref[h][:, 0:1], l_ref[h][:, 0:1], acc_ref[h])
            m_ref[h] = jnp.broadcast_to(m, (tq, LANES))
            l_ref[h] = jnp.broadcast_to(l, (tq, LANES))
            acc_ref[h] = acc
        return carry

    lax.fori_loop(0, nkt, attend, 0)
    for h in range(SA_HEADS):
        o_ref[:, h * LANES:(h + 1) * LANES] = (acc_ref[h] * (1.0 / l_ref[h][:, 0:1])).astype(o_ref.dtype)


def _dsa(qd, qi, wi, kd, vd, ki, B, T, ksel, *, tq=128, tk=512):
    M = qd.shape[0]
    tq, tk = min(tq, T), min(tk, T)
    nq = T // tq
    qblk = lambda w: pl.BlockSpec((tq, w), lambda b, i: (b * nq + i, 0))
    kblk = pl.BlockSpec((T, LANES), lambda b, i: (b, 0))
    return pl.pallas_call(
        functools.partial(_dsa_body, tq=tq, tk=tk, ksel=ksel, T=T),
        grid=(B, nq),
        in_specs=[qblk(SA_WIDTH), qblk(IDX_HEADS * IDX_DIM), qblk(LANES), kblk, kblk, kblk],
        out_specs=qblk(SA_WIDTH),
        out_shape=jax.ShapeDtypeStruct((M, SA_WIDTH), BF16),
        scratch_shapes=[pltpu.VMEM((tq, T), I32), pltpu.VMEM((tq, LANES), I32),
                        pltpu.VMEM((SA_HEADS, tq, LANES), F32), pltpu.VMEM((SA_HEADS, tq, LANES), F32),
                        pltpu.VMEM((SA_HEADS, tq, LANES), F32)],
        compiler_params=_cparams(("parallel", "arbitrary")),
        name="dsa_attention",
    )(qd, qi, wi, kd, vd, ki)


def _pad_cols(w, width):
    return jnp.pad(w, ((0, 0), (0, width - w.shape[1])))


def _odd_in_weight(w):
    rw_in = 3 * RW_WIDTH + W_LORA + A_LORA + V_LORA + G_LORA
    o = 3 * RW_WIDTH
    segs = [w[:, :o],
            _pad_cols(w[:, o:o + W_LORA], LANES),
            _pad_cols(w[:, o + W_LORA:o + W_LORA + A_LORA], LANES),
            _pad_cols(w[:, o + W_LORA + A_LORA:o + W_LORA + A_LORA + V_LORA], LANES),
            w[:, rw_in - G_LORA:rw_in]]
    d = rw_in
    q = w[:, d:d + SA_WIDTH]
    kd = w[:, d + SA_WIDTH:d + SA_WIDTH + SA_HEAD_DIM]
    vd = w[:, d + SA_WIDTH + SA_HEAD_DIM:d + SA_WIDTH + 2 * SA_HEAD_DIM]
    d2 = d + SA_WIDTH + 2 * SA_HEAD_DIM
    qi = w[:, d2:d2 + IDX_HEADS * IDX_DIM]
    ki = w[:, d2 + IDX_HEADS * IDX_DIM:d2 + IDX_HEADS * IDX_DIM + IDX_DIM]
    wi = w[:, d2 + IDX_HEADS * IDX_DIM + IDX_DIM:]
    segs += [q, kd, vd, qi, ki, ki, _pad_cols(wi, LANES)]
    return _pad_cols(jnp.concatenate(segs, axis=1), OD_PAD).astype(BF16)


def _odd_mu(mu):
    o = 3 * RW_WIDTH
    segs = [mu[:o],
            jnp.pad(mu[o:o + W_LORA], (0, LANES - W_LORA)),
            jnp.pad(mu[o + W_LORA:o + W_LORA + A_LORA], (0, LANES - A_LORA)),
            jnp.pad(mu[o + W_LORA + A_LORA:o + W_LORA + A_LORA + V_LORA], (0, LANES - V_LORA)),
            mu[o + W_LORA + A_LORA + V_LORA:]]
    return jnp.concatenate(segs).astype(F32).reshape(1, OD_RW_END)


def _even_mixer(xf, h, w_in, w_out, q_norm, k_norm, lam_p, subln, conv_w, tabs64, lam_init, B, T):
    z = _matmul(h, w_in, name="even_in")
    qh = _segment(z, 0, DA_WIDTH, q_norm, tabs64, T, gs=64, do_norm=True, do_rope=True,
                  scale=DA_HEAD_DIM ** -0.5)
    kh = _segment(z, DA_WIDTH, DA_WIDTH, k_norm, tabs64, T, gs=64, do_norm=True, do_rope=True)
    vh = _segment(z, 2 * DA_WIDTH, DA_WIDTH, None, tabs64, T, gs=64, do_norm=False, do_rope=False)
    o = _diff_attention(qh, kh, vh, lam_p, subln, lam_init, B, T)
    y = _short_conv(z, conv_w, B, T)
    cat = jnp.concatenate([o, y], axis=1)
    return _matmul(cat, w_out, resid=xf, name="even_out"), z


def _odd_mixer(xf, h, w_in_p, w_out, mu_p, w0, w2, a0, a2, v0, v2, g2, k_k, k_a, r_k, lnx_w, lnx_b,
               q_norm, k_norm, idxk_norm, z_first, tabs64, tabs128, B, T, ksel):
    z = _matmul(h, w_in_p, tn=512, name="odd_in")
    r, ld, k, v, a, g = _rwkv_prep(z, z_first, mu_p, w0, a0, v0, w2, a2, v2, g2, T)
    rw_out = _rwkv(r, ld, k, v, a, g, k_k, k_a, r_k.reshape(-1), lnx_w, lnx_b, B, T)
    qd = _segment(z, OD_Q, SA_WIDTH, q_norm, tabs128, T, gs=128, do_norm=True, do_rope=True,
                  scale=SA_HEAD_DIM ** -0.5)
    kd = _segment(z, OD_KD, LANES, k_norm, tabs128, T, gs=128, do_norm=True, do_rope=True)
    vd = _segment(z, OD_VDD, LANES, None, tabs128, T, gs=128, do_norm=False, do_rope=False)
    qi = _segment(z, OD_QI, IDX_HEADS * IDX_DIM, None, tabs64, T, gs=64, do_norm=False, do_rope=True)
    ki = _segment(z, OD_KI, LANES, idxk_norm, tabs64, T, gs=64, do_norm=True, do_rope=True)
    wi = _segment(z, OD_WI, LANES, None, tabs64, T, gs=64, do_norm=False, do_rope=False,
                  scale=IDX_HEADS ** -0.5 * IDX_DIM ** -0.5, out_dtype=F32)
    sa_out = _dsa(qd, qi, wi, kd, vd, ki, B, T, ksel)
    cat = jnp.concatenate([rw_out, sa_out], axis=1)
    return _matmul(cat, w_out, resid=xf, name="odd_out")


def kernel(x, mix_norm, ffn_norm, ffn_gate, ffn_up, ffn_down, ev_w_in, ev_w_out, da_q_norm, da_k_norm, da_lambda, da_subln, sc_conv, od_w_in, od_w_out, rw_mu, rw_w0, rw_w2, rw_a0, rw_a2, rw_v0, rw_v2, rw_g2, rw_k_k, rw_k_a, rw_r_k, rw_lnx_w, rw_lnx_b, sa_q_norm, sa_k_norm, idx_k_norm):
    B, T, D = x.shape
    M = B * T
    ksel = min(TOPK_MAX, T // 4)
    xf = x.reshape(M, D)
    tabs64 = _rope_tables(T, 64)
    tabs128 = _rope_tables(T, 128)
    fpad = FFN_PAD - FFN_HIDDEN
    z_first = None
    for i in range(DEPTH):
        h = _rmsnorm(xf, mix_norm[i])
        if i % 2 == 0:
            e = i // 2
            lam_init = 0.8 - 0.6 * math.exp(-0.3 * i)
            xf, z = _even_mixer(xf, h, ev_w_in[e].astype(BF16), ev_w_out[e].astype(BF16), da_q_norm[e],
                                da_k_norm[e], da_lambda[e], da_subln[e], sc_conv[e], tabs64, lam_init, B, T)
            if z_first is None:
                z_first = z
        else:
            o = i // 2
            xf = _odd_mixer(xf, h, _odd_in_weight(od_w_in[o]), od_w_out[o].astype(BF16), _odd_mu(rw_mu[o]),
                            rw_w0[o], rw_w2[o], rw_a0[o], rw_a2[o], rw_v0[o], rw_v2[o], rw_g2[o],
                            rw_k_k[o], rw_k_a[o], rw_r_k[o], rw_lnx_w[o], rw_lnx_b[o],
                            sa_q_norm[o], sa_k_norm[o], idx_k_norm[o], z_first, tabs64, tabs128, B, T, ksel)
        h = _rmsnorm(xf, ffn_norm[i])
        wg = jnp.pad(ffn_gate[i], ((0, 0), (0, fpad))).astype(BF16)
        wu = jnp.pad(ffn_up[i], ((0, 0), (0, fpad))).astype(BF16)
        wd = jnp.pad(ffn_down[i], ((0, fpad), (0, 0))).astype(BF16)
        hid = _matmul(h, wg, b2=wu, out_dtype=BF16, name="ffn_in")
        xf = _matmul(hid, wd, resid=xf, name="ffn_out")
    return xf.reshape(B, T, D)
```

```python
import functools
import math

import jax
import jax.numpy as jnp
from jax import lax
from jax.experimental import pallas as pl
from jax.experimental.pallas import tpu as pltpu

F32 = jnp.float32
BF16 = jnp.bfloat16
I32 = jnp.int32

D_MODEL = 4096
DEPTH = 4
DA_WIDTH = 2048
DA_HEADS = 16
DA_HEAD_DIM = 64
SC_WIDTH = 2048
CONV_W = 3
RW_WIDTH = 2048
RW_HEAD_DIM = 64
RW_HEADS = 32
RW_GROUP = 4
RW_GW = RW_GROUP * RW_HEAD_DIM
W_LORA, A_LORA, V_LORA, G_LORA = 96, 96, 64, 256
LNX_EPS = 64e-5
SA_WIDTH = 2048
SA_HEAD_DIM = 128
SA_HEADS = 16
IDX_HEADS = 16
IDX_DIM = 64
TOPK_MAX = 256
FFN_HIDDEN = 11008
ROPE_THETA = 10000.0
EPS = 1e-6

LANES = 128
SUBLANES = 8
V7X_VMEM_BYTES = 64 * 1024 * 1024
VMEM_LIMIT = (V7X_VMEM_BYTES * 3) // 4

FFN_PAD = 11264
OD_R, OD_K, OD_V = 0, 2048, 4096
OD_WD, OD_AD, OD_VD, OD_GD = 6144, 6272, 6400, 6528
OD_RW_END = 6784
OD_Q, OD_KD, OD_VDD, OD_QI, OD_KI, OD_WI = 6784, 8832, 8960, 9088, 10112, 10240
OD_PAD = 10752
LOG2E = math.log2(math.e)
NEG = -1e30
INT_MIN = -2 ** 31
NEG_INF_KEY = -2139095041


def _cparams(sem):
    return pltpu.CompilerParams(dimension_semantics=sem, vmem_limit_bytes=VMEM_LIMIT)


def _mm_body(*refs, nk, mode):
    if mode == "swiglu":
        a_ref, b_ref, b2_ref, o_ref, acc_ref, acc2_ref = refs
    elif mode == "resid":
        a_ref, b_ref, r_ref, o_ref, acc_ref = refs
    else:
        a_ref, b_ref, o_ref, acc_ref = refs
    k = pl.program_id(2)

    @pl.when(k == 0)
    def _init():
        acc_ref[...] = jnp.zeros_like(acc_ref)
        if mode == "swiglu":
            acc2_ref[...] = jnp.zeros_like(acc2_ref)

    a = a_ref[...]
    acc_ref[...] += jnp.dot(a, b_ref[...], preferred_element_type=F32)
    if mode == "swiglu":
        acc2_ref[...] += jnp.dot(a, b2_ref[...], preferred_element_type=F32)

    @pl.when(k == nk - 1)
    def _fin():
        if mode == "swiglu":
            g = acc_ref[...]
            o_ref[...] = (g * (1.0 / (1.0 + jnp.exp(-g))) * acc2_ref[...]).astype(o_ref.dtype)
        elif mode == "resid":
            o_ref[...] = r_ref[...] + acc_ref[...]
        else:
            o_ref[...] = acc_ref[...].astype(o_ref.dtype)


def _matmul(a, b, *, b2=None, resid=None, out_dtype=F32, tm=1024, tn=1024, tk=1024, name="mm"):
    M, K = a.shape
    N = b.shape[1]
    tm, tn, tk = min(tm, M), min(tn, N), min(tk, K)
    assert M % tm == 0 and N % tn == 0 and K % tk == 0, (a.shape, b.shape, tm, tn, tk)
    nk = K // tk
    mode = "swiglu" if b2 is not None else ("resid" if resid is not None else "plain")
    in_specs = [pl.BlockSpec((tm, tk), lambda i, j, k: (i, k)),
                pl.BlockSpec((tk, tn), lambda i, j, k: (k, j))]
    args = [a, b]
    scratch = [pltpu.VMEM((tm, tn), F32)]
    if b2 is not None:
        in_specs.append(pl.BlockSpec((tk, tn), lambda i, j, k: (k, j)))
        args.append(b2)
        scratch.append(pltpu.VMEM((tm, tn), F32))
    if resid is not None:
        in_specs.append(pl.BlockSpec((tm, tn), lambda i, j, k: (i, j)))
        args.append(resid)
    return pl.pallas_call(
        functools.partial(_mm_body, nk=nk, mode=mode),
        grid=(M // tm, N // tn, nk),
        in_specs=in_specs,
        out_specs=pl.BlockSpec((tm, tn), lambda i, j, k: (i, j)),
        out_shape=jax.ShapeDtypeStruct((M, N), out_dtype),
        scratch_shapes=scratch,
        compiler_params=_cparams(("parallel", "parallel", "arbitrary")),
        name=name,
    )(*args)


def _rms_body(x_ref, g_ref, o_ref):
    x = x_ref[...]
    ms = jnp.mean(x * x, axis=-1, keepdims=True)
    o_ref[...] = (x * lax.rsqrt(ms + EPS) * g_ref[...]).astype(o_ref.dtype)


def _rmsnorm(x, g, *, tr=256):
    M, D = x.shape
    tr = min(tr, M)
    return pl.pallas_call(
        _rms_body,
        grid=(M // tr,),
        in_specs=[pl.BlockSpec((tr, D), lambda i: (i, 0)),
                  pl.BlockSpec((1, D), lambda i: (0, 0))],
        out_specs=pl.BlockSpec((tr, D), lambda i: (i, 0)),
        out_shape=jax.ShapeDtypeStruct((M, D), BF16),
        compiler_params=_cparams(("parallel",)),
        name="rmsnorm",
    )(x, g.reshape(1, D))


def _rope_tables(T, gs):
    half = gs // 2
    inv = ROPE_THETA ** (-jnp.arange(half, dtype=F32) / half)
    ang = jnp.arange(T, dtype=jnp.int32).astype(F32)[:, None] * inv[None, :]
    cos, sin = jnp.cos(ang), jnp.sin(ang)
    cosg = jnp.concatenate([cos, cos], axis=1)
    sing = jnp.concatenate([-sin, sin], axis=1)
    reps = LANES // gs
    return jnp.tile(cosg, (1, reps)), jnp.tile(sing, (1, reps))


def _group_ones(gs):
    r = jnp.arange(LANES)
    return (r[:, None] // gs == r[None, :] // gs).astype(BF16)


def _seg_body(x_ref, g_ref, cos_ref, sin_ref, bd_ref, o_ref, *, gs, do_norm, do_rope, scale):
    x = x_ref[...]
    if do_norm:
        x2 = x * x
        hi = x2.astype(BF16)
        lo = (x2 - hi.astype(F32)).astype(BF16)
        bd = bd_ref[...]
        ssum = jnp.dot(hi, bd, preferred_element_type=F32) + jnp.dot(lo, bd, preferred_element_type=F32)
        x = x * lax.rsqrt(ssum * (1.0 / gs) + EPS) * g_ref[...]
    if do_rope:
        if gs == 64:
            lane = lax.broadcasted_iota(I32, x.shape, 1)
            rot = jnp.where((lane & 63) < 32, pltpu.roll(x, 96, 1), pltpu.roll(x, 32, 1))
        else:
            rot = pltpu.roll(x, 64, 1)
        x = x * cos_ref[...] + rot * sin_ref[...]
    if scale != 1.0:
        x = x * scale
    o_ref[...] = x.astype(o_ref.dtype)


def _segment(z, col_off, width, gain, tables, T, *, gs, do_norm, do_rope, scale=1.0, out_dtype=BF16, tr=512):
    M = z.shape[0]
    tr = min(tr, T)
    nrb = T // tr
    cb = col_off // LANES
    cos, sin = tables
    if gain is None:
        gain = jnp.ones((gs,), F32)
    gt = jnp.tile(gain.astype(F32), LANES // gs).reshape(1, LANES)
    return pl.pallas_call(
        functools.partial(_seg_body, gs=gs, do_norm=do_norm, do_rope=do_rope, scale=scale),
        grid=(M // tr, width // LANES),
        in_specs=[pl.BlockSpec((tr, LANES), lambda i, j: (i, cb + j)),
                  pl.BlockSpec((1, LANES), lambda i, j: (0, 0)),
                  pl.BlockSpec((tr, LANES), lambda i, j: (i % nrb, 0)),
                  pl.BlockSpec((tr, LANES), lambda i, j: (i % nrb, 0)),
                  pl.BlockSpec((LANES, LANES), lambda i, j: (0, 0))],
        out_specs=pl.BlockSpec((tr, LANES), lambda i, j: (i, j)),
        out_shape=jax.ShapeDtypeStruct((M, width), out_dtype),
        compiler_params=_cparams(("parallel", "parallel")),
        name="segment",
    )(z, gt, cos, sin, _group_ones(gs))


def _online_update(s, mask, vs, m, l, acc):
    s = jnp.where(mask, s, NEG)
    mn = jnp.maximum(m, jnp.max(s, axis=-1, keepdims=True))
    p = jnp.where(mask, jnp.exp(s - mn), 0.0)
    alpha = jnp.exp(m - mn)
    l = alpha * l + jnp.sum(p, axis=-1, keepdims=True)
    acc = alpha * acc + jnp.dot(p.astype(BF16), vs, preferred_element_type=F32)
    return mn, l, acc


def _dot_nt(a, b):
    return lax.dot_general(a, b, (((1,), (1,)), ((), ())), preferred_element_type=F32)


def _dattn_body(lam_ref, sub_ref, q_ref, k_ref, v_ref, o_ref, m_ref, l_ref, acc_ref, *, tq, tk, lam_init):
    i = pl.program_id(2)
    q = q_ref[...]
    lane = lax.broadcasted_iota(I32, q.shape, 1)
    zero = jnp.zeros_like(q)
    qs = (jnp.where(lane < DA_HEAD_DIM, q, zero), jnp.where(lane >= DA_HEAD_DIM, q, zero))
    m_ref[...] = jnp.full(m_ref.shape, -jnp.inf, F32)
    l_ref[...] = jnp.zeros(l_ref.shape, F32)
    acc_ref[...] = jnp.zeros(acc_ref.shape, F32)
    reps = tk // LANES

    def tile_step(j, masked):
        off = pl.multiple_of(j * tk, tk)
        ks = k_ref[pl.ds(off, tk), :]
        vs = v_ref[pl.ds(off, tk), :]
        if masked:
            row = i * tq + lax.broadcasted_iota(I32, (tq, tk), 0)
            col = off + lax.broadcasted_iota(I32, (tq, tk), 1)
            vis = col <= row
        for c in range(2):
            s = _dot_nt(qs[c], ks)
            if masked:
                s = jnp.where(vis, s, -jnp.inf)
            m_prev = m_ref[c]
            m_next = jnp.maximum(m_prev, jnp.max(s, axis=-1, keepdims=True))
            p = jnp.exp2(s - jnp.concatenate([m_next] * reps, axis=1))
            alpha = jnp.exp2(m_prev - m_next)
            l_ref[c] = alpha * l_ref[c] + jnp.sum(p, axis=-1, keepdims=True)
            acc_ref[c] = alpha * acc_ref[c] + jnp.dot(p.astype(BF16), vs, preferred_element_type=F32)
            m_ref[c] = m_next

    n_full = (i * tq + 1) // tk
    n_all = (i * tq + tq + tk - 1) // tk

    def full_body(j, carry):
        tile_step(j, False)
        return carry

    def diag_body(j, carry):
        tile_step(j, True)
        return carry

    lax.fori_loop(0, n_full, full_body, 0)
    lax.fori_loop(n_full, n_all, diag_body, 0)

    lp = lam_ref[...]
    lam = (jnp.exp(jnp.sum(lp[0:1] * lp[1:2], axis=-1, keepdims=True))
           - jnp.exp(jnp.sum(lp[2:3] * lp[3:4], axis=-1, keepdims=True)) + lam_init)
    o = acc_ref[0] * (1.0 / l_ref[0]) - lam * (acc_ref[1] * (1.0 / l_ref[1]))
    ms = jnp.mean(o * o, axis=-1, keepdims=True)
    o = o * lax.rsqrt(ms + EPS) * sub_ref[...] * (1.0 - lam_init)
    o_ref[...] = o.astype(o_ref.dtype)


def _diff_attention(qh, kh, vh, lam_p, subln, lam_init, B, T, *, tq=256, tk=512):
    M = qh.shape[0]
    tq, tk = min(tq, T), min(tk, T)
    nq = T // tq
    return pl.pallas_call(
        functools.partial(_dattn_body, tq=tq, tk=tk, lam_init=lam_init),
        grid=(B, DA_HEADS, nq),
        in_specs=[pl.BlockSpec((4, DA_HEAD_DIM), lambda b, h, i: (0, 0)),
                  pl.BlockSpec((1, LANES), lambda b, h, i: (0, 0)),
                  pl.BlockSpec((tq, LANES), lambda b, h, i: (b * nq + i, h)),
                  pl.BlockSpec((T, LANES), lambda b, h, i: (b, h)),
                  pl.BlockSpec((T, LANES), lambda b, h, i: (b, h))],
        out_specs=pl.BlockSpec((tq, LANES), lambda b, h, i: (b * nq + i, h)),
        out_shape=jax.ShapeDtypeStruct((M, DA_WIDTH), BF16),
        scratch_shapes=[pltpu.VMEM((2, tq, LANES), F32)] * 3,
        compiler_params=_cparams(("parallel", "parallel", "arbitrary")),
        name="diff_attention",
    )(lam_p.astype(F32), subln.astype(F32).reshape(1, LANES), qh, kh, vh)


def _conv_body(gb_ref, gc_ref, u_ref, w_ref, o_ref, sh_ref, *, T):
    cu = gc_ref[...] * u_ref[...]
    sh_ref[0:SUBLANES, :] = jnp.zeros((SUBLANES, LANES), F32)
    sh_ref[SUBLANES:SUBLANES + T, :] = cu
    w = w_ref[...]
    conv = (sh_ref[SUBLANES - 2:SUBLANES - 2 + T, :] * w[0:1]
            + sh_ref[SUBLANES - 1:SUBLANES - 1 + T, :] * w[1:2]
            + cu * w[2:3])
    o_ref[...] = (gb_ref[...] * conv).astype(o_ref.dtype)


def _short_conv(z, conv_w, B, T):
    M = z.shape[0]
    nc = SC_WIDTH // LANES
    base = 3 * DA_WIDTH // LANES
    return pl.pallas_call(
        functools.partial(_conv_body, T=T),
        grid=(B, nc),
        in_specs=[pl.BlockSpec((T, LANES), lambda b, c: (b, base + c)),
                  pl.BlockSpec((T, LANES), lambda b, c: (b, base + nc + c)),
                  pl.BlockSpec((T, LANES), lambda b, c: (b, base + 2 * nc + c)),
                  pl.BlockSpec((CONV_W, LANES), lambda b, c: (0, c))],
        out_specs=pl.BlockSpec((T, LANES), lambda b, c: (b, c)),
        out_shape=jax.ShapeDtypeStruct((M, SC_WIDTH), BF16),
        scratch_shapes=[pltpu.VMEM((T + SUBLANES, LANES), F32)],
        compiler_params=_cparams(("parallel", "parallel")),
        name="short_conv",
    )(z, z, z, conv_w.astype(F32))


def _sigmoid(x):
    return 1.0 / (1.0 + jnp.exp(-x))


def _rwprep_body(z_ref, zh_ref, vf_ref, mu_ref, w0_ref, a0_ref, v0_ref, w2_ref, a2_ref, v2_ref, g2_ref,
                 r_o, ld_o, k_o, v_o, a_o, g_o, sh_ref, *, tr, nrb):
    i = pl.program_id(0)
    first = (i % nrb) == 0
    sh_ref[SUBLANES - 1:SUBLANES, :] = jnp.where(first, 0.0, zh_ref[SUBLANES - 1:SUBLANES, :])
    sh_ref[SUBLANES:SUBLANES + tr, :] = z_ref[...]

    def shifted(lo, hi):
        zc = z_ref[:, lo:hi]
        zp = sh_ref[SUBLANES - 1:SUBLANES - 1 + tr, lo:hi]
        return zc + (zp - zc) * mu_ref[:, lo:hi]

    r_o[...] = shifted(OD_R, OD_K)
    k_o[...] = shifted(OD_K, OD_V)
    wd = shifted(OD_WD, OD_AD)
    lw = w0_ref[...] + jnp.dot(jnp.tanh(wd).astype(BF16), w2_ref[...], preferred_element_type=F32)
    nlw = -lw
    softplus = jnp.maximum(nlw, 0.0) + jnp.log(1.0 + jnp.exp(-jnp.abs(nlw)))
    ld_o[...] = -jnp.exp(-softplus - 0.5)
    ad = shifted(OD_AD, OD_VD)
    a_o[...] = _sigmoid(a0_ref[...] + jnp.dot(ad.astype(BF16), a2_ref[...], preferred_element_type=F32))
    vd = shifted(OD_VD, OD_GD)
    v = shifted(OD_V, OD_WD)
    vg = _sigmoid(v0_ref[...] + jnp.dot(vd.astype(BF16), v2_ref[...], preferred_element_type=F32))
    v_o[...] = v + (vf_ref[...] - v) * vg
    gd = shifted(OD_GD, OD_RW_END)
    g_o[...] = jnp.dot(_sigmoid(gd).astype(BF16), g2_ref[...], preferred_element_type=F32)


def _pad_rows(w, rows):
    return jnp.pad(w, ((0, rows - w.shape[0]), (0, 0))).astype(BF16)


def _rwkv_prep(z, z_first, mu_p, w0, a0, v0, w2, a2, v2, g2, T, *, tr=128):
    M = z.shape[0]
    tr = min(tr, T)
    nrb = T // tr
    W = OD_RW_END
    hb = tr // SUBLANES
    row = lambda a: a.astype(F32).reshape(1, RW_WIDTH)
    full = lambda shape: pl.BlockSpec(shape, lambda i: (0, 0))
    out = jax.ShapeDtypeStruct((M, RW_WIDTH), F32)
    ospec = pl.BlockSpec((tr, RW_WIDTH), lambda i: (i, 0))
    return pl.pallas_call(
        functools.partial(_rwprep_body, tr=tr, nrb=nrb),
        grid=(M // tr,),
        in_specs=[pl.BlockSpec((tr, W), lambda i: (i, 0)),
                  pl.BlockSpec((SUBLANES, W), lambda i: (jnp.maximum(i * hb - 1, 0), 0)),
                  pl.BlockSpec((tr, RW_WIDTH), lambda i: (i, 2 * DA_WIDTH // RW_WIDTH)),
                  full((1, W)), full((1, RW_WIDTH)), full((1, RW_WIDTH)), full((1, RW_WIDTH)),
                  full((LANES, RW_WIDTH)), full((LANES, RW_WIDTH)), full((LANES, RW_WIDTH)),
                  full((G_LORA, RW_WIDTH))],
        out_specs=[ospec] * 6,
        out_shape=[out] * 6,
        scratch_shapes=[pltpu.VMEM((tr + SUBLANES, W), F32)],
        compiler_params=_cparams(("parallel",)),
        name="rwkv_prep",
    )(z, z, z_first, mu_p, row(w0), row(a0), row(v0),
      _pad_rows(w2, LANES), _pad_rows(a2, LANES), _pad_rows(v2, LANES), g2.astype(BF16))


def _split3(x):
    hi = x.astype(BF16)
    r1 = x - hi.astype(F32)
    mid = r1.astype(BF16)
    lo = (r1 - mid.astype(F32)).astype(BF16)
    return hi, mid, lo


def _mm(a, b):
    return jnp.dot(a.astype(BF16), b.astype(BF16), preferred_element_type=F32)


def _mm_tn(a, b):
    return jnp.dot(a.T.astype(BF16), b.astype(BF16), preferred_element_type=F32)


def _rwkv_body(r_ref, ld_ref, k_ref, v_ref, a_ref, g_ref, kk_ref, ka_ref, rk_ref, lw_ref, lb_ref,
               o_ref, st_ref, *, ng, L):
    c = pl.program_id(2)

    @pl.when(c == 0)
    def _init():
        st_ref[...] = jnp.zeros_like(st_ref)

    N = RW_HEAD_DIM
    S = RW_GROUP * L
    ri = lax.broadcasted_iota(I32, (S, S), 0)
    ci = lax.broadcasted_iota(I32, (S, S), 1)
    same = (ri // L) == (ci // L)
    incl = same & (ci <= ri)
    strict = same & (ci < ri)
    eye_s = (ci == ri).astype(F32)
    blk16 = (ri // 16) == (ci // 16)
    hmask = (lax.broadcasted_iota(I32, (S, RW_GW), 0) // L) == (lax.broadcasted_iota(I32, (S, RW_GW), 1) // N)
    rl = lax.broadcasted_iota(I32, (L, L), 0)
    cl = lax.broadcasted_iota(I32, (L, L), 1)
    tri = (cl <= rl).astype(BF16)
    rn = lax.broadcasted_iota(I32, (RW_GW, RW_GW), 0)
    cn = lax.broadcasted_iota(I32, (RW_GW, RW_GW), 1)
    eye_g = rn == cn
    ones_g = ((rn // N) == (cn // N)).astype(BF16)

    def gsum(x):
        hi = x.astype(BF16)
        lo = (x - hi.astype(F32)).astype(BF16)
        return jnp.dot(hi, ones_g, preferred_element_type=F32) + jnp.dot(lo, ones_g, preferred_element_type=F32)

    def tile(x):
        return jnp.concatenate([x] * RW_GROUP, axis=0)

    def stack(x):
        return jnp.where(hmask, tile(x), 0.0)

    for gi in range(ng):
        sl = slice(gi * RW_GW, (gi + 1) * RW_GW)
        r = r_ref[:, sl]
        ld = ld_ref[:, sl]
        k = k_ref[:, sl]
        v = v_ref[:, sl]
        a = a_ref[:, sl]
        kk = k * kk_ref[:, sl]
        kk = kk / jnp.maximum(jnp.sqrt(gsum(kk * kk)), 1e-12)
        k2 = k * (1.0 + (a - 1.0) * ka_ref[:, sl])
        av = -kk
        bv = kk * a
        hi, mid, lo = _split3(ld)
        cum = (jnp.dot(tri, hi, preferred_element_type=F32) + jnp.dot(tri, mid, preferred_element_type=F32)
               + jnp.dot(tri, lo, preferred_element_type=F32))
        clast = cum[L - 1:L, :]
        e_neg = jnp.exp(-cum)
        e_l = jnp.exp(clast - cum)
        p_l = jnp.exp(clast)
        at = stack(av * jnp.exp(cum - ld))
        rt = stack(r * jnp.exp(cum))
        vs = stack(v)
        bh = stack(bv * e_l)
        kh = stack(k2 * e_l)
        btb = tile(bv * e_neg).astype(BF16)
        ktb = tile(k2 * e_neg).astype(BF16)
        atb, rtb = at.astype(BF16), rt.astype(BF16)
        mab = jnp.where(strict, _dot_nt(atb, btb), 0.0)
        mak = jnp.where(strict, _dot_nt(atb, ktb), 0.0)
        mrb = jnp.where(incl, _dot_nt(rtb, btb), 0.0)
        mrk = jnp.where(incl, _dot_nt(rtb, ktb), 0.0)
        nd = jnp.where(blk16, mab, 0.0)
        n2 = _mm(nd, nd)
        n4 = _mm(n2, n2)
        n8 = _mm(n4, n4)
        t = eye_s + nd
        t = t + _mm(t, n2)
        t = t + _mm(t, n4)
        t = t + _mm(t, n8)
        size = 16
        while size < L:
            off = ((ri // size) == (ci // size) + 1) & ((ri // (2 * size)) == (ci // (2 * size)))
            t = t + _mm(_mm(t, jnp.where(off, mab, 0.0)), t)
            size *= 2
        wm = _mm(t, at)
        ul = _mm(t, _mm(mak, vs))
        qe = rt + _mm(mrb, wm)
        yl = _mm(mrb, ul) + _mm(mrk, vs)
        gm = _mm_tn(bh, wm) + jnp.where(eye_g, p_l, 0.0)
        hm = _mm_tn(bh, ul) + _mm_tn(kh, vs)
        st = st_ref[gi]
        ys = _mm(qe, st) + yl
        st_ref[gi] = _mm(gm, st) + hm
        y = ys[0:L]
        for hh in range(1, RW_GROUP):
            y = y + ys[hh * L:(hh + 1) * L]
        mean = gsum(y) * (1.0 / N)
        d = y - mean
        var = gsum(d * d) * (1.0 / N)
        yn = d * lax.rsqrt(var + LNX_EPS) * lw_ref[:, sl] + lb_ref[:, sl]
        yn = yn + gsum(r * k2 * rk_ref[:, sl]) * v
        o_ref[:, sl] = (yn * g_ref[:, sl]).astype(o_ref.dtype)


def _rwkv(r, ld, k, v, a, g, k_k, k_a, r_k, lnx_w, lnx_b, B, T, *, ng=2, L=64):
    M = r.shape[0]
    L = min(L, T)
    nc = T // L
    W = ng * RW_GW
    blk = pl.BlockSpec((L, W), lambda b, hg, c: (b * nc + c, hg))
    par = pl.BlockSpec((1, W), lambda b, hg, c: (0, hg))
    row = lambda p: p.astype(F32).reshape(1, RW_WIDTH)
    return pl.pallas_call(
        functools.partial(_rwkv_body, ng=ng, L=L),
        grid=(B, RW_WIDTH // W, nc),
        in_specs=[blk] * 6 + [par] * 5,
        out_specs=blk,
        out_shape=jax.ShapeDtypeStruct((M, RW_WIDTH), BF16),
        scratch_shapes=[pltpu.VMEM((ng, RW_GW, RW_GW), F32)],
        compiler_params=_cparams(("parallel", "parallel", "arbitrary")),
        name="rwkv7_chunk",
    )(r, ld, k, v, a, g, row(k_k), row(k_a), row(r_k), row(lnx_w), row(lnx_b))


def _dsa_body(qd_ref, qi_ref, wi_ref, kd_ref, vd_ref, ki_ref, o_ref,
              keys_ref, cut_ref, m_ref, l_ref, acc_ref, *, tq, tk, ksel, T):
    i = pl.program_id(1)
    nkt = (i * tq + tq + tk - 1) // tk
    row = i * tq + lax.broadcasted_iota(I32, (tq, tk), 0)
    col = lax.broadcasted_iota(I32, (tq, tk), 1)
    lane = lax.broadcasted_iota(I32, (tq, LANES), 1)
    low_half = lane < IDX_DIM
    wi = wi_ref[...]

    def score_tile(j, carry):
        off = pl.multiple_of(j * tk, tk)
        kt = ki_ref[pl.ds(off, tk), :]
        acc = jnp.zeros((tq, tk), F32)
        for h in range(IDX_HEADS):
            qt = qi_ref[:, (h // 2) * LANES:(h // 2 + 1) * LANES]
            qm = jnp.where(low_half if h % 2 == 0 else jnp.logical_not(low_half), qt, jnp.zeros_like(qt))
            acc = acc + jnp.maximum(_dot_nt(qm, kt), 0.0) * wi[:, h:h + 1]
        acc = acc + 0.0
        sc = jnp.where((col + off) <= row, acc, -jnp.inf)
        bits = pltpu.bitcast(sc, I32)
        keys_ref[:, pl.ds(off, tk)] = bits ^ ((bits >> 31) & 0x7FFFFFFF)
        return carry

    lax.fori_loop(0, nkt, score_tile, 0)

    def count(pred):
        def body(j, c):
            off = pl.multiple_of(j * tk, tk)
            hit = jnp.where(pred(keys_ref[:, pl.ds(off, tk)], col + off), 1.0, 0.0)
            for s in range(tk // LANES):
                c = c + hit[:, s * LANES:(s + 1) * LANES]
            return c
        c = lax.fori_loop(0, nkt, body, jnp.zeros((tq, LANES), F32))
        return jnp.sum(c, axis=-1, keepdims=True)

    def bit_step(b, thr):
        cand = thr + jnp.left_shift(jnp.int32(1), 31 - b)
        cnt = count(lambda key, _: key >= cand)
        return jnp.where(cnt >= ksel, cand, thr)

    thr = lax.fori_loop(0, 32, bit_step, jnp.full((tq, 1), INT_MIN, I32))

    n_gt = count(lambda key, _: key > thr)
    n_ge = count(lambda key, _: key >= thr)
    need = (n_ge > ksel) & (thr > NEG_INF_KEY)
    quota = ksel - n_gt
    cut_ref[...] = jnp.full((tq, LANES), T, I32)

    @pl.when(jnp.max(jnp.where(need, 1.0, 0.0)) > 0.0)
    def _ties():
        def pos_step(b, p):
            cand = p + jnp.left_shift(jnp.int32(1), (T.bit_length() - 1) - b)
            cnt = count(lambda key, c: (key == thr) & (c < cand))
            return jnp.where(cnt < quota, cand, p)
        p = lax.fori_loop(0, T.bit_length(), pos_step, jnp.zeros((tq, 1), I32))
        cut_ref[...] = jnp.broadcast_to(jnp.where(need, p, T), (tq, LANES))

    cut = cut_ref[:, 0:1]

    m_ref[...] = jnp.full(m_ref.shape, NEG, F32)
    l_ref[...] = jnp.zeros(l_ref.shape, F32)
    acc_ref[...] = jnp.zeros(acc_ref.shape, F32)

    def attend(j, carry):
        off = pl.multiple_of(j * tk, tk)
        ks = kd_ref[pl.ds(off, tk), :]
        vs = vd_ref[pl.ds(off, tk), :]
        key = keys_ref[:, pl.ds(off, tk)]
        pos = col + off
        sel = ((key > thr) | ((key == thr) & (pos <= cut))) & (pos <= row)
        for h in range(SA_HEADS):
            q = qd_ref[:, h * LANES:(h + 1) * LANES]
            m, l, acc = _online_update(_dot_nt(q, ks), sel, vs, m_ref[h][:, 0:1], l_ref[h][:, 0:1], acc_ref[h])
            m_ref[h] = jnp.broadcast_to(m, (tq, LANES))
            l_ref[h] = jnp.broadcast_to(l, (tq, LANES))
            acc_ref[h] = acc
        return carry

    lax.fori_loop(0, nkt, attend, 0)
    for h in range(SA_HEADS):
        o_ref[:, h * LANES:(h + 1) * LANES] = (acc_ref[h] * (1.0 / l_ref[h][:, 0:1])).astype(o_ref.dtype)


def _dsa(qd, qi, wi, kd, vd, ki, B, T, ksel, *, tq=128, tk=512):
    M = qd.shape[0]
    tq, tk = min(tq, T), min(tk, T)
    nq = T // tq
    qblk = lambda w: pl.BlockSpec((tq, w), lambda b, i: (b * nq + i, 0))
    kblk = pl.BlockSpec((T, LANES), lambda b, i: (b, 0))
    return pl.pallas_call(
        functools.partial(_dsa_body, tq=tq, tk=tk, ksel=ksel, T=T),
        grid=(B, nq),
        in_specs=[qblk(SA_WIDTH), qblk(IDX_HEADS * IDX_DIM), qblk(LANES), kblk, kblk, kblk],
        out_specs=qblk(SA_WIDTH),
        out_shape=jax.ShapeDtypeStruct((M, SA_WIDTH), BF16),
        scratch_shapes=[pltpu.VMEM((tq, T), I32), pltpu.VMEM((tq, LANES), I32),
                        pltpu.VMEM((SA_HEADS, tq, LANES), F32), pltpu.VMEM((SA_HEADS, tq, LANES), F32),
                        pltpu.VMEM((SA_HEADS, tq, LANES), F32)],
        compiler_params=_cparams(("parallel", "arbitrary")),
        name="dsa_attention",
    )(qd, qi, wi, kd, vd, ki)


def _pad_cols(w, width):
    return jnp.pad(w, ((0, 0), (0, width - w.shape[1])))


def _odd_in_weight(w):
    rw_in = 3 * RW_WIDTH + W_LORA + A_LORA + V_LORA + G_LORA
    o = 3 * RW_WIDTH
    segs = [w[:, :o],
            _pad_cols(w[:, o:o + W_LORA], LANES),
            _pad_cols(w[:, o + W_LORA:o + W_LORA + A_LORA], LANES),
            _pad_cols(w[:, o + W_LORA + A_LORA:o + W_LORA + A_LORA + V_LORA], LANES),
            w[:, rw_in - G_LORA:rw_in]]
    d = rw_in
    q = w[:, d:d + SA_WIDTH]
    kd = w[:, d + SA_WIDTH:d + SA_WIDTH + SA_HEAD_DIM]
    vd = w[:, d + SA_WIDTH + SA_HEAD_DIM:d + SA_WIDTH + 2 * SA_HEAD_DIM]
    d2 = d + SA_WIDTH + 2 * SA_HEAD_DIM
    qi = w[:, d2:d2 + IDX_HEADS * IDX_DIM]
    ki = w[:, d2 + IDX_HEADS * IDX_DIM:d2 + IDX_HEADS * IDX_DIM + IDX_DIM]
    wi = w[:, d2 + IDX_HEADS * IDX_DIM + IDX_DIM:]
    segs += [q, kd, vd, qi, ki, ki, _pad_cols(wi, LANES)]
    return _pad_cols(jnp.concatenate(segs, axis=1), OD_PAD).astype(BF16)


def _odd_mu(mu):
    o = 3 * RW_WIDTH
    segs = [mu[:o],
            jnp.pad(mu[o:o + W_LORA], (0, LANES - W_LORA)),
            jnp.pad(mu[o + W_LORA:o + W_LORA + A_LORA], (0, LANES - A_LORA)),
            jnp.pad(mu[o + W_LORA + A_LORA:o + W_LORA + A_LORA + V_LORA], (0, LANES - V_LORA)),
            mu[o + W_LORA + A_LORA + V_LORA:]]
    return jnp.concatenate(segs).astype(F32).reshape(1, OD_RW_END)


def _even_mixer(xf, h, w_in, w_out, q_norm, k_norm, lam_p, subln, conv_w, tabs64, lam_init, B, T):
    z = _matmul(h, w_in, name="even_in")
    qh = _segment(z, 0, DA_WIDTH, q_norm, tabs64, T, gs=64, do_norm=True, do_rope=True,
                  scale=DA_HEAD_DIM ** -0.5 * LOG2E)
    kh = _segment(z, DA_WIDTH, DA_WIDTH, k_norm, tabs64, T, gs=64, do_norm=True, do_rope=True)
    vh = _segment(z, 2 * DA_WIDTH, DA_WIDTH, None, tabs64, T, gs=64, do_norm=False, do_rope=False)
    o = _diff_attention(qh, kh, vh, lam_p, subln, lam_init, B, T)
    y = _short_conv(z, conv_w, B, T)
    cat = jnp.concatenate([o, y], axis=1)
    return _matmul(cat, w_out, resid=xf, name="even_out"), z


def _odd_mixer(xf, h, w_in_p, w_out, mu_p, w0, w2, a0, a2, v0, v2, g2, k_k, k_a, r_k, lnx_w, lnx_b,
               q_norm, k_norm, idxk_norm, z_first, tabs64, tabs128, B, T, ksel):
    z = _matmul(h, w_in_p, tn=512, name="odd_in")
    r, ld, k, v, a, g = _rwkv_prep(z, z_first, mu_p, w0, a0, v0, w2, a2, v2, g2, T)
    rw_out = _rwkv(r, ld, k, v, a, g, k_k, k_a, r_k.reshape(-1), lnx_w, lnx_b, B, T)
    qd = _segment(z, OD_Q, SA_WIDTH, q_norm, tabs128, T, gs=128, do_norm=True, do_rope=True,
                  scale=SA_HEAD_DIM ** -0.5)
    kd = _segment(z, OD_KD, LANES, k_norm, tabs128, T, gs=128, do_norm=True, do_rope=True)
    vd = _segment(z, OD_VDD, LANES, None, tabs128, T, gs=128, do_norm=False, do_rope=False)
    qi = _segment(z, OD_QI, IDX_HEADS * IDX_DIM, None, tabs64, T, gs=64, do_norm=False, do_rope=True)
    ki = _segment(z, OD_KI, LANES, idxk_norm, tabs64, T, gs=64, do_norm=True, do_rope=True)
    wi = _segment(z, OD_WI, LANES, None, tabs64, T, gs=64, do_norm=False, do_rope=False,
                  scale=IDX_HEADS ** -0.5 * IDX_DIM ** -0.5, out_dtype=F32)
    sa_out = _dsa(qd, qi, wi, kd, vd, ki, B, T, ksel)
    cat = jnp.concatenate([rw_out, sa_out], axis=1)
    return _matmul(cat, w_out, resid=xf, name="odd_out")


def kernel(x, mix_norm, ffn_norm, ffn_gate, ffn_up, ffn_down, ev_w_in, ev_w_out, da_q_norm, da_k_norm, da_lambda, da_subln, sc_conv, od_w_in, od_w_out, rw_mu, rw_w0, rw_w2, rw_a0, rw_a2, rw_v0, rw_v2, rw_g2, rw_k_k, rw_k_a, rw_r_k, rw_lnx_w, rw_lnx_b, sa_q_norm, sa_k_norm, idx_k_norm):
    B, T, D = x.shape
    M = B * T
    ksel = min(TOPK_MAX, T // 4)
    xf = x.reshape(M, D)
    tabs64 = _rope_tables(T, 64)
    tabs128 = _rope_tables(T, 128)
    fpad = FFN_PAD - FFN_HIDDEN
    z_first = None
    for i in range(DEPTH):
        h = _rmsnorm(xf, mix_norm[i])
        if i % 2 == 0:
            e = i // 2
            lam_init = 0.8 - 0.6 * math.exp(-0.3 * i)
            xf, z = _even_mixer(xf, h, ev_w_in[e].astype(BF16), ev_w_out[e].astype(BF16), da_q_norm[e],
                                da_k_norm[e], da_lambda[e], da_subln[e], sc_conv[e], tabs64, lam_init, B, T)
            if z_first is None:
                z_first = z
        else:
            o = i // 2
            xf = _odd_mixer(xf, h, _odd_in_weight(od_w_in[o]), od_w_out[o].astype(BF16), _odd_mu(rw_mu[o]),
                            rw_w0[o], rw_w2[o], rw_a0[o], rw_a2[o], rw_v0[o], rw_v2[o], rw_g2[o],
                            rw_k_k[o], rw_k_a[o], rw_r_k[o], rw_lnx_w[o], rw_lnx_b[o],
                            sa_q_norm[o], sa_k_norm[o], idx_k_norm[o], z_first, tabs64, tabs128, B, T, ksel)
        h = _rmsnorm(xf, ffn_norm[i])
        wg = jnp.pad(ffn_gate[i], ((0, 0), (0, fpad))).astype(BF16)
        wu = jnp.pad(ffn_up[i], ((0, 0), (0, fpad))).astype(BF16)
        wd = jnp.pad(ffn_down[i], ((0, fpad), (0, 0))).astype(BF16)
        hid = _matmul(h, wg, b2=wu, out_dtype=BF16, name="ffn_in")
        xf = _matmul(hid, wd, resid=xf, name="ffn_out")
    return xf.reshape(B, T, D)
```

```python
import functools
import math

import jax
import jax.numpy as jnp
from jax import lax
from jax.experimental import pallas as pl
from jax.experimental.pallas import tpu as pltpu

F32 = jnp.float32
BF16 = jnp.bfloat16
I32 = jnp.int32

D_MODEL = 4096
DEPTH = 4
DA_WIDTH = 2048
DA_HEADS = 16
DA_HEAD_DIM = 64
SC_WIDTH = 2048
CONV_W = 3
RW_WIDTH = 2048
RW_HEAD_DIM = 64
RW_HEADS = 32
RW_GROUP = 4
RW_GW = RW_GROUP * RW_HEAD_DIM
W_LORA, A_LORA, V_LORA, G_LORA = 96, 96, 64, 256
LNX_EPS = 64e-5
SA_WIDTH = 2048
SA_HEAD_DIM = 128
SA_HEADS = 16
IDX_HEADS = 16
IDX_DIM = 64
TOPK_MAX = 256
FFN_HIDDEN = 11008
ROPE_THETA = 10000.0
EPS = 1e-6

LANES = 128
SUBLANES = 8
V7X_VMEM_BYTES = 64 * 1024 * 1024
VMEM_LIMIT = (V7X_VMEM_BYTES * 3) // 4

FFN_PAD = 11264
OD_R, OD_K, OD_V = 0, 2048, 4096
OD_WD, OD_AD, OD_VD, OD_GD = 6144, 6272, 6400, 6528
OD_RW_END = 6784
OD_Q, OD_KD, OD_VDD, OD_QI, OD_KI, OD_WI = 6784, 8832, 8960, 9088, 10112, 10240
OD_PAD = 10752
LOG2E = math.log2(math.e)
INT_MIN = -2 ** 31
NEG_INF_KEY = -2139095041


def _cparams(sem):
    return pltpu.CompilerParams(dimension_semantics=sem, vmem_limit_bytes=VMEM_LIMIT)


def _mm_body(*refs, nk, mode):
    if mode == "swiglu":
        a_ref, b_ref, b2_ref, o_ref, acc_ref, acc2_ref = refs
    elif mode == "resid":
        a_ref, b_ref, r_ref, o_ref, acc_ref = refs
    else:
        a_ref, b_ref, o_ref, acc_ref = refs
    k = pl.program_id(2)

    @pl.when(k == 0)
    def _init():
        acc_ref[...] = jnp.zeros_like(acc_ref)
        if mode == "swiglu":
            acc2_ref[...] = jnp.zeros_like(acc2_ref)

    a = a_ref[...]
    acc_ref[...] += jnp.dot(a, b_ref[...], preferred_element_type=F32)
    if mode == "swiglu":
        acc2_ref[...] += jnp.dot(a, b2_ref[...], preferred_element_type=F32)

    @pl.when(k == nk - 1)
    def _fin():
        if mode == "swiglu":
            g = acc_ref[...]
            o_ref[...] = (g * (1.0 / (1.0 + jnp.exp(-g))) * acc2_ref[...]).astype(o_ref.dtype)
        elif mode == "resid":
            o_ref[...] = r_ref[...] + acc_ref[...]
        else:
            o_ref[...] = acc_ref[...].astype(o_ref.dtype)


def _epilogue(mode, acc, acc2, resid, dtype):
    if mode == "swiglu":
        return (acc * (1.0 / (1.0 + jnp.exp(-acc))) * acc2).astype(dtype)
    if mode == "resid":
        return resid + acc
    return acc.astype(dtype)


def _mm_fullk_body(*refs, mode):
    if mode == "swiglu":
        a_ref, b_ref, b2_ref, o_ref = refs
    elif mode == "resid":
        a_ref, b_ref, r_ref, o_ref = refs
    else:
        a_ref, b_ref, o_ref = refs
    a = a_ref[...]
    acc = jnp.dot(a, b_ref[...], preferred_element_type=F32)
    acc2 = jnp.dot(a, b2_ref[...], preferred_element_type=F32) if mode == "swiglu" else None
    o_ref[...] = _epilogue(mode, acc, acc2, r_ref[...] if mode == "resid" else None, o_ref.dtype)


def _matmul_fullk(a, b, *, b2=None, resid=None, out_dtype=F32, tm=1024, tn=512, name="mm"):
    M, K = a.shape
    N = b.shape[1]
    tm, tn = min(tm, M), min(tn, N)
    assert M % tm == 0 and N % tn == 0, (a.shape, b.shape, tm, tn)
    mode = "swiglu" if b2 is not None else ("resid" if resid is not None else "plain")
    in_specs = [pl.BlockSpec((tm, K), lambda i, j: (i, 0)), pl.BlockSpec((K, tn), lambda i, j: (0, j))]
    args = [a, b]
    if b2 is not None:
        in_specs.append(pl.BlockSpec((K, tn), lambda i, j: (0, j)))
        args.append(b2)
    if resid is not None:
        in_specs.append(pl.BlockSpec((tm, tn), lambda i, j: (i, j)))
        args.append(resid)
    return pl.pallas_call(
        functools.partial(_mm_fullk_body, mode=mode),
        grid=(M // tm, N // tn),
        in_specs=in_specs,
        out_specs=pl.BlockSpec((tm, tn), lambda i, j: (i, j)),
        out_shape=jax.ShapeDtypeStruct((M, N), out_dtype),
        compiler_params=_cparams(("parallel", "arbitrary")),
        name=name,
    )(*args)


def _cast_body(x_ref, o_ref, *, rows, cols, tr, tc):
    x = x_ref[...]
    r = pl.program_id(0) * tr + lax.broadcasted_iota(I32, x.shape, 0)
    c = pl.program_id(1) * tc + lax.broadcasted_iota(I32, x.shape, 1)
    o_ref[...] = jnp.where((r < rows) & (c < cols), x, 0.0).astype(o_ref.dtype)


def _cast_pad(w, rows_p=None, cols_p=None, *, tr=512, tc=1024):
    rows, cols = w.shape
    rows_p, cols_p = rows_p or rows, cols_p or cols
    tr, tc = min(tr, rows_p), min(tc, cols_p)
    assert rows_p % tr == 0 and cols_p % tc == 0, (w.shape, rows_p, cols_p)
    return pl.pallas_call(
        functools.partial(_cast_body, rows=rows, cols=cols, tr=tr, tc=tc),
        grid=(rows_p // tr, cols_p // tc),
        in_specs=[pl.BlockSpec((tr, tc), lambda i, j: (i, j))],
        out_specs=pl.BlockSpec((tr, tc), lambda i, j: (i, j)),
        out_shape=jax.ShapeDtypeStruct((rows_p, cols_p), BF16),
        compiler_params=_cparams(("parallel", "parallel")),
        name="cast_pad",
    )(w)


def _matmul(a, b, *, b2=None, resid=None, out_dtype=F32, tm=1024, tn=1024, tk=1024, name="mm"):
    M, K = a.shape
    N = b.shape[1]
    tm, tn, tk = min(tm, M), min(tn, N), min(tk, K)
    assert M % tm == 0 and N % tn == 0 and K % tk == 0, (a.shape, b.shape, tm, tn, tk)
    nk = K // tk
    mode = "swiglu" if b2 is not None else ("resid" if resid is not None else "plain")
    in_specs = [pl.BlockSpec((tm, tk), lambda i, j, k: (i, k)),
                pl.BlockSpec((tk, tn), lambda i, j, k: (k, j))]
    args = [a, b]
    scratch = [pltpu.VMEM((tm, tn), F32)]
    if b2 is not None:
        in_specs.append(pl.BlockSpec((tk, tn), lambda i, j, k: (k, j)))
        args.append(b2)
        scratch.append(pltpu.VMEM((tm, tn), F32))
    if resid is not None:
        in_specs.append(pl.BlockSpec((tm, tn), lambda i, j, k: (i, j)))
        args.append(resid)
    return pl.pallas_call(
        functools.partial(_mm_body, nk=nk, mode=mode),
        grid=(M // tm, N // tn, nk),
        in_specs=in_specs,
        out_specs=pl.BlockSpec((tm, tn), lambda i, j, k: (i, j)),
        out_shape=jax.ShapeDtypeStruct((M, N), out_dtype),
        scratch_shapes=scratch,
        compiler_params=_cparams(("parallel", "parallel", "arbitrary")),
        name=name,
    )(*args)


def _rms_body(x_ref, g_ref, o_ref):
    x = x_ref[...]
    ms = jnp.mean(x * x, axis=-1, keepdims=True)
    o_ref[...] = (x * lax.rsqrt(ms + EPS) * g_ref[...]).astype(o_ref.dtype)


def _rmsnorm(x, g, *, tr=256):
    M, D = x.shape
    tr = min(tr, M)
    return pl.pallas_call(
        _rms_body,
        grid=(M // tr,),
        in_specs=[pl.BlockSpec((tr, D), lambda i: (i, 0)),
                  pl.BlockSpec((1, D), lambda i: (0, 0))],
        out_specs=pl.BlockSpec((tr, D), lambda i: (i, 0)),
        out_shape=jax.ShapeDtypeStruct((M, D), BF16),
        compiler_params=_cparams(("parallel",)),
        name="rmsnorm",
    )(x, g.reshape(1, D))


def _rope_tables(T, gs):
    half = gs // 2
    inv = ROPE_THETA ** (-jnp.arange(half, dtype=F32) / half)
    ang = jnp.arange(T, dtype=jnp.int32).astype(F32)[:, None] * inv[None, :]
    cos, sin = jnp.cos(ang), jnp.sin(ang)
    cosg = jnp.concatenate([cos, cos], axis=1)
    sing = jnp.concatenate([-sin, sin], axis=1)
    reps = LANES // gs
    return jnp.tile(cosg, (1, reps)), jnp.tile(sing, (1, reps))


def _group_ones(gs):
    r = jnp.arange(LANES)
    return (r[:, None] // gs == r[None, :] // gs).astype(BF16)


def _seg_body(x_ref, g_ref, cos_ref, sin_ref, bd_ref, o_ref, *, gs, do_norm, do_rope, scale):
    x = x_ref[...]
    if do_norm:
        x2 = x * x
        hi = x2.astype(BF16)
        lo = (x2 - hi.astype(F32)).astype(BF16)
        bd = bd_ref[...]
        ssum = jnp.dot(hi, bd, preferred_element_type=F32) + jnp.dot(lo, bd, preferred_element_type=F32)
        x = x * lax.rsqrt(ssum * (1.0 / gs) + EPS) * g_ref[...]
    if do_rope:
        if gs == 64:
            lane = lax.broadcasted_iota(I32, x.shape, 1)
            rot = jnp.where((lane & 63) < 32, pltpu.roll(x, 96, 1), pltpu.roll(x, 32, 1))
        else:
            rot = pltpu.roll(x, 64, 1)
        x = x * cos_ref[...] + rot * sin_ref[...]
    if scale != 1.0:
        x = x * scale
    o_ref[...] = x.astype(o_ref.dtype)


def _segment(z, col_off, width, gain, tables, T, *, gs, do_norm, do_rope, scale=1.0, out_dtype=BF16, tr=512):
    M = z.shape[0]
    tr = min(tr, T)
    nrb = T // tr
    cb = col_off // LANES
    cos, sin = tables
    if gain is None:
        gain = jnp.ones((gs,), F32)
    gt = jnp.tile(gain.astype(F32), LANES // gs).reshape(1, LANES)
    return pl.pallas_call(
        functools.partial(_seg_body, gs=gs, do_norm=do_norm, do_rope=do_rope, scale=scale),
        grid=(M // tr, width // LANES),
        in_specs=[pl.BlockSpec((tr, LANES), lambda i, j: (i, cb + j)),
                  pl.BlockSpec((1, LANES), lambda i, j: (0, 0)),
                  pl.BlockSpec((tr, LANES), lambda i, j: (i % nrb, 0)),
                  pl.BlockSpec((tr, LANES), lambda i, j: (i % nrb, 0)),
                  pl.BlockSpec((LANES, LANES), lambda i, j: (0, 0))],
        out_specs=pl.BlockSpec((tr, LANES), lambda i, j: (i, j)),
        out_shape=jax.ShapeDtypeStruct((M, width), out_dtype),
        compiler_params=_cparams(("parallel", "parallel")),
        name="segment",
    )(z, gt, cos, sin, _group_ones(gs))


def _dot_nt(a, b):
    return lax.dot_general(a, b, (((1,), (1,)), ((), ())), preferred_element_type=F32)


def _softmax_step(s, vt, m_ref, l_ref, acc_ref, idx, guard):
    m_prev = m_ref[idx]
    m_next = jnp.maximum(m_prev, jnp.max(s, axis=0, keepdims=True))
    m_use = jnp.where(m_next == -jnp.inf, 0.0, m_next) if guard else m_next
    p = jnp.exp2(s - m_use)
    alpha = jnp.exp2(m_prev - m_use)
    l_ref[idx] = alpha * l_ref[idx] + jnp.sum(p, axis=0, keepdims=True)
    acc_ref[idx] = alpha * acc_ref[idx] + jnp.dot(vt, p.astype(BF16), preferred_element_type=F32)
    m_ref[idx] = m_next


def _dattn_body(lam_ref, sub_ref, q_ref, k_ref, v_ref, o_ref, m_ref, l_ref, acc_ref, *, tq, tk, lam_init):
    i = pl.program_id(2)
    q = q_ref[...]
    lane = lax.broadcasted_iota(I32, q.shape, 1)
    zero = jnp.zeros_like(q)
    qs = (jnp.where(lane < DA_HEAD_DIM, q, zero), jnp.where(lane >= DA_HEAD_DIM, q, zero))
    m_ref[...] = jnp.full(m_ref.shape, -jnp.inf, F32)
    l_ref[...] = jnp.zeros(l_ref.shape, F32)
    acc_ref[...] = jnp.zeros(acc_ref.shape, F32)
    reps = tk // LANES

    def tile_step(j, masked):
        off = pl.multiple_of(j * tk, tk)
        ks = k_ref[pl.ds(off, tk), :]
        vs = v_ref[pl.ds(off, tk), :]
        if masked:
            row = i * tq + lax.broadcasted_iota(I32, (tq, tk), 0)
            col = off + lax.broadcasted_iota(I32, (tq, tk), 1)
            vis = col <= row
        for c in range(2):
            s = _dot_nt(qs[c], ks)
            if masked:
                s = jnp.where(vis, s, -jnp.inf)
            m_prev = m_ref[c]
            m_next = jnp.maximum(m_prev, jnp.max(s, axis=-1, keepdims=True))
            p = jnp.exp2(s - jnp.concatenate([m_next] * reps, axis=1))
            alpha = jnp.exp2(m_prev - m_next)
            l_ref[c] = alpha * l_ref[c] + jnp.sum(p, axis=-1, keepdims=True)
            acc_ref[c] = alpha * acc_ref[c] + jnp.dot(p.astype(BF16), vs, preferred_element_type=F32)
            m_ref[c] = m_next

    n_full = (i * tq + 1) // tk
    n_all = (i * tq + tq + tk - 1) // tk

    def full_body(j, carry):
        tile_step(j, False)
        return carry

    def diag_body(j, carry):
        tile_step(j, True)
        return carry

    lax.fori_loop(0, n_full, full_body, 0)
    lax.fori_loop(n_full, n_all, diag_body, 0)

    lp = lam_ref[...]
    lam = (jnp.exp(jnp.sum(lp[0:1] * lp[1:2], axis=-1, keepdims=True))
           - jnp.exp(jnp.sum(lp[2:3] * lp[3:4], axis=-1, keepdims=True)) + lam_init)
    o = acc_ref[0] * (1.0 / l_ref[0]) - lam * (acc_ref[1] * (1.0 / l_ref[1]))
    ms = jnp.mean(o * o, axis=-1, keepdims=True)
    o = o * lax.rsqrt(ms + EPS) * sub_ref[...] * (1.0 - lam_init)
    o_ref[...] = o.astype(o_ref.dtype)


def _diff_attention(qh, kh, vh, lam_p, subln, lam_init, B, T, *, tq=512, tk=512):
    M = qh.shape[0]
    tq, tk = min(tq, T), min(tk, T)
    nq = T // tq
    return pl.pallas_call(
        functools.partial(_dattn_body, tq=tq, tk=tk, lam_init=lam_init),
        grid=(B, DA_HEADS, nq),
        in_specs=[pl.BlockSpec((4, DA_HEAD_DIM), lambda b, h, i: (0, 0)),
                  pl.BlockSpec((1, LANES), lambda b, h, i: (0, 0)),
                  pl.BlockSpec((tq, LANES), lambda b, h, i: (b * nq + i, h)),
                  pl.BlockSpec((T, LANES), lambda b, h, i: (b, h)),
                  pl.BlockSpec((T, LANES), lambda b, h, i: (b, h))],
        out_specs=pl.BlockSpec((tq, LANES), lambda b, h, i: (b * nq + i, h)),
        out_shape=jax.ShapeDtypeStruct((M, DA_WIDTH), BF16),
        scratch_shapes=[pltpu.VMEM((2, tq, LANES), F32)] * 3,
        compiler_params=_cparams(("parallel", "parallel", "arbitrary")),
        name="diff_attention",
    )(lam_p.astype(F32), subln.astype(F32).reshape(1, LANES), qh, kh, vh)


def _conv_body(gb_ref, gc_ref, u_ref, w_ref, o_ref, sh_ref, *, T):
    cu = gc_ref[...] * u_ref[...]
    sh_ref[0:SUBLANES, :] = jnp.zeros((SUBLANES, LANES), F32)
    sh_ref[SUBLANES:SUBLANES + T, :] = cu
    w = w_ref[...]
    conv = (sh_ref[SUBLANES - 2:SUBLANES - 2 + T, :] * w[0:1]
            + sh_ref[SUBLANES - 1:SUBLANES - 1 + T, :] * w[1:2]
            + cu * w[2:3])
    o_ref[...] = (gb_ref[...] * conv).astype(o_ref.dtype)


def _short_conv(z, conv_w, B, T):
    M = z.shape[0]
    nc = SC_WIDTH // LANES
    base = 3 * DA_WIDTH // LANES
    return pl.pallas_call(
        functools.partial(_conv_body, T=T),
        grid=(B, nc),
        in_specs=[pl.BlockSpec((T, LANES), lambda b, c: (b, base + c)),
                  pl.BlockSpec((T, LANES), lambda b, c: (b, base + nc + c)),
                  pl.BlockSpec((T, LANES), lambda b, c: (b, base + 2 * nc + c)),
                  pl.BlockSpec((CONV_W, LANES), lambda b, c: (0, c))],
        out_specs=pl.BlockSpec((T, LANES), lambda b, c: (b, c)),
        out_shape=jax.ShapeDtypeStruct((M, SC_WIDTH), BF16),
        scratch_shapes=[pltpu.VMEM((T + SUBLANES, LANES), F32)],
        compiler_params=_cparams(("parallel", "parallel")),
        name="short_conv",
    )(z, z, z, conv_w.astype(F32))


def _sigmoid(x):
    return 1.0 / (1.0 + jnp.exp(-x))


def _rwprep_body(z_ref, zh_ref, vf_ref, mu_ref, w0_ref, a0_ref, v0_ref, w2_ref, a2_ref, v2_ref, g2_ref,
                 r_o, ld_o, k_o, v_o, a_o, g_o, sh_ref, *, tr, nrb):
    i = pl.program_id(0)
    first = (i % nrb) == 0
    sh_ref[SUBLANES - 1:SUBLANES, :] = jnp.where(first, 0.0, zh_ref[SUBLANES - 1:SUBLANES, :])
    sh_ref[SUBLANES:SUBLANES + tr, :] = z_ref[...]

    def shifted(lo, hi):
        zc = z_ref[:, lo:hi]
        zp = sh_ref[SUBLANES - 1:SUBLANES - 1 + tr, lo:hi]
        return zc + (zp - zc) * mu_ref[:, lo:hi]

    r_o[...] = shifted(OD_R, OD_K)
    k_o[...] = shifted(OD_K, OD_V)
    wd = shifted(OD_WD, OD_AD)
    lw = w0_ref[...] + jnp.dot(jnp.tanh(wd).astype(BF16), w2_ref[...], preferred_element_type=F32)
    nlw = -lw
    softplus = jnp.maximum(nlw, 0.0) + jnp.log(1.0 + jnp.exp(-jnp.abs(nlw)))
    ld_o[...] = -jnp.exp(-softplus - 0.5)
    ad = shifted(OD_AD, OD_VD)
    a_o[...] = _sigmoid(a0_ref[...] + jnp.dot(ad.astype(BF16), a2_ref[...], preferred_element_type=F32))
    vd = shifted(OD_VD, OD_GD)
    v = shifted(OD_V, OD_WD)
    vg = _sigmoid(v0_ref[...] + jnp.dot(vd.astype(BF16), v2_ref[...], preferred_element_type=F32))
    v_o[...] = v + (vf_ref[...] - v) * vg
    gd = shifted(OD_GD, OD_RW_END)
    g_o[...] = jnp.dot(_sigmoid(gd).astype(BF16), g2_ref[...], preferred_element_type=F32)


def _pad_rows(w, rows):
    return jnp.pad(w, ((0, rows - w.shape[0]), (0, 0))).astype(BF16)


def _rwkv_prep(z, z_first, mu_p, w0, a0, v0, w2, a2, v2, g2, T, *, tr=128):
    M = z.shape[0]
    tr = min(tr, T)
    nrb = T // tr
    W = OD_RW_END
    hb = tr // SUBLANES
    row = lambda a: a.astype(F32).reshape(1, RW_WIDTH)
    full = lambda shape: pl.BlockSpec(shape, lambda i: (0, 0))
    out = jax.ShapeDtypeStruct((M, RW_WIDTH), F32)
    ospec = pl.BlockSpec((tr, RW_WIDTH), lambda i: (i, 0))
    return pl.pallas_call(
        functools.partial(_rwprep_body, tr=tr, nrb=nrb),
        grid=(M // tr,),
        in_specs=[pl.BlockSpec((tr, W), lambda i: (i, 0)),
                  pl.BlockSpec((SUBLANES, W), lambda i: (jnp.maximum(i * hb - 1, 0), 0)),
                  pl.BlockSpec((tr, RW_WIDTH), lambda i: (i, 2 * DA_WIDTH // RW_WIDTH)),
                  full((1, W)), full((1, RW_WIDTH)), full((1, RW_WIDTH)), full((1, RW_WIDTH)),
                  full((LANES, RW_WIDTH)), full((LANES, RW_WIDTH)), full((LANES, RW_WIDTH)),
                  full((G_LORA, RW_WIDTH))],
        out_specs=[ospec] * 6,
        out_shape=[out] * 6,
        scratch_shapes=[pltpu.VMEM((tr + SUBLANES, W), F32)],
        compiler_params=_cparams(("parallel",)),
        name="rwkv_prep",
    )(z, z, z_first, mu_p, row(w0), row(a0), row(v0),
      _pad_rows(w2, LANES), _pad_rows(a2, LANES), _pad_rows(v2, LANES), g2.astype(BF16))


def _split3(x):
    hi = x.astype(BF16)
    r1 = x - hi.astype(F32)
    mid = r1.astype(BF16)
    lo = (r1 - mid.astype(F32)).astype(BF16)
    return hi, mid, lo


def _mm(a, b):
    return jnp.dot(a.astype(BF16), b.astype(BF16), preferred_element_type=F32)


def _mm_tn(a, b):
    return jnp.dot(a.T.astype(BF16), b.astype(BF16), preferred_element_type=F32)


def _rwkv_body(r_ref, ld_ref, k_ref, v_ref, a_ref, g_ref, kk_ref, ka_ref, rk_ref, lw_ref, lb_ref,
               o_ref, st_ref, *, ng, L):
    c = pl.program_id(2)

    @pl.when(c == 0)
    def _init():
        st_ref[...] = jnp.zeros_like(st_ref)

    N = RW_HEAD_DIM
    S = RW_GROUP * L
    ri = lax.broadcasted_iota(I32, (S, S), 0)
    ci = lax.broadcasted_iota(I32, (S, S), 1)
    same = (ri // L) == (ci // L)
    incl = same & (ci <= ri)
    strict = same & (ci < ri)
    eye_s = (ci == ri).astype(F32)
    blk16 = (ri // 16) == (ci // 16)
    hmask = (lax.broadcasted_iota(I32, (S, RW_GW), 0) // L) == (lax.broadcasted_iota(I32, (S, RW_GW), 1) // N)
    rl = lax.broadcasted_iota(I32, (L, L), 0)
    cl = lax.broadcasted_iota(I32, (L, L), 1)
    tri = (cl <= rl).astype(BF16)
    rn = lax.broadcasted_iota(I32, (RW_GW, RW_GW), 0)
    cn = lax.broadcasted_iota(I32, (RW_GW, RW_GW), 1)
    eye_g = rn == cn
    ones_g = ((rn // N) == (cn // N)).astype(BF16)

    def gsum(x):
        hi = x.astype(BF16)
        lo = (x - hi.astype(F32)).astype(BF16)
        return jnp.dot(hi, ones_g, preferred_element_type=F32) + jnp.dot(lo, ones_g, preferred_element_type=F32)

    def tile(x):
        return jnp.concatenate([x] * RW_GROUP, axis=0)

    def stack(x):
        return jnp.where(hmask, tile(x), 0.0)

    for gi in range(ng):
        sl = slice(gi * RW_GW, (gi + 1) * RW_GW)
        r = r_ref[:, sl]
        ld = ld_ref[:, sl]
        k = k_ref[:, sl]
        v = v_ref[:, sl]
        a = a_ref[:, sl]
        kk = k * kk_ref[:, sl]
        kk = kk / jnp.maximum(jnp.sqrt(gsum(kk * kk)), 1e-12)
        k2 = k * (1.0 + (a - 1.0) * ka_ref[:, sl])
        av = -kk
        bv = kk * a
        hi, mid, lo = _split3(ld)
        cum = (jnp.dot(tri, hi, preferred_element_type=F32) + jnp.dot(tri, mid, preferred_element_type=F32)
               + jnp.dot(tri, lo, preferred_element_type=F32))
        clast = cum[L - 1:L, :]
        e_neg = jnp.exp(-cum)
        e_l = jnp.exp(clast - cum)
        p_l = jnp.exp(clast)
        at = stack(av * jnp.exp(cum - ld))
        rt = stack(r * jnp.exp(cum))
        vs = stack(v)
        bh = stack(bv * e_l)
        kh = stack(k2 * e_l)
        btb = tile(bv * e_neg).astype(BF16)
        ktb = tile(k2 * e_neg).astype(BF16)
        atb, rtb = at.astype(BF16), rt.astype(BF16)
        mab = jnp.where(strict, _dot_nt(atb, btb), 0.0)
        mak = jnp.where(strict, _dot_nt(atb, ktb), 0.0)
        mrb = jnp.where(incl, _dot_nt(rtb, btb), 0.0)
        mrk = jnp.where(incl, _dot_nt(rtb, ktb), 0.0)
        nd = jnp.where(blk16, mab, 0.0)
        n2 = _mm(nd, nd)
        n4 = _mm(n2, n2)
        n8 = _mm(n4, n4)
        t = eye_s + nd
        t = t + _mm(t, n2)
        t = t + _mm(t, n4)
        t = t + _mm(t, n8)
        size = 16
        while size < L:
            off = ((ri // size) == (ci // size) + 1) & ((ri // (2 * size)) == (ci // (2 * size)))
            t = t + _mm(_mm(t, jnp.where(off, mab, 0.0)), t)
            size *= 2
        wm = _mm(t, at)
        ul = _mm(t, _mm(mak, vs))
        qe = rt + _mm(mrb, wm)
        yl = _mm(mrb, ul) + _mm(mrk, vs)
        gm = _mm_tn(bh, wm) + jnp.where(eye_g, p_l, 0.0)
        hm = _mm_tn(bh, ul) + _mm_tn(kh, vs)
        st = st_ref[gi]
        ys = _mm(qe, st) + yl
        st_ref[gi] = _mm(gm, st) + hm
        y = ys[0:L]
        for hh in range(1, RW_GROUP):
            y = y + ys[hh * L:(hh + 1) * L]
        mean = gsum(y) * (1.0 / N)
        d = y - mean
        var = gsum(d * d) * (1.0 / N)
        yn = d * lax.rsqrt(var + LNX_EPS) * lw_ref[:, sl] + lb_ref[:, sl]
        yn = yn + gsum(r * k2 * rk_ref[:, sl]) * v
        o_ref[:, sl] = (yn * g_ref[:, sl]).astype(o_ref.dtype)


def _rwkv(r, ld, k, v, a, g, k_k, k_a, r_k, lnx_w, lnx_b, B, T, *, ng=2, L=64):
    M = r.shape[0]
    L = min(L, T)
    nc = T // L
    W = ng * RW_GW
    blk = pl.BlockSpec((L, W), lambda b, hg, c: (b * nc + c, hg))
    par = pl.BlockSpec((1, W), lambda b, hg, c: (0, hg))
    row = lambda p: p.astype(F32).reshape(1, RW_WIDTH)
    return pl.pallas_call(
        functools.partial(_rwkv_body, ng=ng, L=L),
        grid=(B, RW_WIDTH // W, nc),
        in_specs=[blk] * 6 + [par] * 5,
        out_specs=blk,
        out_shape=jax.ShapeDtypeStruct((M, RW_WIDTH), BF16),
        scratch_shapes=[pltpu.VMEM((ng, RW_GW, RW_GW), F32)],
        compiler_params=_cparams(("parallel", "parallel", "arbitrary")),
        name="rwkv7_chunk",
    )(r, ld, k, v, a, g, row(k_k), row(k_a), row(r_k), row(lnx_w), row(lnx_b))


def _dsa_body(qd_ref, qi_ref, wi_ref, kd_ref, vd_ref, ki_ref, o_ref,
              keys_ref, bias_ref, qim_ref, cut_ref, m_ref, l_ref, acc_ref, *, tq, tk, ksel, T):
    i = pl.program_id(1)
    nkt = (i * tq + tq + tk - 1) // tk
    krow = lax.broadcasted_iota(I32, (tk, tq), 0)
    qpos = i * tq + lax.broadcasted_iota(I32, (tk, tq), 1)
    low_half = lax.broadcasted_iota(I32, (tq, LANES), 1) < IDX_DIM
    for h in range(IDX_HEADS):
        qt = qi_ref[:, (h // 2) * LANES:(h // 2 + 1) * LANES]
        qim_ref[h] = jnp.where(low_half if h % 2 == 0 else jnp.logical_not(low_half), qt,
                               jnp.zeros_like(qt)).astype(qim_ref.dtype)
    wit = wi_ref[...].T

    def score_tile(j, carry):
        off = pl.multiple_of(j * tk, tk)
        kt = ki_ref[pl.ds(off, tk), :]
        acc = jnp.zeros((tk, tq), F32)
        for h in range(IDX_HEADS):
            acc = acc + jnp.maximum(_dot_nt(kt, qim_ref[h]), 0.0) * wit[h:h + 1, :]
        acc = acc + 0.0
        sc = jnp.where((krow + off) <= qpos, acc, -jnp.inf)
        bits = pltpu.bitcast(sc, I32)
        keys_ref[pl.ds(off, tk), :] = bits ^ ((bits >> 31) & 0x7FFFFFFF)
        return carry

    lax.fori_loop(0, nkt, score_tile, 0)

    def count(pred):
        def body(j, c):
            off = pl.multiple_of(j * tk, tk)
            hit = jnp.where(pred(keys_ref[pl.ds(off, tk), :], krow + off), 1.0, 0.0)
            return c + jnp.sum(hit, axis=0, keepdims=True)
        return lax.fori_loop(0, nkt, body, jnp.zeros((1, tq), F32))

    def bit_step(b, thr):
        cand = thr + jnp.left_shift(jnp.int32(1), 31 - b)
        cnt = count(lambda key, _: key >= cand)
        return jnp.where(cnt >= ksel, cand, thr)

    thr = lax.fori_loop(0, 32, bit_step, jnp.full((1, tq), INT_MIN, I32))

    n_gt = count(lambda key, _: key > thr)
    n_ge = count(lambda key, _: key >= thr)
    need = (n_ge > ksel) & (thr > NEG_INF_KEY)
    quota = ksel - n_gt
    cut_ref[...] = jnp.full((1, tq), T, I32)

    @pl.when(jnp.max(jnp.where(need, 1.0, 0.0)) > 0.0)
    def _ties():
        def pos_step(b, p):
            cand = p + jnp.left_shift(jnp.int32(1), (T.bit_length() - 1) - b)
            cnt = count(lambda key, pos: (key == thr) & (pos < cand))
            return jnp.where(cnt < quota, cand, p)
        p = lax.fori_loop(0, T.bit_length(), pos_step, jnp.zeros((1, tq), I32))
        cut_ref[...] = jnp.where(need, p, T)

    cut = cut_ref[...]
    thr_sel = jnp.maximum(thr, NEG_INF_KEY + 1)

    def bias_tile(j, carry):
        off = pl.multiple_of(j * tk, tk)
        key = keys_ref[pl.ds(off, tk), :]
        sel = (key > thr_sel) | ((key == thr_sel) & ((krow + off) <= cut))
        bias_ref[pl.ds(off, tk), :] = jnp.where(sel, 0.0, -jnp.inf)
        return carry

    lax.fori_loop(0, nkt, bias_tile, 0)

    m_ref[...] = jnp.full(m_ref.shape, -jnp.inf, F32)
    l_ref[...] = jnp.zeros(l_ref.shape, F32)
    acc_ref[...] = jnp.zeros(acc_ref.shape, F32)

    def attend(j, carry):
        off = pl.multiple_of(j * tk, tk)
        ks = kd_ref[pl.ds(off, tk), :]
        vt = vd_ref[pl.ds(off, tk), :].T
        bias = bias_ref[pl.ds(off, tk), :]
        for h in range(SA_HEADS):
            s = _dot_nt(ks, qd_ref[:, h * LANES:(h + 1) * LANES]) + bias
            _softmax_step(s, vt, m_ref, l_ref, acc_ref, h, guard=True)
        return carry

    lax.fori_loop(0, nkt, attend, 0)
    for h in range(SA_HEADS):
        o_ref[:, h * LANES:(h + 1) * LANES] = (acc_ref[h] * (1.0 / l_ref[h])).T.astype(o_ref.dtype)


def _dsa(qd, qi, wi, kd, vd, ki, B, T, ksel, *, tq=256, tk=512):
    M = qd.shape[0]
    tq, tk = min(tq, T), min(tk, T)
    nq = T // tq
    qblk = lambda w: pl.BlockSpec((tq, w), lambda b, i: (b * nq + i, 0))
    kblk = pl.BlockSpec((T, LANES), lambda b, i: (b, 0))
    return pl.pallas_call(
        functools.partial(_dsa_body, tq=tq, tk=tk, ksel=ksel, T=T),
        grid=(B, nq),
        in_specs=[qblk(SA_WIDTH), qblk(IDX_HEADS * IDX_DIM), qblk(LANES), kblk, kblk, kblk],
        out_specs=qblk(SA_WIDTH),
        out_shape=jax.ShapeDtypeStruct((M, SA_WIDTH), BF16),
        scratch_shapes=[pltpu.VMEM((T, tq), I32), pltpu.VMEM((T, tq), F32),
                        pltpu.VMEM((IDX_HEADS, tq, LANES), BF16), pltpu.VMEM((1, tq), I32),
                        pltpu.VMEM((SA_HEADS, 1, tq), F32), pltpu.VMEM((SA_HEADS, 1, tq), F32),
                        pltpu.VMEM((SA_HEADS, LANES, tq), F32)],
        compiler_params=_cparams(("parallel", "arbitrary")),
        name="dsa_attention",
    )(qd, qi, wi, kd, vd, ki)


def _pad_cols(w, width):
    return jnp.pad(w, ((0, 0), (0, width - w.shape[1])))


def _odd_in_weight(w):
    rw_in = 3 * RW_WIDTH + W_LORA + A_LORA + V_LORA + G_LORA
    o = 3 * RW_WIDTH
    segs = [w[:, :o],
            _pad_cols(w[:, o:o + W_LORA], LANES),
            _pad_cols(w[:, o + W_LORA:o + W_LORA + A_LORA], LANES),
            _pad_cols(w[:, o + W_LORA + A_LORA:o + W_LORA + A_LORA + V_LORA], LANES),
            w[:, rw_in - G_LORA:rw_in]]
    d = rw_in
    q = w[:, d:d + SA_WIDTH]
    kd = w[:, d + SA_WIDTH:d + SA_WIDTH + SA_HEAD_DIM]
    vd = w[:, d + SA_WIDTH + SA_HEAD_DIM:d + SA_WIDTH + 2 * SA_HEAD_DIM]
    d2 = d + SA_WIDTH + 2 * SA_HEAD_DIM
    qi = w[:, d2:d2 + IDX_HEADS * IDX_DIM]
    ki = w[:, d2 + IDX_HEADS * IDX_DIM:d2 + IDX_HEADS * IDX_DIM + IDX_DIM]
    wi = w[:, d2 + IDX_HEADS * IDX_DIM + IDX_DIM:]
    segs += [q, kd, vd, qi, ki, ki, _pad_cols(wi, LANES)]
    return _pad_cols(jnp.concatenate(segs, axis=1), OD_PAD).astype(BF16)


def _odd_mu(mu):
    o = 3 * RW_WIDTH
    segs = [mu[:o],
            jnp.pad(mu[o:o + W_LORA], (0, LANES - W_LORA)),
            jnp.pad(mu[o + W_LORA:o + W_LORA + A_LORA], (0, LANES - A_LORA)),
            jnp.pad(mu[o + W_LORA + A_LORA:o + W_LORA + A_LORA + V_LORA], (0, LANES - V_LORA)),
            mu[o + W_LORA + A_LORA + V_LORA:]]
    return jnp.concatenate(segs).astype(F32).reshape(1, OD_RW_END)


def _even_mixer(xf, h, w_in, w_out, q_norm, k_norm, lam_p, subln, conv_w, tabs64, lam_init, B, T):
    z = _matmul_fullk(h, w_in, name="even_in")
    qh = _segment(z, 0, DA_WIDTH, q_norm, tabs64, T, gs=64, do_norm=True, do_rope=True,
                  scale=DA_HEAD_DIM ** -0.5 * LOG2E)
    kh = _segment(z, DA_WIDTH, DA_WIDTH, k_norm, tabs64, T, gs=64, do_norm=True, do_rope=True)
    vh = _segment(z, 2 * DA_WIDTH, DA_WIDTH, None, tabs64, T, gs=64, do_norm=False, do_rope=False)
    o = _diff_attention(qh, kh, vh, lam_p, subln, lam_init, B, T)
    y = _short_conv(z, conv_w, B, T)
    cat = jnp.concatenate([o, y], axis=1)
    return _matmul_fullk(cat, w_out, resid=xf, name="even_out"), z


def _odd_mixer(xf, h, w_in_p, w_out, mu_p, w0, w2, a0, a2, v0, v2, g2, k_k, k_a, r_k, lnx_w, lnx_b,
               q_norm, k_norm, idxk_norm, z_first, tabs64, tabs128, B, T, ksel):
    z = _matmul_fullk(h, w_in_p, name="odd_in")
    r, ld, k, v, a, g = _rwkv_prep(z, z_first, mu_p, w0, a0, v0, w2, a2, v2, g2, T)
    rw_out = _rwkv(r, ld, k, v, a, g, k_k, k_a, r_k.reshape(-1), lnx_w, lnx_b, B, T)
    qd = _segment(z, OD_Q, SA_WIDTH, q_norm, tabs128, T, gs=128, do_norm=True, do_rope=True,
                  scale=SA_HEAD_DIM ** -0.5 * LOG2E)
    kd = _segment(z, OD_KD, LANES, k_norm, tabs128, T, gs=128, do_norm=True, do_rope=True)
    vd = _segment(z, OD_VDD, LANES, None, tabs128, T, gs=128, do_norm=False, do_rope=False)
    qi = _segment(z, OD_QI, IDX_HEADS * IDX_DIM, None, tabs64, T, gs=64, do_norm=False, do_rope=True)
    ki = _segment(z, OD_KI, LANES, idxk_norm, tabs64, T, gs=64, do_norm=True, do_rope=True)
    wi = _segment(z, OD_WI, LANES, None, tabs64, T, gs=64, do_norm=False, do_rope=False,
                  scale=IDX_HEADS ** -0.5 * IDX_DIM ** -0.5, out_dtype=F32)
    sa_out = _dsa(qd, qi, wi, kd, vd, ki, B, T, ksel)
    cat = jnp.concatenate([rw_out, sa_out], axis=1)
    return _matmul_fullk(cat, w_out, resid=xf, name="odd_out")


def kernel(x, mix_norm, ffn_norm, ffn_gate, ffn_up, ffn_down, ev_w_in, ev_w_out, da_q_norm, da_k_norm, da_lambda, da_subln, sc_conv, od_w_in, od_w_out, rw_mu, rw_w0, rw_w2, rw_a0, rw_a2, rw_v0, rw_v2, rw_g2, rw_k_k, rw_k_a, rw_r_k, rw_lnx_w, rw_lnx_b, sa_q_norm, sa_k_norm, idx_k_norm):
    B, T, D = x.shape
    M = B * T
    ksel = min(TOPK_MAX, T // 4)
    xf = x.reshape(M, D)
    tabs64 = _rope_tables(T, 64)
    tabs128 = _rope_tables(T, 128)
    z_first = None
    for i in range(DEPTH):
        h = _rmsnorm(xf, mix_norm[i])
        if i % 2 == 0:
            e = i // 2
            lam_init = 0.8 - 0.6 * math.exp(-0.3 * i)
            xf, z = _even_mixer(xf, h, _cast_pad(ev_w_in[e]), _cast_pad(ev_w_out[e]), da_q_norm[e],
                                da_k_norm[e], da_lambda[e], da_subln[e], sc_conv[e], tabs64, lam_init, B, T)
            if z_first is None:
                z_first = z
        else:
            o = i // 2
            xf = _odd_mixer(xf, h, _odd_in_weight(od_w_in[o]), _cast_pad(od_w_out[o]), _odd_mu(rw_mu[o]),
                            rw_w0[o], rw_w2[o], rw_a0[o], rw_a2[o], rw_v0[o], rw_v2[o], rw_g2[o],
                            rw_k_k[o], rw_k_a[o], rw_r_k[o], rw_lnx_w[o], rw_lnx_b[o],
                            sa_q_norm[o], sa_k_norm[o], idx_k_norm[o], z_first, tabs64, tabs128, B, T, ksel)
        h = _rmsnorm(xf, ffn_norm[i])
        wg = _cast_pad(ffn_gate[i], cols_p=FFN_PAD)
        wu = _cast_pad(ffn_up[i], cols_p=FFN_PAD)
        wd = _cast_pad(ffn_down[i], rows_p=FFN_PAD)
        hid = _matmul_fullk(h, wg, b2=wu, out_dtype=BF16, name="ffn_in")
        xf = _matmul(hid, wd, resid=xf, name="ffn_out")
    return xf.reshape(B, T, D)
```

```python
import functools
import math

import jax
import jax.numpy as jnp
from jax import lax
from jax.experimental import pallas as pl
from jax.experimental.pallas import tpu as pltpu

F32 = jnp.float32
BF16 = jnp.bfloat16
I32 = jnp.int32

D_MODEL = 4096
DEPTH = 4
DA_WIDTH = 2048
DA_HEADS = 16
DA_HEAD_DIM = 64
SC_WIDTH = 2048
CONV_W = 3
RW_WIDTH = 2048
RW_HEAD_DIM = 64
RW_HEADS = 32
RW_GROUP = 4
RW_GW = RW_GROUP * RW_HEAD_DIM
W_LORA, A_LORA, V_LORA, G_LORA = 96, 96, 64, 256
LNX_EPS = 64e-5
SA_WIDTH = 2048
SA_HEAD_DIM = 128
SA_HEADS = 16
IDX_HEADS = 16
IDX_DIM = 64
TOPK_MAX = 256
FFN_HIDDEN = 11008
ROPE_THETA = 10000.0
EPS = 1e-6

LANES = 128
SUBLANES = 8
V7X_VMEM_BYTES = 64 * 1024 * 1024
VMEM_LIMIT = (V7X_VMEM_BYTES * 3) // 4

FFN_PAD = 11264
OD_R, OD_K, OD_V = 0, 2048, 4096
OD_LORA = 3 * RW_WIDTH
LORA_W = W_LORA + A_LORA + V_LORA + G_LORA
OD_RW_END = OD_LORA + LORA_W
OD_Q = OD_RW_END
OD_KD = OD_Q + SA_WIDTH
OD_VDD = OD_KD + SA_HEAD_DIM
OD_QI = OD_VDD + SA_HEAD_DIM
OD_KI = OD_QI + IDX_HEADS * IDX_DIM
OD_PAD = 10240
LOG2E = math.log2(math.e)
INT_MIN = -2 ** 31
NEG_INF_KEY = -2139095041


def _cparams(sem):
    return pltpu.CompilerParams(dimension_semantics=sem, vmem_limit_bytes=VMEM_LIMIT)


def _mm_body(*refs, nk, mode):
    if mode == "swiglu":
        a_ref, b_ref, b2_ref, o_ref, acc_ref, acc2_ref = refs
    elif mode == "resid":
        a_ref, b_ref, r_ref, o_ref, acc_ref = refs
    else:
        a_ref, b_ref, o_ref, acc_ref = refs
    k = pl.program_id(2)

    @pl.when(k == 0)
    def _init():
        acc_ref[...] = jnp.zeros_like(acc_ref)
        if mode == "swiglu":
            acc2_ref[...] = jnp.zeros_like(acc2_ref)

    a = a_ref[...]
    acc_ref[...] += jnp.dot(a, b_ref[...], preferred_element_type=F32)
    if mode == "swiglu":
        acc2_ref[...] += jnp.dot(a, b2_ref[...], preferred_element_type=F32)

    @pl.when(k == nk - 1)
    def _fin():
        if mode == "swiglu":
            g = acc_ref[...]
            o_ref[...] = (g * (1.0 / (1.0 + jnp.exp(-g))) * acc2_ref[...]).astype(o_ref.dtype)
        elif mode == "resid":
            o_ref[...] = r_ref[...] + acc_ref[...]
        else:
            o_ref[...] = acc_ref[...].astype(o_ref.dtype)


def _epilogue(mode, acc, acc2, resid, dtype):
    if mode == "swiglu":
        return (acc * (1.0 / (1.0 + jnp.exp(-acc))) * acc2).astype(dtype)
    if mode == "resid":
        return resid + acc
    return acc.astype(dtype)


def _mm_fullk_body(*refs, mode):
    if mode == "swiglu":
        a_ref, b_ref, b2_ref, o_ref = refs
    elif mode == "resid":
        a_ref, b_ref, r_ref, o_ref = refs
    else:
        a_ref, b_ref, o_ref = refs
    a = a_ref[...]
    acc = jnp.dot(a, b_ref[...], preferred_element_type=F32)
    acc2 = jnp.dot(a, b2_ref[...], preferred_element_type=F32) if mode == "swiglu" else None
    o_ref[...] = _epilogue(mode, acc, acc2, r_ref[...] if mode == "resid" else None, o_ref.dtype)


def _matmul_fullk(a, b, *, b2=None, resid=None, out_dtype=F32, tm=1024, tn=512, name="mm"):
    M, K = a.shape
    N = b.shape[1]
    tm, tn = min(tm, M), min(tn, N)
    assert M % tm == 0 and N % tn == 0, (a.shape, b.shape, tm, tn)
    mode = "swiglu" if b2 is not None else ("resid" if resid is not None else "plain")
    in_specs = [pl.BlockSpec((tm, K), lambda i, j: (i, 0)), pl.BlockSpec((K, tn), lambda i, j: (0, j))]
    args = [a, b]
    if b2 is not None:
        in_specs.append(pl.BlockSpec((K, tn), lambda i, j: (0, j)))
        args.append(b2)
    if resid is not None:
        in_specs.append(pl.BlockSpec((tm, tn), lambda i, j: (i, j)))
        args.append(resid)
    return pl.pallas_call(
        functools.partial(_mm_fullk_body, mode=mode),
        grid=(M // tm, N // tn),
        in_specs=in_specs,
        out_specs=pl.BlockSpec((tm, tn), lambda i, j: (i, j)),
        out_shape=jax.ShapeDtypeStruct((M, N), out_dtype),
        compiler_params=_cparams(("parallel", "arbitrary")),
        name=name,
    )(*args)


def _cast_body(x_ref, o_ref, *, rows, cols, tr, tc):
    x = x_ref[...]
    r = pl.program_id(0) * tr + lax.broadcasted_iota(I32, x.shape, 0)
    c = pl.program_id(1) * tc + lax.broadcasted_iota(I32, x.shape, 1)
    o_ref[...] = jnp.where((r < rows) & (c < cols), x, 0.0).astype(o_ref.dtype)


def _cast_pad(w, layer, rows_p=None, cols_p=None, *, tr=512, tc=1024):
    _, rows, cols = w.shape
    rows_p, cols_p = rows_p or rows, cols_p or cols
    tr, tc = min(tr, rows_p), min(tc, cols_p)
    assert rows_p % tr == 0 and cols_p % tc == 0, (w.shape, rows_p, cols_p)
    return pl.pallas_call(
        functools.partial(_cast_body, rows=rows, cols=cols, tr=tr, tc=tc),
        grid=(rows_p // tr, cols_p // tc),
        in_specs=[pl.BlockSpec((None, tr, tc), lambda i, j: (layer, i, j))],
        out_specs=pl.BlockSpec((tr, tc), lambda i, j: (i, j)),
        out_shape=jax.ShapeDtypeStruct((rows_p, cols_p), BF16),
        compiler_params=_cparams(("parallel", "parallel")),
        name="cast_pad",
    )(w)


def _matmul(a, b, *, b2=None, resid=None, out_dtype=F32, tm=1024, tn=1024, tk=1024, name="mm"):
    M, K = a.shape
    N = b.shape[1]
    tm, tn, tk = min(tm, M), min(tn, N), min(tk, K)
    assert M % tm == 0 and N % tn == 0 and K % tk == 0, (a.shape, b.shape, tm, tn, tk)
    nk = K // tk
    mode = "swiglu" if b2 is not None else ("resid" if resid is not None else "plain")
    in_specs = [pl.BlockSpec((tm, tk), lambda i, j, k: (i, k)),
                pl.BlockSpec((tk, tn), lambda i, j, k: (k, j))]
    args = [a, b]
    scratch = [pltpu.VMEM((tm, tn), F32)]
    if b2 is not None:
        in_specs.append(pl.BlockSpec((tk, tn), lambda i, j, k: (k, j)))
        args.append(b2)
        scratch.append(pltpu.VMEM((tm, tn), F32))
    if resid is not None:
        in_specs.append(pl.BlockSpec((tm, tn), lambda i, j, k: (i, j)))
        args.append(resid)
    return pl.pallas_call(
        functools.partial(_mm_body, nk=nk, mode=mode),
        grid=(M // tm, N // tn, nk),
        in_specs=in_specs,
        out_specs=pl.BlockSpec((tm, tn), lambda i, j, k: (i, j)),
        out_shape=jax.ShapeDtypeStruct((M, N), out_dtype),
        scratch_shapes=scratch,
        compiler_params=_cparams(("parallel", "parallel", "arbitrary")),
        name=name,
    )(*args)


def _rms_body(x_ref, g_ref, o_ref):
    x = x_ref[...]
    ms = jnp.mean(x * x, axis=-1, keepdims=True)
    o_ref[...] = (x * lax.rsqrt(ms + EPS) * g_ref[...]).astype(o_ref.dtype)


def _rmsnorm(x, g, *, tr=256):
    M, D = x.shape
    tr = min(tr, M)
    return pl.pallas_call(
        _rms_body,
        grid=(M // tr,),
        in_specs=[pl.BlockSpec((tr, D), lambda i: (i, 0)),
                  pl.BlockSpec((1, D), lambda i: (0, 0))],
        out_specs=pl.BlockSpec((tr, D), lambda i: (i, 0)),
        out_shape=jax.ShapeDtypeStruct((M, D), BF16),
        compiler_params=_cparams(("parallel",)),
        name="rmsnorm",
    )(x, g.reshape(1, D))


def _rope_tables(T, gs):
    half = gs // 2
    inv = ROPE_THETA ** (-jnp.arange(half, dtype=F32) / half)
    ang = jnp.arange(T, dtype=jnp.int32).astype(F32)[:, None] * inv[None, :]
    cos, sin = jnp.cos(ang), jnp.sin(ang)
    cosg = jnp.concatenate([cos, cos], axis=1)
    sing = jnp.concatenate([-sin, sin], axis=1)
    reps = LANES // gs
    return jnp.tile(cosg, (1, reps)), jnp.tile(sing, (1, reps))


def _group_ones(gs):
    r = jnp.arange(LANES)
    return (r[:, None] // gs == r[None, :] // gs).astype(BF16)


def _seg_body(x_ref, g_ref, cos_ref, sin_ref, bd_ref, o_ref, *, gs, do_norm, do_rope, scale, pick):
    half = gs // 2
    lane = lax.broadcasted_iota(I32, (x_ref.shape[0], LANES), 1)
    for t in range(x_ref.shape[1] // LANES):
        sl = slice(t * LANES, (t + 1) * LANES)
        x = x_ref[:, sl]
        if pick == "dup_low":
            x = jnp.where(lane < LANES // 2, x, pltpu.roll(x, LANES // 2, 1))
        elif pick == "high16":
            x = jnp.where(lane < IDX_HEADS, pltpu.roll(x, LANES // 2, 1), 0.0)
        if do_norm:
            x2 = x * x
            hi = x2.astype(BF16)
            lo = (x2 - hi.astype(F32)).astype(BF16)
            bd = bd_ref[...]
            ssum = jnp.dot(hi, bd, preferred_element_type=F32) + jnp.dot(lo, bd, preferred_element_type=F32)
            x = x * lax.rsqrt(ssum * (1.0 / gs) + EPS) * g_ref[...]
        if do_rope:
            if gs == LANES:
                rot = pltpu.roll(x, half, 1)
            else:
                rot = jnp.where((lane & (gs - 1)) < half, pltpu.roll(x, LANES - half, 1), pltpu.roll(x, half, 1))
            x = x * cos_ref[...] + rot * sin_ref[...]
        if scale != 1.0:
            x = x * scale
        o_ref[:, sl] = x.astype(o_ref.dtype)


def _segment(z, col_off, width, gain, tables, T, *, gs, do_norm, do_rope, scale=1.0, out_dtype=BF16, pick=None,
             tr=512, max_cw=1024):
    M = z.shape[0]
    tr = min(tr, T)
    nrb = T // tr
    cw = math.gcd(math.gcd(col_off, width), max_cw)
    assert cw % LANES == 0, (col_off, width)
    cb = col_off // cw
    cos, sin = tables
    if gain is None:
        gain = jnp.ones((gs,), F32)
    gt = jnp.tile(gain.astype(F32), LANES // gs).reshape(1, LANES)
    return pl.pallas_call(
        functools.partial(_seg_body, gs=gs, do_norm=do_norm, do_rope=do_rope, scale=scale, pick=pick),
        grid=(M // tr, width // cw),
        in_specs=[pl.BlockSpec((tr, cw), lambda i, j: (i, cb + j)),
                  pl.BlockSpec((1, LANES), lambda i, j: (0, 0)),
                  pl.BlockSpec((tr, LANES), lambda i, j: (i % nrb, 0)),
                  pl.BlockSpec((tr, LANES), lambda i, j: (i % nrb, 0)),
                  pl.BlockSpec((LANES, LANES), lambda i, j: (0, 0))],
        out_specs=pl.BlockSpec((tr, cw), lambda i, j: (i, j)),
        out_shape=jax.ShapeDtypeStruct((M, width), out_dtype),
        compiler_params=_cparams(("parallel", "parallel")),
        name="segment",
    )(z, gt, cos, sin, _group_ones(gs))


def _dot_nt(a, b):
    return lax.dot_general(a, b, (((1,), (1,)), ((), ())), preferred_element_type=F32)


def _softmax_steps(ss, vt, m_ref, l_ref, acc_ref, idxs, guard):
    m_prev = [m_ref[i] for i in idxs]
    m_next = [jnp.maximum(mp, jnp.max(s, axis=0, keepdims=True)) for mp, s in zip(m_prev, ss)]
    m_use = [jnp.where(mn == -jnp.inf, 0.0, mn) for mn in m_next] if guard else m_next
    ps = [jnp.exp2(s - mu) for s, mu in zip(ss, m_use)]
    alpha = [jnp.exp2(mp - mu) for mp, mu in zip(m_prev, m_use)]
    pv = [jnp.dot(vt, p.astype(BF16), preferred_element_type=F32) for p in ps]
    for n, i in enumerate(idxs):
        l_ref[i] = alpha[n] * l_ref[i] + jnp.sum(ps[n], axis=0, keepdims=True)
        acc_ref[i] = alpha[n] * acc_ref[i] + pv[n]
        m_ref[i] = m_next[n]


def _dattn_body(lam_ref, sub_ref, q_ref, k_ref, v_ref, o_ref, m_ref, l_ref, acc_ref, *, tq, tk, lam_init):
    i = pl.program_id(2)
    q = q_ref[...]
    lane = lax.broadcasted_iota(I32, q.shape, 1)
    zero = jnp.zeros_like(q)
    qs = (jnp.where(lane < DA_HEAD_DIM, q, zero), jnp.where(lane >= DA_HEAD_DIM, q, zero))
    m_ref[...] = jnp.full(m_ref.shape, -jnp.inf, F32)
    l_ref[...] = jnp.zeros(l_ref.shape, F32)
    acc_ref[...] = jnp.zeros(acc_ref.shape, F32)
    reps = tk // LANES

    def tile_step(j, masked):
        off = pl.multiple_of(j * tk, tk)
        ks = k_ref[pl.ds(off, tk), :]
        vs = v_ref[pl.ds(off, tk), :]
        if masked:
            row = i * tq + lax.broadcasted_iota(I32, (tq, tk), 0)
            col = off + lax.broadcasted_iota(I32, (tq, tk), 1)
            vis = col <= row
        cs = range(2)
        s = [_dot_nt(qs[c], ks) for c in cs]
        if masked:
            s = [jnp.where(vis, x, -jnp.inf) for x in s]
        m_prev = [m_ref[c] for c in cs]
        m_next = [jnp.maximum(m_prev[c], jnp.max(s[c], axis=-1, keepdims=True)) for c in cs]
        p = [jnp.exp2(s[c] - jnp.concatenate([m_next[c]] * reps, axis=1)) for c in cs]
        alpha = [jnp.exp2(m_prev[c] - m_next[c]) for c in cs]
        pv = [jnp.dot(p[c].astype(BF16), vs, preferred_element_type=F32) for c in cs]
        for c in cs:
            l_ref[c] = alpha[c] * l_ref[c] + jnp.sum(p[c], axis=-1, keepdims=True)
            acc_ref[c] = alpha[c] * acc_ref[c] + pv[c]
            m_ref[c] = m_next[c]

    n_full = (i * tq + 1) // tk
    n_all = (i * tq + tq + tk - 1) // tk

    def full_body(j, carry):
        tile_step(j, False)
        return carry

    def diag_body(j, carry):
        tile_step(j, True)
        return carry

    lax.fori_loop(0, n_full, full_body, 0)
    lax.fori_loop(n_full, n_all, diag_body, 0)

    lp = lam_ref[...]
    lam = (jnp.exp(jnp.sum(lp[0:1] * lp[1:2], axis=-1, keepdims=True))
           - jnp.exp(jnp.sum(lp[2:3] * lp[3:4], axis=-1, keepdims=True)) + lam_init)
    o = acc_ref[0] * (1.0 / l_ref[0]) - lam * (acc_ref[1] * (1.0 / l_ref[1]))
    ms = jnp.mean(o * o, axis=-1, keepdims=True)
    o = o * lax.rsqrt(ms + EPS) * sub_ref[...] * (1.0 - lam_init)
    o_ref[...] = o.astype(o_ref.dtype)


def _diff_attention(qh, kh, vh, lam_p, subln, lam_init, B, T, *, tq=512, tk=512):
    M = qh.shape[0]
    tq, tk = min(tq, T), min(tk, T)
    nq = T // tq
    return pl.pallas_call(
        functools.partial(_dattn_body, tq=tq, tk=tk, lam_init=lam_init),
        grid=(B, DA_HEADS, nq),
        in_specs=[pl.BlockSpec((4, DA_HEAD_DIM), lambda b, h, i: (0, 0)),
                  pl.BlockSpec((1, LANES), lambda b, h, i: (0, 0)),
                  pl.BlockSpec((tq, LANES), lambda b, h, i: (b * nq + i, h)),
                  pl.BlockSpec((T, LANES), lambda b, h, i: (b, h)),
                  pl.BlockSpec((T, LANES), lambda b, h, i: (b, h))],
        out_specs=pl.BlockSpec((tq, LANES), lambda b, h, i: (b * nq + i, h)),
        out_shape=jax.ShapeDtypeStruct((M, DA_WIDTH), BF16),
        scratch_shapes=[pltpu.VMEM((2, tq, LANES), F32)] * 3,
        compiler_params=_cparams(("parallel", "parallel", "arbitrary")),
        name="diff_attention",
    )(lam_p.astype(F32), subln.astype(F32).reshape(1, LANES), qh, kh, vh)


def _conv_body(gb_ref, gc_ref, u_ref, w_ref, o_ref, sh_ref, *, T):
    cu = gc_ref[...] * u_ref[...]
    sh_ref[0:SUBLANES, :] = jnp.zeros((SUBLANES, LANES), F32)
    sh_ref[SUBLANES:SUBLANES + T, :] = cu
    w = w_ref[...]
    conv = (sh_ref[SUBLANES - 2:SUBLANES - 2 + T, :] * w[0:1]
            + sh_ref[SUBLANES - 1:SUBLANES - 1 + T, :] * w[1:2]
            + cu * w[2:3])
    o_ref[...] = (gb_ref[...] * conv).astype(o_ref.dtype)


def _short_conv(z, conv_w, B, T):
    M = z.shape[0]
    nc = SC_WIDTH // LANES
    base = 3 * DA_WIDTH // LANES
    return pl.pallas_call(
        functools.partial(_conv_body, T=T),
        grid=(B, nc),
        in_specs=[pl.BlockSpec((T, LANES), lambda b, c: (b, base + c)),
                  pl.BlockSpec((T, LANES), lambda b, c: (b, base + nc + c)),
                  pl.BlockSpec((T, LANES), lambda b, c: (b, base + 2 * nc + c)),
                  pl.BlockSpec((CONV_W, LANES), lambda b, c: (0, c))],
        out_specs=pl.BlockSpec((T, LANES), lambda b, c: (b, c)),
        out_shape=jax.ShapeDtypeStruct((M, SC_WIDTH), BF16),
        scratch_shapes=[pltpu.VMEM((T + SUBLANES, LANES), F32)],
        compiler_params=_cparams(("parallel", "parallel")),
        name="short_conv",
    )(z, z, z, conv_w.astype(F32))


def _sigmoid(x):
    return 1.0 / (1.0 + jnp.exp(-x))


def _rwprep_body(z_ref, zh_ref, vf_ref, mu_ref, w0_ref, a0_ref, v0_ref, w2_ref, a2_ref, v2_ref, g2_ref,
                 r_o, ld_o, k_o, v_o, a_o, g_o, sh_ref, *, tr, nrb):
    i = pl.program_id(0)
    first = (i % nrb) == 0
    sh_ref[SUBLANES - 1:SUBLANES, :] = jnp.where(first, 0.0, zh_ref[SUBLANES - 1:SUBLANES, :])
    sh_ref[SUBLANES:SUBLANES + tr, :] = z_ref[...]

    def shifted(lo, hi):
        zc = z_ref[:, lo:hi]
        zp = sh_ref[SUBLANES - 1:SUBLANES - 1 + tr, lo:hi]
        return zc + (zp - zc) * mu_ref[:, lo:hi]

    r_o[...] = shifted(OD_R, OD_K)
    k_o[...] = shifted(OD_K, OD_V)
    lora = shifted(OD_LORA, OD_RW_END)
    lora_b = lora.astype(BF16)
    lw = w0_ref[...] + jnp.dot(jnp.tanh(lora).astype(BF16), w2_ref[...], preferred_element_type=F32)
    nlw = -lw
    softplus = jnp.maximum(nlw, 0.0) + jnp.log(1.0 + jnp.exp(-jnp.abs(nlw)))
    ld_o[...] = -jnp.exp(-softplus - 0.5)
    a_o[...] = _sigmoid(a0_ref[...] + jnp.dot(lora_b, a2_ref[...], preferred_element_type=F32))
    v = shifted(OD_V, OD_LORA)
    vg = _sigmoid(v0_ref[...] + jnp.dot(lora_b, v2_ref[...], preferred_element_type=F32))
    v_o[...] = v + (vf_ref[...] - v) * vg
    g_o[...] = jnp.dot(_sigmoid(lora).astype(BF16), g2_ref[...], preferred_element_type=F32)


def _place_rows(w, start):
    return jnp.pad(w, ((start, LORA_W - start - w.shape[0]), (0, 0))).astype(BF16)


def _rwkv_prep(z, z_first, mu_p, w0, a0, v0, w2, a2, v2, g2, T, *, tr=128):
    M = z.shape[0]
    tr = min(tr, T)
    nrb = T // tr
    W = OD_RW_END
    hb = tr // SUBLANES
    row = lambda a: a.astype(F32).reshape(1, RW_WIDTH)
    full = lambda shape: pl.BlockSpec(shape, lambda i: (0, 0))
    out = jax.ShapeDtypeStruct((M, RW_WIDTH), F32)
    ospec = pl.BlockSpec((tr, RW_WIDTH), lambda i: (i, 0))
    return pl.pallas_call(
        functools.partial(_rwprep_body, tr=tr, nrb=nrb),
        grid=(M // tr,),
        in_specs=[pl.BlockSpec((tr, W), lambda i: (i, 0)),
                  pl.BlockSpec((SUBLANES, W), lambda i: (jnp.maximum(i * hb - 1, 0), 0)),
                  pl.BlockSpec((tr, RW_WIDTH), lambda i: (i, 2 * DA_WIDTH // RW_WIDTH)),
                  full((1, W)), full((1, RW_WIDTH)), full((1, RW_WIDTH)), full((1, RW_WIDTH)),
                  full((LORA_W, RW_WIDTH)), full((LORA_W, RW_WIDTH)), full((LORA_W, RW_WIDTH)),
                  full((LORA_W, RW_WIDTH))],
        out_specs=[ospec] * 6,
        out_shape=[out] * 6,
        scratch_shapes=[pltpu.VMEM((tr + SUBLANES, W), F32)],
        compiler_params=_cparams(("parallel",)),
        name="rwkv_prep",
    )(z, z, z_first, mu_p, row(w0), row(a0), row(v0),
      _place_rows(w2, 0), _place_rows(a2, W_LORA), _place_rows(v2, W_LORA + A_LORA),
      _place_rows(g2, W_LORA + A_LORA + V_LORA))


def _split3(x):
    hi = x.astype(BF16)
    r1 = x - hi.astype(F32)
    mid = r1.astype(BF16)
    lo = (r1 - mid.astype(F32)).astype(BF16)
    return hi, mid, lo


def _mm(a, b):
    return jnp.dot(a.astype(BF16), b.astype(BF16), preferred_element_type=F32)


def _mm_tn(a, b):
    return jnp.dot(a.T.astype(BF16), b.astype(BF16), preferred_element_type=F32)


def _rwkv_body(r_ref, ld_ref, k_ref, v_ref, a_ref, g_ref, kk_ref, ka_ref, rk_ref, lw_ref, lb_ref,
               o_ref, st_ref, *, ng, L):
    c = pl.program_id(2)

    @pl.when(c == 0)
    def _init():
        st_ref[...] = jnp.zeros_like(st_ref)

    N = RW_HEAD_DIM
    S = RW_GROUP * L
    ri = lax.broadcasted_iota(I32, (S, S), 0)
    ci = lax.broadcasted_iota(I32, (S, S), 1)
    same = (ri // L) == (ci // L)
    incl = same & (ci <= ri)
    strict = same & (ci < ri)
    eye_s = (ci == ri).astype(F32)
    blk16 = (ri // 16) == (ci // 16)
    hmask = (lax.broadcasted_iota(I32, (S, RW_GW), 0) // L) == (lax.broadcasted_iota(I32, (S, RW_GW), 1) // N)
    rl = lax.broadcasted_iota(I32, (L, L), 0)
    cl = lax.broadcasted_iota(I32, (L, L), 1)
    tri = (cl <= rl).astype(BF16)
    rn = lax.broadcasted_iota(I32, (RW_GW, RW_GW), 0)
    cn = lax.broadcasted_iota(I32, (RW_GW, RW_GW), 1)
    eye_g = rn == cn
    ones_g = ((rn // N) == (cn // N)).astype(BF16)

    def gsum(x):
        hi = x.astype(BF16)
        lo = (x - hi.astype(F32)).astype(BF16)
        return jnp.dot(hi, ones_g, preferred_element_type=F32) + jnp.dot(lo, ones_g, preferred_element_type=F32)

    def tile(x):
        return jnp.concatenate([x] * RW_GROUP, axis=0)

    def stack(x):
        return jnp.where(hmask, tile(x), 0.0)

    G = range(ng)
    sls = [slice(gi * RW_GW, (gi + 1) * RW_GW) for gi in G]
    each = lambda f, *lists: [f(*xs) for xs in zip(*lists)]
    r = [r_ref[:, sl] for sl in sls]
    ld = [ld_ref[:, sl] for sl in sls]
    k = [k_ref[:, sl] for sl in sls]
    v = [v_ref[:, sl] for sl in sls]
    a = [a_ref[:, sl] for sl in sls]
    kk = [k[gi] * kk_ref[:, sls[gi]] for gi in G]
    kk = each(lambda x: x / jnp.maximum(jnp.sqrt(gsum(x * x)), 1e-12), kk)
    k2 = [k[gi] * (1.0 + (a[gi] - 1.0) * ka_ref[:, sls[gi]]) for gi in G]
    bv = each(lambda x, y: x * y, kk, a)
    parts = each(_split3, ld)
    cum = each(lambda p: (jnp.dot(tri, p[0], preferred_element_type=F32) + jnp.dot(tri, p[1], preferred_element_type=F32)
                          + jnp.dot(tri, p[2], preferred_element_type=F32)), parts)
    clast = each(lambda c_: c_[L - 1:L, :], cum)
    e_neg = each(lambda c_: jnp.exp(-c_), cum)
    e_l = each(lambda cl_, c_: jnp.exp(cl_ - c_), clast, cum)
    p_l = each(jnp.exp, clast)
    at = each(lambda x, c_, l_: stack(-x * jnp.exp(c_ - l_)), kk, cum, ld)
    rt = each(lambda x, c_: stack(x * jnp.exp(c_)), r, cum)
    vs = each(stack, v)
    bh = each(lambda x, e: stack(x * e), bv, e_l)
    kh = each(lambda x, e: stack(x * e), k2, e_l)
    btb = each(lambda x, e: tile(x * e).astype(BF16), bv, e_neg)
    ktb = each(lambda x, e: tile(x * e).astype(BF16), k2, e_neg)
    atb = each(lambda x: x.astype(BF16), at)
    rtb = each(lambda x: x.astype(BF16), rt)
    mab = each(lambda x, y: jnp.where(strict, _dot_nt(x, y), 0.0), atb, btb)
    mak = each(lambda x, y: jnp.where(strict, _dot_nt(x, y), 0.0), atb, ktb)
    mrb = each(lambda x, y: jnp.where(incl, _dot_nt(x, y), 0.0), rtb, btb)
    mrk = each(lambda x, y: jnp.where(incl, _dot_nt(x, y), 0.0), rtb, ktb)
    nd = each(lambda m: jnp.where(blk16, m, 0.0), mab)
    n2 = each(_mm, nd, nd)
    mv = each(_mm, mak, vs)
    n4 = each(_mm, n2, n2)
    t = each(lambda n, n2_: eye_s + n + _mm(eye_s + n, n2_), nd, n2)
    n8 = each(_mm, n4, n4)
    t = each(lambda t_, n: t_ + _mm(t_, n), t, n4)
    t = each(lambda t_, n: t_ + _mm(t_, n), t, n8)
    size = 16
    while size < L:
        off = ((ri // size) == (ci // size) + 1) & ((ri // (2 * size)) == (ci // (2 * size)))
        u = each(lambda t_, m: _mm(t_, jnp.where(off, m, 0.0)), t, mab)
        t = each(lambda t_, u_: t_ + _mm(u_, t_), t, u)
        size *= 2
    wm = each(_mm, t, at)
    ul = each(_mm, t, mv)
    qe = each(lambda x, m, w: x + _mm(m, w), rt, mrb, wm)
    yl = each(lambda m, u_, m2, x: _mm(m, u_) + _mm(m2, x), mrb, ul, mrk, vs)
    gm = each(lambda b, w, p: _mm_tn(b, w) + jnp.where(eye_g, p, 0.0), bh, wm, p_l)
    hm = each(lambda b, u_, k_, x: _mm_tn(b, u_) + _mm_tn(k_, x), bh, ul, kh, vs)
    st = [st_ref[gi] for gi in G]
    ys = each(lambda q_, s_, y_: _mm(q_, s_) + y_, qe, st, yl)
    st_new = each(lambda g_, s_, h_: _mm(g_, s_) + h_, gm, st, hm)
    for gi in G:
        st_ref[gi] = st_new[gi]
        sl = sls[gi]
        y = ys[gi][0:L]
        for hh in range(1, RW_GROUP):
            y = y + ys[gi][hh * L:(hh + 1) * L]
        mean = gsum(y) * (1.0 / N)
        d = y - mean
        var = gsum(d * d) * (1.0 / N)
        yn = d * lax.rsqrt(var + LNX_EPS) * lw_ref[:, sl] + lb_ref[:, sl]
        yn = yn + gsum(r[gi] * k2[gi] * rk_ref[:, sl]) * v[gi]
        o_ref[:, sl] = (yn * g_ref[:, sl]).astype(o_ref.dtype)


def _rwkv(r, ld, k, v, a, g, k_k, k_a, r_k, lnx_w, lnx_b, B, T, *, ng=4, L=64):
    M = r.shape[0]
    L = min(L, T)
    nc = T // L
    W = ng * RW_GW
    blk = pl.BlockSpec((L, W), lambda b, hg, c: (b * nc + c, hg))
    par = pl.BlockSpec((1, W), lambda b, hg, c: (0, hg))
    row = lambda p: p.astype(F32).reshape(1, RW_WIDTH)
    return pl.pallas_call(
        functools.partial(_rwkv_body, ng=ng, L=L),
        grid=(B, RW_WIDTH // W, nc),
        in_specs=[blk] * 6 + [par] * 5,
        out_specs=blk,
        out_shape=jax.ShapeDtypeStruct((M, RW_WIDTH), BF16),
        scratch_shapes=[pltpu.VMEM((ng, RW_GW, RW_GW), F32)],
        compiler_params=_cparams(("parallel", "parallel", "arbitrary")),
        name="rwkv7_chunk",
    )(r, ld, k, v, a, g, row(k_k), row(k_a), row(r_k), row(lnx_w), row(lnx_b))


def _dsa_body(qd_ref, qi_ref, wi_ref, kd_ref, vd_ref, ki_ref, o_ref,
              keys_ref, bias_ref, qim_ref, cut_ref, m_ref, l_ref, acc_ref, *, tq, tk, ksel, T, hg):
    i = pl.program_id(1)
    nkt = (i * tq + tq + tk - 1) // tk
    krow = lax.broadcasted_iota(I32, (tk, tq), 0)
    qpos = i * tq + lax.broadcasted_iota(I32, (tk, tq), 1)
    low_half = lax.broadcasted_iota(I32, (tq, LANES), 1) < IDX_DIM
    for h in range(IDX_HEADS):
        qt = qi_ref[:, (h // 2) * LANES:(h // 2 + 1) * LANES]
        qim_ref[h] = jnp.where(low_half if h % 2 == 0 else jnp.logical_not(low_half), qt,
                               jnp.zeros_like(qt)).astype(qim_ref.dtype)
    wit = wi_ref[...].T

    def score_tile(j, carry):
        off = pl.multiple_of(j * tk, tk)
        kt = ki_ref[pl.ds(off, tk), :]
        acc = jnp.zeros((tk, tq), F32)
        for h in range(IDX_HEADS):
            acc = acc + jnp.maximum(_dot_nt(kt, qim_ref[h]), 0.0) * wit[h:h + 1, :]
        acc = acc + 0.0
        sc = jnp.where((krow + off) <= qpos, acc, -jnp.inf)
        bits = pltpu.bitcast(sc, I32)
        keys_ref[pl.ds(off, tk), :] = bits ^ ((bits >> 31) & 0x7FFFFFFF)
        return carry

    lax.fori_loop(0, nkt, score_tile, 0)

    def count(pred):
        def body(j, c):
            off = pl.multiple_of(j * tk, tk)
            hit = jnp.where(pred(keys_ref[pl.ds(off, tk), :], krow + off), 1.0, 0.0)
            return c + jnp.sum(hit.reshape(tk // SUBLANES, SUBLANES, tq), axis=0)
        c = lax.fori_loop(0, nkt, body, jnp.zeros((SUBLANES, tq), F32))
        return jnp.sum(c, axis=0, keepdims=True)

    def bit_step(b, thr):
        cand = thr + jnp.left_shift(jnp.int32(1), 31 - b)
        cnt = count(lambda key, _: key >= cand)
        return jnp.where(cnt >= ksel, cand, thr)

    thr = lax.fori_loop(0, 32, bit_step, jnp.full((1, tq), INT_MIN, I32))

    n_gt = count(lambda key, _: key > thr)
    n_ge = count(lambda key, _: key >= thr)
    need = (n_ge > ksel) & (thr > NEG_INF_KEY)
    quota = ksel - n_gt
    cut_ref[...] = jnp.full((1, tq), T, I32)

    @pl.when(jnp.max(jnp.where(need, 1.0, 0.0)) > 0.0)
    def _ties():
        def pos_step(b, p):
            cand = p + jnp.left_shift(jnp.int32(1), (T.bit_length() - 1) - b)
            cnt = count(lambda key, pos: (key == thr) & (pos < cand))
            return jnp.where(cnt < quota, cand, p)
        p = lax.fori_loop(0, T.bit_length(), pos_step, jnp.zeros((1, tq), I32))
        cut_ref[...] = jnp.where(need, p, T)

    cut = cut_ref[...]
    thr_sel = jnp.maximum(thr, NEG_INF_KEY + 1)

    def bias_tile(j, carry):
        off = pl.multiple_of(j * tk, tk)
        key = keys_ref[pl.ds(off, tk), :]
        sel = (key > thr_sel) | ((key == thr_sel) & ((krow + off) <= cut))
        bias_ref[pl.ds(off, tk), :] = jnp.where(sel, 0.0, -jnp.inf)
        return carry

    lax.fori_loop(0, nkt, bias_tile, 0)

    m_ref[...] = jnp.full(m_ref.shape, -jnp.inf, F32)
    l_ref[...] = jnp.zeros(l_ref.shape, F32)
    acc_ref[...] = jnp.zeros(acc_ref.shape, F32)

    def attend(j, carry):
        off = pl.multiple_of(j * tk, tk)
        ks = kd_ref[pl.ds(off, tk), :]
        vt = vd_ref[pl.ds(off, tk), :].T
        bias = bias_ref[pl.ds(off, tk), :]
        for h0 in range(0, SA_HEADS, hg):
            heads = list(range(h0, h0 + hg))
            ss = [_dot_nt(ks, qd_ref[:, h * LANES:(h + 1) * LANES]) + bias for h in heads]
            _softmax_steps(ss, vt, m_ref, l_ref, acc_ref, heads, guard=True)
        return carry

    lax.fori_loop(0, nkt, attend, 0)
    for h in range(SA_HEADS):
        o_ref[:, h * LANES:(h + 1) * LANES] = (acc_ref[h] * (1.0 / l_ref[h])).T.astype(o_ref.dtype)


def _dsa(qd, qi, wi, kd, vd, ki, B, T, ksel, *, tq=256, tk=512, hg=8):
    M = qd.shape[0]
    tq, tk = min(tq, T), min(tk, T)
    nq = T // tq
    qblk = lambda w: pl.BlockSpec((tq, w), lambda b, i: (b * nq + i, 0))
    kblk = pl.BlockSpec((T, LANES), lambda b, i: (b, 0))
    return pl.pallas_call(
        functools.partial(_dsa_body, tq=tq, tk=tk, ksel=ksel, T=T, hg=hg),
        grid=(B, nq),
        in_specs=[qblk(SA_WIDTH), qblk(IDX_HEADS * IDX_DIM), qblk(LANES), kblk, kblk, kblk],
        out_specs=qblk(SA_WIDTH),
        out_shape=jax.ShapeDtypeStruct((M, SA_WIDTH), BF16),
        scratch_shapes=[pltpu.VMEM((T, tq), I32), pltpu.VMEM((T, tq), F32),
                        pltpu.VMEM((IDX_HEADS, tq, LANES), BF16), pltpu.VMEM((1, tq), I32),
                        pltpu.VMEM((SA_HEADS, 1, tq), F32), pltpu.VMEM((SA_HEADS, 1, tq), F32),
                        pltpu.VMEM((SA_HEADS, LANES, tq), F32)],
        compiler_params=_cparams(("parallel", "arbitrary")),
        name="dsa_attention",
    )(qd, qi, wi, kd, vd, ki)


def _even_mixer(xf, h, w_in, w_out, q_norm, k_norm, lam_p, subln, conv_w, tabs64, lam_init, B, T):
    z = _matmul_fullk(h, w_in, name="even_in")
    qh = _segment(z, 0, DA_WIDTH, q_norm, tabs64, T, gs=64, do_norm=True, do_rope=True,
                  scale=DA_HEAD_DIM ** -0.5 * LOG2E)
    kh = _segment(z, DA_WIDTH, DA_WIDTH, k_norm, tabs64, T, gs=64, do_norm=True, do_rope=True)
    vh = _segment(z, 2 * DA_WIDTH, DA_WIDTH, None, tabs64, T, gs=64, do_norm=False, do_rope=False)
    o = _diff_attention(qh, kh, vh, lam_p, subln, lam_init, B, T)
    y = _short_conv(z, conv_w, B, T)
    cat = jnp.concatenate([o, y], axis=1)
    return _matmul_fullk(cat, w_out, resid=xf, name="even_out"), z


def _odd_mixer(xf, h, w_in_p, w_out, mu_p, w0, w2, a0, a2, v0, v2, g2, k_k, k_a, r_k, lnx_w, lnx_b,
               q_norm, k_norm, idxk_norm, z_first, tabs64, tabs128, B, T, ksel):
    z = _matmul_fullk(h, w_in_p, name="odd_in")
    r, ld, k, v, a, g = _rwkv_prep(z, z_first, mu_p, w0, a0, v0, w2, a2, v2, g2, T)
    rw_out = _rwkv(r, ld, k, v, a, g, k_k, k_a, r_k.reshape(-1), lnx_w, lnx_b, B, T)
    qd = _segment(z, OD_Q, SA_WIDTH, q_norm, tabs128, T, gs=128, do_norm=True, do_rope=True,
                  scale=SA_HEAD_DIM ** -0.5 * LOG2E)
    kd = _segment(z, OD_KD, LANES, k_norm, tabs128, T, gs=128, do_norm=True, do_rope=True)
    vd = _segment(z, OD_VDD, LANES, None, tabs128, T, gs=128, do_norm=False, do_rope=False)
    qi = _segment(z, OD_QI, IDX_HEADS * IDX_DIM, None, tabs64, T, gs=64, do_norm=False, do_rope=True)
    ki = _segment(z, OD_KI, LANES, idxk_norm, tabs64, T, gs=64, do_norm=True, do_rope=True, pick="dup_low")
    wi = _segment(z, OD_KI, LANES, None, tabs64, T, gs=64, do_norm=False, do_rope=False,
                  scale=IDX_HEADS ** -0.5 * IDX_DIM ** -0.5, out_dtype=F32, pick="high16")
    sa_out = _dsa(qd, qi, wi, kd, vd, ki, B, T, ksel)
    cat = jnp.concatenate([rw_out, sa_out], axis=1)
    return _matmul_fullk(cat, w_out, resid=xf, name="odd_out")


def kernel(x, mix_norm, ffn_norm, ffn_gate, ffn_up, ffn_down, ev_w_in, ev_w_out, da_q_norm, da_k_norm, da_lambda, da_subln, sc_conv, od_w_in, od_w_out, rw_mu, rw_w0, rw_w2, rw_a0, rw_a2, rw_v0, rw_v2, rw_g2, rw_k_k, rw_k_a, rw_r_k, rw_lnx_w, rw_lnx_b, sa_q_norm, sa_k_norm, idx_k_norm):
    B, T, D = x.shape
    M = B * T
    ksel = min(TOPK_MAX, T // 4)
    xf = x.reshape(M, D)
    tabs64 = _rope_tables(T, 64)
    tabs128 = _rope_tables(T, 128)
    z_first = None
    for i in range(DEPTH):
        h = _rmsnorm(xf, mix_norm[i])
        if i % 2 == 0:
            e = i // 2
            lam_init = 0.8 - 0.6 * math.exp(-0.3 * i)
            xf, z = _even_mixer(xf, h, _cast_pad(ev_w_in, e), _cast_pad(ev_w_out, e), da_q_norm[e],
                                da_k_norm[e], da_lambda[e], da_subln[e], sc_conv[e], tabs64, lam_init, B, T)
            if z_first is None:
                z_first = z
        else:
            o = i // 2
            xf = _odd_mixer(xf, h, _cast_pad(od_w_in, o, cols_p=OD_PAD), _cast_pad(od_w_out, o),
                            rw_mu[o].astype(F32).reshape(1, OD_RW_END),
                            rw_w0[o], rw_w2[o], rw_a0[o], rw_a2[o], rw_v0[o], rw_v2[o], rw_g2[o],
                            rw_k_k[o], rw_k_a[o], rw_r_k[o], rw_lnx_w[o], rw_lnx_b[o],
                            sa_q_norm[o], sa_k_norm[o], idx_k_norm[o], z_first, tabs64, tabs128, B, T, ksel)
        h = _rmsnorm(xf, ffn_norm[i])
        wg = _cast_pad(ffn_gate, i, cols_p=FFN_PAD)
        wu = _cast_pad(ffn_up, i, cols_p=FFN_PAD)
        wd = _cast_pad(ffn_down, i, rows_p=FFN_PAD)
        hid = _matmul_fullk(h, wg, b2=wu, out_dtype=BF16, name="ffn_in")
        xf = _matmul(hid, wd, resid=xf, name="ffn_out")
    return xf.reshape(B, T, D)
```

```python
import functools
import math

import jax
import jax.numpy as jnp
from jax import lax
from jax.experimental import pallas as pl
from jax.experimental.pallas import tpu as pltpu

F32 = jnp.float32
BF16 = jnp.bfloat16
I32 = jnp.int32

D_MODEL = 4096
DEPTH = 4
DA_WIDTH = 2048
DA_HEADS = 16
DA_HEAD_DIM = 64
SC_WIDTH = 2048
CONV_W = 3
RW_WIDTH = 2048
RW_HEAD_DIM = 64
RW_HEADS = 32
RW_GROUP = 4
RW_GW = RW_GROUP * RW_HEAD_DIM
W_LORA, A_LORA, V_LORA, G_LORA = 96, 96, 64, 256
LNX_EPS = 64e-5
SA_WIDTH = 2048
SA_HEAD_DIM = 128
SA_HEADS = 16
IDX_HEADS = 16
IDX_DIM = 64
TOPK_MAX = 256
FFN_HIDDEN = 11008
ROPE_THETA = 10000.0
EPS = 1e-6

LANES = 128
SUBLANES = 8
V7X_VMEM_BYTES = 64 * 1024 * 1024
VMEM_LIMIT = (V7X_VMEM_BYTES * 3) // 4

FFN_PAD = 11264
OD_R, OD_K, OD_V = 0, 2048, 4096
OD_LORA = 3 * RW_WIDTH
LORA_W = W_LORA + A_LORA + V_LORA + G_LORA
OD_RW_END = OD_LORA + LORA_W
OD_Q = OD_RW_END
OD_KD = OD_Q + SA_WIDTH
OD_VDD = OD_KD + SA_HEAD_DIM
OD_QI = OD_VDD + SA_HEAD_DIM
OD_KI = OD_QI + IDX_HEADS * IDX_DIM
OD_PAD = 10240
LOG2E = math.log2(math.e)
INT_MIN = -2 ** 31
NEG_INF_KEY = -2139095041


def _cparams(sem):
    return pltpu.CompilerParams(dimension_semantics=sem, vmem_limit_bytes=VMEM_LIMIT)


def _mm_body(*refs, nk, mode):
    if mode == "swiglu":
        a_ref, b_ref, b2_ref, o_ref, acc_ref, acc2_ref = refs
    elif mode == "resid":
        a_ref, b_ref, r_ref, o_ref, acc_ref = refs
    else:
        a_ref, b_ref, o_ref, acc_ref = refs
    k = pl.program_id(2)

    @pl.when(k == 0)
    def _init():
        acc_ref[...] = jnp.zeros_like(acc_ref)
        if mode == "swiglu":
            acc2_ref[...] = jnp.zeros_like(acc2_ref)

    a = a_ref[...]
    acc_ref[...] += jnp.dot(a, b_ref[...], preferred_element_type=F32)
    if mode == "swiglu":
        acc2_ref[...] += jnp.dot(a, b2_ref[...], preferred_element_type=F32)

    @pl.when(k == nk - 1)
    def _fin():
        if mode == "swiglu":
            g = acc_ref[...]
            o_ref[...] = (g * (1.0 / (1.0 + jnp.exp(-g))) * acc2_ref[...]).astype(o_ref.dtype)
        elif mode == "resid":
            o_ref[...] = r_ref[...] + acc_ref[...]
        else:
            o_ref[...] = acc_ref[...].astype(o_ref.dtype)


def _epilogue(mode, acc, acc2, resid, dtype):
    if mode == "swiglu":
        return (acc * (1.0 / (1.0 + jnp.exp(-acc))) * acc2).astype(dtype)
    if mode == "resid":
        return resid + acc
    return acc.astype(dtype)


def _mm_fullk_body(*refs, mode):
    if mode == "swiglu":
        a_ref, b_ref, b2_ref, o_ref = refs
    elif mode == "resid":
        a_ref, b_ref, r_ref, o_ref = refs
    else:
        a_ref, b_ref, o_ref = refs
    a = a_ref[...]
    acc = jnp.dot(a, b_ref[...], preferred_element_type=F32)
    acc2 = jnp.dot(a, b2_ref[...], preferred_element_type=F32) if mode == "swiglu" else None
    o_ref[...] = _epilogue(mode, acc, acc2, r_ref[...] if mode == "resid" else None, o_ref.dtype)


def _matmul_fullk(a, b, *, b2=None, resid=None, out_dtype=F32, tm=1024, tn=512, name="mm"):
    M, K = a.shape
    N = b.shape[1]
    tm, tn = min(tm, M), min(tn, N)
    assert M % tm == 0 and N % tn == 0, (a.shape, b.shape, tm, tn)
    mode = "swiglu" if b2 is not None else ("resid" if resid is not None else "plain")
    in_specs = [pl.BlockSpec((tm, K), lambda i, j: (i, 0)), pl.BlockSpec((K, tn), lambda i, j: (0, j))]
    args = [a, b]
    if b2 is not None:
        in_specs.append(pl.BlockSpec((K, tn), lambda i, j: (0, j)))
        args.append(b2)
    if resid is not None:
        in_specs.append(pl.BlockSpec((tm, tn), lambda i, j: (i, j)))
        args.append(resid)
    return pl.pallas_call(
        functools.partial(_mm_fullk_body, mode=mode),
        grid=(M // tm, N // tn),
        in_specs=in_specs,
        out_specs=pl.BlockSpec((tm, tn), lambda i, j: (i, j)),
        out_shape=jax.ShapeDtypeStruct((M, N), out_dtype),
        compiler_params=_cparams(("parallel", "arbitrary")),
        name=name,
    )(*args)


def _wproj_body(*refs, mode, k1):
    refs = list(refs)
    a_ref = refs.pop(0)
    a2_ref = refs.pop(0) if k1 else None
    w_ref = refs.pop(0)
    w2_ref = refs.pop(0) if mode == "swiglu" else None
    r_ref = refs.pop(0) if mode == "resid" else None
    o_ref = refs.pop(0)
    wb_ref = refs.pop(0)
    wb2_ref = refs.pop(0) if mode == "swiglu" else None

    @pl.when(pl.program_id(1) == 0)
    def _cast():
        wb_ref[...] = w_ref[...].astype(BF16)
        if mode == "swiglu":
            wb2_ref[...] = w2_ref[...].astype(BF16)

    def product(wb):
        if k1:
            return (jnp.dot(a_ref[...], wb[0:k1, :], preferred_element_type=F32)
                    + jnp.dot(a2_ref[...], wb[k1:, :], preferred_element_type=F32))
        return jnp.dot(a_ref[...], wb[...], preferred_element_type=F32)

    acc = product(wb_ref)
    acc2 = product(wb2_ref) if mode == "swiglu" else None
    o_ref[...] = _epilogue(mode, acc, acc2, r_ref[...] if mode == "resid" else None, o_ref.dtype)


def _wproj(a, w, layer, *, a2=None, w2=None, resid=None, out_dtype=F32, tm=1024, tn=256, name="wproj"):
    M, k1 = a.shape
    K, N = w.shape[1:]
    assert K == k1 + (a2.shape[1] if a2 is not None else 0)
    tm, tn = min(tm, M), min(tn, N)
    assert M % tm == 0 and N % tn == 0, (a.shape, w.shape, tm, tn)
    mode = "swiglu" if w2 is not None else ("resid" if resid is not None else "plain")
    wspec = pl.BlockSpec((None, K, tn), lambda j, i: (layer, 0, j))
    in_specs = [pl.BlockSpec((tm, k1), lambda j, i: (i, 0))]
    args = [a]
    if a2 is not None:
        in_specs.append(pl.BlockSpec((tm, K - k1), lambda j, i: (i, 0)))
        args.append(a2)
    in_specs.append(wspec)
    args.append(w)
    scratch = [pltpu.VMEM((K, tn), BF16)]
    if w2 is not None:
        in_specs.append(wspec)
        args.append(w2)
        scratch.append(pltpu.VMEM((K, tn), BF16))
    if resid is not None:
        in_specs.append(pl.BlockSpec((tm, tn), lambda j, i: (i, j)))
        args.append(resid)
    return pl.pallas_call(
        functools.partial(_wproj_body, mode=mode, k1=k1 if a2 is not None else 0),
        grid=(N // tn, M // tm),
        in_specs=in_specs,
        out_specs=pl.BlockSpec((tm, tn), lambda j, i: (i, j)),
        out_shape=jax.ShapeDtypeStruct((M, N), out_dtype),
        scratch_shapes=scratch,
        compiler_params=_cparams(("parallel", "arbitrary")),
        name=name,
    )(*args)


def _cast_body(x_ref, o_ref, *, rows, cols, tr, tc):
    x = x_ref[...]
    r = pl.program_id(0) * tr + lax.broadcasted_iota(I32, x.shape, 0)
    c = pl.program_id(1) * tc + lax.broadcasted_iota(I32, x.shape, 1)
    o_ref[...] = jnp.where((r < rows) & (c < cols), x, 0.0).astype(o_ref.dtype)


def _cast_pad(w, layer, rows_p=None, cols_p=None, *, tr=512, tc=1024):
    _, rows, cols = w.shape
    rows_p, cols_p = rows_p or rows, cols_p or cols
    tr, tc = min(tr, rows_p), min(tc, cols_p)
    assert rows_p % tr == 0 and cols_p % tc == 0, (w.shape, rows_p, cols_p)
    return pl.pallas_call(
        functools.partial(_cast_body, rows=rows, cols=cols, tr=tr, tc=tc),
        grid=(rows_p // tr, cols_p // tc),
        in_specs=[pl.BlockSpec((None, tr, tc), lambda i, j: (layer, i, j))],
        out_specs=pl.BlockSpec((tr, tc), lambda i, j: (i, j)),
        out_shape=jax.ShapeDtypeStruct((rows_p, cols_p), BF16),
        compiler_params=_cparams(("parallel", "parallel")),
        name="cast_pad",
    )(w)


def _matmul(a, b, *, b2=None, resid=None, out_dtype=F32, tm=1024, tn=1024, tk=1024, name="mm"):
    M, K = a.shape
    N = b.shape[1]
    tm, tn, tk = min(tm, M), min(tn, N), min(tk, K)
    assert M % tm == 0 and N % tn == 0 and K % tk == 0, (a.shape, b.shape, tm, tn, tk)
    nk = K // tk
    mode = "swiglu" if b2 is not None else ("resid" if resid is not None else "plain")
    in_specs = [pl.BlockSpec((tm, tk), lambda i, j, k: (i, k)),
                pl.BlockSpec((tk, tn), lambda i, j, k: (k, j))]
    args = [a, b]
    scratch = [pltpu.VMEM((tm, tn), F32)]
    if b2 is not None:
        in_specs.append(pl.BlockSpec((tk, tn), lambda i, j, k: (k, j)))
        args.append(b2)
        scratch.append(pltpu.VMEM((tm, tn), F32))
    if resid is not None:
        in_specs.append(pl.BlockSpec((tm, tn), lambda i, j, k: (i, j)))
        args.append(resid)
    return pl.pallas_call(
        functools.partial(_mm_body, nk=nk, mode=mode),
        grid=(M // tm, N // tn, nk),
        in_specs=in_specs,
        out_specs=pl.BlockSpec((tm, tn), lambda i, j, k: (i, j)),
        out_shape=jax.ShapeDtypeStruct((M, N), out_dtype),
        scratch_shapes=scratch,
        compiler_params=_cparams(("parallel", "parallel", "arbitrary")),
        name=name,
    )(*args)


def _rms_body(x_ref, g_ref, o_ref):
    x = x_ref[...]
    ms = jnp.mean(x * x, axis=-1, keepdims=True)
    o_ref[...] = (x * lax.rsqrt(ms + EPS) * g_ref[...]).astype(o_ref.dtype)


def _rmsnorm(x, g, *, tr=256):
    M, D = x.shape
    tr = min(tr, M)
    return pl.pallas_call(
        _rms_body,
        grid=(M // tr,),
        in_specs=[pl.BlockSpec((tr, D), lambda i: (i, 0)),
                  pl.BlockSpec((1, D), lambda i: (0, 0))],
        out_specs=pl.BlockSpec((tr, D), lambda i: (i, 0)),
        out_shape=jax.ShapeDtypeStruct((M, D), BF16),
        compiler_params=_cparams(("parallel",)),
        name="rmsnorm",
    )(x, g.reshape(1, D))


def _rope_tables(T, gs):
    half = gs // 2
    inv = ROPE_THETA ** (-jnp.arange(half, dtype=F32) / half)
    ang = jnp.arange(T, dtype=jnp.int32).astype(F32)[:, None] * inv[None, :]
    cos, sin = jnp.cos(ang), jnp.sin(ang)
    cosg = jnp.concatenate([cos, cos], axis=1)
    sing = jnp.concatenate([-sin, sin], axis=1)
    reps = LANES // gs
    return jnp.tile(cosg, (1, reps)), jnp.tile(sing, (1, reps))


def _group_ones(gs):
    r = jnp.arange(LANES)
    return (r[:, None] // gs == r[None, :] // gs).astype(BF16)


def _seg_body(x_ref, g_ref, cos_ref, sin_ref, bd_ref, o_ref, *, gs, do_norm, do_rope, scale, pick):
    half = gs // 2
    lane = lax.broadcasted_iota(I32, (x_ref.shape[0], LANES), 1)
    for t in range(x_ref.shape[1] // LANES):
        sl = slice(t * LANES, (t + 1) * LANES)
        x = x_ref[:, sl]
        if pick == "dup_low":
            x = jnp.where(lane < LANES // 2, x, pltpu.roll(x, LANES // 2, 1))
        elif pick == "high16":
            x = jnp.where(lane < IDX_HEADS, pltpu.roll(x, LANES // 2, 1), 0.0)
        if do_norm:
            x2 = x * x
            hi = x2.astype(BF16)
            lo = (x2 - hi.astype(F32)).astype(BF16)
            bd = bd_ref[...]
            ssum = jnp.dot(hi, bd, preferred_element_type=F32) + jnp.dot(lo, bd, preferred_element_type=F32)
            x = x * lax.rsqrt(ssum * (1.0 / gs) + EPS) * g_ref[...]
        if do_rope:
            if gs == LANES:
                rot = pltpu.roll(x, half, 1)
            else:
                rot = jnp.where((lane & (gs - 1)) < half, pltpu.roll(x, LANES - half, 1), pltpu.roll(x, half, 1))
            x = x * cos_ref[...] + rot * sin_ref[...]
        if scale != 1.0:
            x = x * scale
        o_ref[:, sl] = x.astype(o_ref.dtype)


def _segment(z, col_off, width, gain, tables, T, *, gs, do_norm, do_rope, scale=1.0, out_dtype=BF16, pick=None,
             tr=512, max_cw=1024):
    M = z.shape[0]
    tr = min(tr, T)
    nrb = T // tr
    cw = math.gcd(math.gcd(col_off, width), max_cw)
    assert cw % LANES == 0, (col_off, width)
    cb = col_off // cw
    cos, sin = tables
    if gain is None:
        gain = jnp.ones((gs,), F32)
    gt = jnp.tile(gain.astype(F32), LANES // gs).reshape(1, LANES)
    return pl.pallas_call(
        functools.partial(_seg_body, gs=gs, do_norm=do_norm, do_rope=do_rope, scale=scale, pick=pick),
        grid=(M // tr, width // cw),
        in_specs=[pl.BlockSpec((tr, cw), lambda i, j: (i, cb + j)),
                  pl.BlockSpec((1, LANES), lambda i, j: (0, 0)),
                  pl.BlockSpec((tr, LANES), lambda i, j: (i % nrb, 0)),
                  pl.BlockSpec((tr, LANES), lambda i, j: (i % nrb, 0)),
                  pl.BlockSpec((LANES, LANES), lambda i, j: (0, 0))],
        out_specs=pl.BlockSpec((tr, cw), lambda i, j: (i, j)),
        out_shape=jax.ShapeDtypeStruct((M, width), out_dtype),
        compiler_params=_cparams(("parallel", "parallel")),
        name="segment",
    )(z, gt, cos, sin, _group_ones(gs))


def _dot_nt(a, b):
    return lax.dot_general(a, b, (((1,), (1,)), ((), ())), preferred_element_type=F32)


def _softmax_steps(ss, vt, m_ref, l_ref, acc_ref, idxs, guard):
    m_prev = [m_ref[i] for i in idxs]
    m_next = [jnp.maximum(mp, jnp.max(s, axis=0, keepdims=True)) for mp, s in zip(m_prev, ss)]
    m_use = [jnp.where(mn == -jnp.inf, 0.0, mn) for mn in m_next] if guard else m_next
    ps = [jnp.exp2(s - mu) for s, mu in zip(ss, m_use)]
    alpha = [jnp.exp2(mp - mu) for mp, mu in zip(m_prev, m_use)]
    pv = [jnp.dot(vt, p.astype(BF16), preferred_element_type=F32) for p in ps]
    for n, i in enumerate(idxs):
        l_ref[i] = alpha[n] * l_ref[i] + jnp.sum(ps[n], axis=0, keepdims=True)
        acc_ref[i] = alpha[n] * acc_ref[i] + pv[n]
        m_ref[i] = m_next[n]


def _dattn_body(lam_ref, sub_ref, q_ref, k_ref, v_ref, o_ref, m_ref, l_ref, acc_ref, *, tq, tk, lam_init):
    i = pl.program_id(2)
    q = q_ref[...]
    lane = lax.broadcasted_iota(I32, q.shape, 1)
    zero = jnp.zeros_like(q)
    qs = (jnp.where(lane < DA_HEAD_DIM, q, zero), jnp.where(lane >= DA_HEAD_DIM, q, zero))
    m_ref[...] = jnp.full(m_ref.shape, -jnp.inf, F32)
    l_ref[...] = jnp.zeros(l_ref.shape, F32)
    acc_ref[...] = jnp.zeros(acc_ref.shape, F32)
    reps = tk // LANES

    def tile_step(j, masked):
        off = pl.multiple_of(j * tk, tk)
        ks = k_ref[pl.ds(off, tk), :]
        vs = v_ref[pl.ds(off, tk), :]
        if masked:
            row = i * tq + lax.broadcasted_iota(I32, (tq, tk), 0)
            col = off + lax.broadcasted_iota(I32, (tq, tk), 1)
            vis = col <= row
        cs = range(2)
        s = [_dot_nt(qs[c], ks) for c in cs]
        if masked:
            s = [jnp.where(vis, x, -jnp.inf) for x in s]
        m_prev = [m_ref[c] for c in cs]
        m_next = [jnp.maximum(m_prev[c], jnp.max(s[c], axis=-1, keepdims=True)) for c in cs]
        p = [jnp.exp2(s[c] - jnp.concatenate([m_next[c]] * reps, axis=1)) for c in cs]
        alpha = [jnp.exp2(m_prev[c] - m_next[c]) for c in cs]
        pv = [jnp.dot(p[c].astype(BF16), vs, preferred_element_type=F32) for c in cs]
        for c in cs:
            l_ref[c] = alpha[c] * l_ref[c] + jnp.sum(p[c], axis=-1, keepdims=True)
            acc_ref[c] = alpha[c] * acc_ref[c] + pv[c]
            m_ref[c] = m_next[c]

    n_full = (i * tq + 1) // tk
    n_all = (i * tq + tq + tk - 1) // tk

    def full_body(j, carry):
        tile_step(j, False)
        return carry

    def diag_body(j, carry):
        tile_step(j, True)
        return carry

    lax.fori_loop(0, n_full, full_body, 0)
    lax.fori_loop(n_full, n_all, diag_body, 0)

    lp = lam_ref[...]
    lam = (jnp.exp(jnp.sum(lp[0:1] * lp[1:2], axis=-1, keepdims=True))
           - jnp.exp(jnp.sum(lp[2:3] * lp[3:4], axis=-1, keepdims=True)) + lam_init)
    o = acc_ref[0] * (1.0 / l_ref[0]) - lam * (acc_ref[1] * (1.0 / l_ref[1]))
    ms = jnp.mean(o * o, axis=-1, keepdims=True)
    o = o * lax.rsqrt(ms + EPS) * sub_ref[...] * (1.0 - lam_init)
    o_ref[...] = o.astype(o_ref.dtype)


def _diff_attention(qh, kh, vh, lam_p, subln, lam_init, B, T, *, tq=512, tk=512):
    M = qh.shape[0]
    tq, tk = min(tq, T), min(tk, T)
    nq = T // tq
    return pl.pallas_call(
        functools.partial(_dattn_body, tq=tq, tk=tk, lam_init=lam_init),
        grid=(B, DA_HEADS, nq),
        in_specs=[pl.BlockSpec((4, DA_HEAD_DIM), lambda b, h, i: (0, 0)),
                  pl.BlockSpec((1, LANES), lambda b, h, i: (0, 0)),
                  pl.BlockSpec((tq, LANES), lambda b, h, i: (b * nq + i, h)),
                  pl.BlockSpec((T, LANES), lambda b, h, i: (b, h)),
                  pl.BlockSpec((T, LANES), lambda b, h, i: (b, h))],
        out_specs=pl.BlockSpec((tq, LANES), lambda b, h, i: (b * nq + i, h)),
        out_shape=jax.ShapeDtypeStruct((M, DA_WIDTH), BF16),
        scratch_shapes=[pltpu.VMEM((2, tq, LANES), F32)] * 3,
        compiler_params=_cparams(("parallel", "parallel", "arbitrary")),
        name="diff_attention",
    )(lam_p.astype(F32), subln.astype(F32).reshape(1, LANES), qh, kh, vh)


def _conv_body(gb_ref, gc_ref, u_ref, w_ref, o_ref, sh_ref, *, T):
    cu = gc_ref[...] * u_ref[...]
    sh_ref[0:SUBLANES, :] = jnp.zeros((SUBLANES, LANES), F32)
    sh_ref[SUBLANES:SUBLANES + T, :] = cu
    w = w_ref[...]
    conv = (sh_ref[SUBLANES - 2:SUBLANES - 2 + T, :] * w[0:1]
            + sh_ref[SUBLANES - 1:SUBLANES - 1 + T, :] * w[1:2]
            + cu * w[2:3])
    o_ref[...] = (gb_ref[...] * conv).astype(o_ref.dtype)


def _short_conv(z, conv_w, B, T):
    M = z.shape[0]
    nc = SC_WIDTH // LANES
    base = 3 * DA_WIDTH // LANES
    return pl.pallas_call(
        functools.partial(_conv_body, T=T),
        grid=(B, nc),
        in_specs=[pl.BlockSpec((T, LANES), lambda b, c: (b, base + c)),
                  pl.BlockSpec((T, LANES), lambda b, c: (b, base + nc + c)),
                  pl.BlockSpec((T, LANES), lambda b, c: (b, base + 2 * nc + c)),
                  pl.BlockSpec((CONV_W, LANES), lambda b, c: (0, c))],
        out_specs=pl.BlockSpec((T, LANES), lambda b, c: (b, c)),
        out_shape=jax.ShapeDtypeStruct((M, SC_WIDTH), BF16),
        scratch_shapes=[pltpu.VMEM((T + SUBLANES, LANES), F32)],
        compiler_params=_cparams(("parallel", "parallel")),
        name="short_conv",
    )(z, z, z, conv_w.astype(F32))


def _sigmoid(x):
    return 1.0 / (1.0 + jnp.exp(-x))


def _rwprep_body(z_ref, zh_ref, vf_ref, mu_ref, w0_ref, a0_ref, v0_ref, w2_ref, a2_ref, v2_ref, g2_ref,
                 r_o, ld_o, k_o, v_o, a_o, g_o, sh_ref, *, tr, nrb):
    i = pl.program_id(0)
    first = (i % nrb) == 0
    sh_ref[SUBLANES - 1:SUBLANES, :] = jnp.where(first, 0.0, zh_ref[SUBLANES - 1:SUBLANES, :])
    sh_ref[SUBLANES:SUBLANES + tr, :] = z_ref[...]

    def shifted(lo, hi):
        zc = z_ref[:, lo:hi]
        zp = sh_ref[SUBLANES - 1:SUBLANES - 1 + tr, lo:hi]
        return zc + (zp - zc) * mu_ref[:, lo:hi]

    r_o[...] = shifted(OD_R, OD_K)
    k_o[...] = shifted(OD_K, OD_V)
    lora = shifted(OD_LORA, OD_RW_END)
    lora_b = lora.astype(BF16)
    lw = w0_ref[...] + jnp.dot(jnp.tanh(lora).astype(BF16), w2_ref[...], preferred_element_type=F32)
    nlw = -lw
    softplus = jnp.maximum(nlw, 0.0) + jnp.log(1.0 + jnp.exp(-jnp.abs(nlw)))
    ld_o[...] = -jnp.exp(-softplus - 0.5)
    a_o[...] = _sigmoid(a0_ref[...] + jnp.dot(lora_b, a2_ref[...], preferred_element_type=F32))
    v = shifted(OD_V, OD_LORA)
    vg = _sigmoid(v0_ref[...] + jnp.dot(lora_b, v2_ref[...], preferred_element_type=F32))
    v_o[...] = v + (vf_ref[...] - v) * vg
    g_o[...] = jnp.dot(_sigmoid(lora).astype(BF16), g2_ref[...], preferred_element_type=F32)


def _place_rows(w, start):
    return jnp.pad(w, ((start, LORA_W - start - w.shape[0]), (0, 0))).astype(BF16)


def _rwkv_prep(z, z_first, mu_p, w0, a0, v0, w2, a2, v2, g2, T, *, tr=128):
    M = z.shape[0]
    tr = min(tr, T)
    nrb = T // tr
    W = OD_RW_END
    hb = tr // SUBLANES
    row = lambda a: a.astype(F32).reshape(1, RW_WIDTH)
    full = lambda shape: pl.BlockSpec(shape, lambda i: (0, 0))
    out = jax.ShapeDtypeStruct((M, RW_WIDTH), F32)
    ospec = pl.BlockSpec((tr, RW_WIDTH), lambda i: (i, 0))
    return pl.pallas_call(
        functools.partial(_rwprep_body, tr=tr, nrb=nrb),
        grid=(M // tr,),
        in_specs=[pl.BlockSpec((tr, W), lambda i: (i, 0)),
                  pl.BlockSpec((SUBLANES, W), lambda i: (jnp.maximum(i * hb - 1, 0), 0)),
                  pl.BlockSpec((tr, RW_WIDTH), lambda i: (i, 2 * DA_WIDTH // RW_WIDTH)),
                  full((1, W)), full((1, RW_WIDTH)), full((1, RW_WIDTH)), full((1, RW_WIDTH)),
                  full((LORA_W, RW_WIDTH)), full((LORA_W, RW_WIDTH)), full((LORA_W, RW_WIDTH)),
                  full((LORA_W, RW_WIDTH))],
        out_specs=[ospec] * 6,
        out_shape=[out] * 6,
        scratch_shapes=[pltpu.VMEM((tr + SUBLANES, W), F32)],
        compiler_params=_cparams(("parallel",)),
        name="rwkv_prep",
    )(z, z, z_first, mu_p, row(w0), row(a0), row(v0),
      _place_rows(w2, 0), _place_rows(a2, W_LORA), _place_rows(v2, W_LORA + A_LORA),
      _place_rows(g2, W_LORA + A_LORA + V_LORA))


def _split3(x):
    hi = x.astype(BF16)
    r1 = x - hi.astype(F32)
    mid = r1.astype(BF16)
    lo = (r1 - mid.astype(F32)).astype(BF16)
    return hi, mid, lo


def _mm(a, b):
    return jnp.dot(a.astype(BF16), b.astype(BF16), preferred_element_type=F32)


def _mm_tn(a, b):
    return jnp.dot(a.T.astype(BF16), b.astype(BF16), preferred_element_type=F32)


def _rwkv_body(r_ref, ld_ref, k_ref, v_ref, a_ref, g_ref, kk_ref, ka_ref, rk_ref, lw_ref, lb_ref,
               o_ref, st_ref, *, ng, L):
    c = pl.program_id(2)

    @pl.when(c == 0)
    def _init():
        st_ref[...] = jnp.zeros_like(st_ref)

    N = RW_HEAD_DIM
    S = RW_GROUP * L
    ri = lax.broadcasted_iota(I32, (S, S), 0)
    ci = lax.broadcasted_iota(I32, (S, S), 1)
    same = (ri // L) == (ci // L)
    incl = same & (ci <= ri)
    strict = same & (ci < ri)
    eye_s = (ci == ri).astype(F32)
    blk16 = (ri // 16) == (ci // 16)
    hmask = (lax.broadcasted_iota(I32, (S, RW_GW), 0) // L) == (lax.broadcasted_iota(I32, (S, RW_GW), 1) // N)
    rl = lax.broadcasted_iota(I32, (L, L), 0)
    cl = lax.broadcasted_iota(I32, (L, L), 1)
    tri = (cl <= rl).astype(BF16)
    rn = lax.broadcasted_iota(I32, (RW_GW, RW_GW), 0)
    cn = lax.broadcasted_iota(I32, (RW_GW, RW_GW), 1)
    eye_g = rn == cn
    ones_g = ((rn // N) == (cn // N)).astype(BF16)

    def gsum(x):
        hi = x.astype(BF16)
        lo = (x - hi.astype(F32)).astype(BF16)
        return jnp.dot(hi, ones_g, preferred_element_type=F32) + jnp.dot(lo, ones_g, preferred_element_type=F32)

    def tile(x):
        return jnp.concatenate([x] * RW_GROUP, axis=0)

    def stack(x):
        return jnp.where(hmask, tile(x), 0.0)

    G = range(ng)
    sls = [slice(gi * RW_GW, (gi + 1) * RW_GW) for gi in G]
    each = lambda f, *lists: [f(*xs) for xs in zip(*lists)]
    r = [r_ref[:, sl] for sl in sls]
    ld = [ld_ref[:, sl] for sl in sls]
    k = [k_ref[:, sl] for sl in sls]
    v = [v_ref[:, sl] for sl in sls]
    a = [a_ref[:, sl] for sl in sls]
    kk = [k[gi] * kk_ref[:, sls[gi]] for gi in G]
    kk = each(lambda x: x / jnp.maximum(jnp.sqrt(gsum(x * x)), 1e-12), kk)
    k2 = [k[gi] * (1.0 + (a[gi] - 1.0) * ka_ref[:, sls[gi]]) for gi in G]
    bv = each(lambda x, y: x * y, kk, a)
    parts = each(_split3, ld)
    cum = each(lambda p: (jnp.dot(tri, p[0], preferred_element_type=F32) + jnp.dot(tri, p[1], preferred_element_type=F32)
                          + jnp.dot(tri, p[2], preferred_element_type=F32)), parts)
    clast = each(lambda c_: c_[L - 1:L, :], cum)
    e_neg = each(lambda c_: jnp.exp(-c_), cum)
    e_l = each(lambda cl_, c_: jnp.exp(cl_ - c_), clast, cum)
    p_l = each(jnp.exp, clast)
    at = each(lambda x, c_, l_: stack(-x * jnp.exp(c_ - l_)), kk, cum, ld)
    rt = each(lambda x, c_: stack(x * jnp.exp(c_)), r, cum)
    vs = each(stack, v)
    bh = each(lambda x, e: stack(x * e), bv, e_l)
    kh = each(lambda x, e: stack(x * e), k2, e_l)
    btb = each(lambda x, e: tile(x * e).astype(BF16), bv, e_neg)
    ktb = each(lambda x, e: tile(x * e).astype(BF16), k2, e_neg)
    atb = each(lambda x: x.astype(BF16), at)
    rtb = each(lambda x: x.astype(BF16), rt)
    mab = each(lambda x, y: jnp.where(strict, _dot_nt(x, y), 0.0), atb, btb)
    mak = each(lambda x, y: jnp.where(strict, _dot_nt(x, y), 0.0), atb, ktb)
    mrb = each(lambda x, y: jnp.where(incl, _dot_nt(x, y), 0.0), rtb, btb)
    mrk = each(lambda x, y: jnp.where(incl, _dot_nt(x, y), 0.0), rtb, ktb)
    nd = each(lambda m: jnp.where(blk16, m, 0.0), mab)
    n2 = each(_mm, nd, nd)
    mv = each(_mm, mak, vs)
    n4 = each(_mm, n2, n2)
    t = each(lambda n, n2_: eye_s + n + _mm(eye_s + n, n2_), nd, n2)
    n8 = each(_mm, n4, n4)
    t = each(lambda t_, n: t_ + _mm(t_, n), t, n4)
    t = each(lambda t_, n: t_ + _mm(t_, n), t, n8)
    size = 16
    while size < L:
        off = ((ri // size) == (ci // size) + 1) & ((ri // (2 * size)) == (ci // (2 * size)))
        u = each(lambda t_, m: _mm(t_, jnp.where(off, m, 0.0)), t, mab)
        t = each(lambda t_, u_: t_ + _mm(u_, t_), t, u)
        size *= 2
    wm = each(_mm, t, at)
    ul = each(_mm, t, mv)
    qe = each(lambda x, m, w: x + _mm(m, w), rt, mrb, wm)
    yl = each(lambda m, u_, m2, x: _mm(m, u_) + _mm(m2, x), mrb, ul, mrk, vs)
    gm = each(lambda b, w, p: _mm_tn(b, w) + jnp.where(eye_g, p, 0.0), bh, wm, p_l)
    hm = each(lambda b, u_, k_, x: _mm_tn(b, u_) + _mm_tn(k_, x), bh, ul, kh, vs)
    st = [st_ref[gi] for gi in G]
    ys = each(lambda q_, s_, y_: _mm(q_, s_) + y_, qe, st, yl)
    st_new = each(lambda g_, s_, h_: _mm(g_, s_) + h_, gm, st, hm)
    for gi in G:
        st_ref[gi] = st_new[gi]
        sl = sls[gi]
        y = ys[gi][0:L]
        for hh in range(1, RW_GROUP):
            y = y + ys[gi][hh * L:(hh + 1) * L]
        mean = gsum(y) * (1.0 / N)
        d = y - mean
        var = gsum(d * d) * (1.0 / N)
        yn = d * lax.rsqrt(var + LNX_EPS) * lw_ref[:, sl] + lb_ref[:, sl]
        yn = yn + gsum(r[gi] * k2[gi] * rk_ref[:, sl]) * v[gi]
        o_ref[:, sl] = (yn * g_ref[:, sl]).astype(o_ref.dtype)


def _rwkv(r, ld, k, v, a, g, k_k, k_a, r_k, lnx_w, lnx_b, B, T, *, ng=4, L=64):
    M = r.shape[0]
    L = min(L, T)
    nc = T // L
    W = ng * RW_GW
    blk = pl.BlockSpec((L, W), lambda b, hg, c: (b * nc + c, hg))
    par = pl.BlockSpec((1, W), lambda b, hg, c: (0, hg))
    row = lambda p: p.astype(F32).reshape(1, RW_WIDTH)
    return pl.pallas_call(
        functools.partial(_rwkv_body, ng=ng, L=L),
        grid=(B, RW_WIDTH // W, nc),
        in_specs=[blk] * 6 + [par] * 5,
        out_specs=blk,
        out_shape=jax.ShapeDtypeStruct((M, RW_WIDTH), BF16),
        scratch_shapes=[pltpu.VMEM((ng, RW_GW, RW_GW), F32)],
        compiler_params=_cparams(("parallel", "parallel", "arbitrary")),
        name="rwkv7_chunk",
    )(r, ld, k, v, a, g, row(k_k), row(k_a), row(r_k), row(lnx_w), row(lnx_b))


def _dsa_body(qd_ref, qi_ref, wi_ref, kd_ref, vd_ref, ki_ref, o_ref,
              keys_ref, bias_ref, qim_ref, cut_ref, m_ref, l_ref, acc_ref, *, tq, tk, ksel, T, hg):
    i = pl.program_id(1)
    nkt = (i * tq + tq + tk - 1) // tk
    krow = lax.broadcasted_iota(I32, (tk, tq), 0)
    qpos = i * tq + lax.broadcasted_iota(I32, (tk, tq), 1)
    low_half = lax.broadcasted_iota(I32, (tq, LANES), 1) < IDX_DIM
    for h in range(IDX_HEADS):
        qt = qi_ref[:, (h // 2) * LANES:(h // 2 + 1) * LANES]
        qim_ref[h] = jnp.where(low_half if h % 2 == 0 else jnp.logical_not(low_half), qt,
                               jnp.zeros_like(qt)).astype(qim_ref.dtype)
    wit = wi_ref[...].T

    def score_tile(j, carry):
        off = pl.multiple_of(j * tk, tk)
        kt = ki_ref[pl.ds(off, tk), :]
        acc = jnp.zeros((tk, tq), F32)
        for h in range(IDX_HEADS):
            acc = acc + jnp.maximum(_dot_nt(kt, qim_ref[h]), 0.0) * wit[h:h + 1, :]
        acc = acc + 0.0
        sc = jnp.where((krow + off) <= qpos, acc, -jnp.inf)
        bits = pltpu.bitcast(sc, I32)
        keys_ref[pl.ds(off, tk), :] = bits ^ ((bits >> 31) & 0x7FFFFFFF)
        return carry

    lax.fori_loop(0, nkt, score_tile, 0)

    def count(pred):
        def body(j, c):
            off = pl.multiple_of(j * tk, tk)
            hit = jnp.where(pred(keys_ref[pl.ds(off, tk), :], krow + off), 1.0, 0.0)
            return c + jnp.sum(hit.reshape(tk // SUBLANES, SUBLANES, tq), axis=0)
        c = lax.fori_loop(0, nkt, body, jnp.zeros((SUBLANES, tq), F32))
        return jnp.sum(c, axis=0, keepdims=True)

    def bit_step(b, thr):
        cand = thr + jnp.left_shift(jnp.int32(1), 31 - b)
        cnt = count(lambda key, _: key >= cand)
        return jnp.where(cnt >= ksel, cand, thr)

    thr = lax.fori_loop(0, 32, bit_step, jnp.full((1, tq), INT_MIN, I32))

    n_gt = count(lambda key, _: key > thr)
    n_ge = count(lambda key, _: key >= thr)
    need = (n_ge > ksel) & (thr > NEG_INF_KEY)
    quota = ksel - n_gt
    cut_ref[...] = jnp.full((1, tq), T, I32)

    @pl.when(jnp.max(jnp.where(need, 1.0, 0.0)) > 0.0)
    def _ties():
        def pos_step(b, p):
            cand = p + jnp.left_shift(jnp.int32(1), (T.bit_length() - 1) - b)
            cnt = count(lambda key, pos: (key == thr) & (pos < cand))
            return jnp.where(cnt < quota, cand, p)
        p = lax.fori_loop(0, T.bit_length(), pos_step, jnp.zeros((1, tq), I32))
        cut_ref[...] = jnp.where(need, p, T)

    cut = cut_ref[...]
    thr_sel = jnp.maximum(thr, NEG_INF_KEY + 1)

    def bias_tile(j, carry):
        off = pl.multiple_of(j * tk, tk)
        key = keys_ref[pl.ds(off, tk), :]
        sel = (key > thr_sel) | ((key == thr_sel) & ((krow + off) <= cut))
        bias_ref[pl.ds(off, tk), :] = jnp.where(sel, 0.0, -jnp.inf)
        return carry

    lax.fori_loop(0, nkt, bias_tile, 0)

    m_ref[...] = jnp.full(m_ref.shape, -jnp.inf, F32)
    l_ref[...] = jnp.zeros(l_ref.shape, F32)
    acc_ref[...] = jnp.zeros(acc_ref.shape, F32)

    def attend(j, carry):
        off = pl.multiple_of(j * tk, tk)
        ks = kd_ref[pl.ds(off, tk), :]
        vt = vd_ref[pl.ds(off, tk), :].T
        bias = bias_ref[pl.ds(off, tk), :]
        for h0 in range(0, SA_HEADS, hg):
            heads = list(range(h0, h0 + hg))
            ss = [_dot_nt(ks, qd_ref[:, h * LANES:(h + 1) * LANES]) + bias for h in heads]
            _softmax_steps(ss, vt, m_ref, l_ref, acc_ref, heads, guard=True)
        return carry

    lax.fori_loop(0, nkt, attend, 0)
    for h in range(SA_HEADS):
        o_ref[:, h * LANES:(h + 1) * LANES] = (acc_ref[h] * (1.0 / l_ref[h])).T.astype(o_ref.dtype)


def _dsa(qd, qi, wi, kd, vd, ki, B, T, ksel, *, tq=256, tk=512, hg=8):
    M = qd.shape[0]
    tq, tk = min(tq, T), min(tk, T)
    nq = T // tq
    qblk = lambda w: pl.BlockSpec((tq, w), lambda b, i: (b * nq + i, 0))
    kblk = pl.BlockSpec((T, LANES), lambda b, i: (b, 0))
    return pl.pallas_call(
        functools.partial(_dsa_body, tq=tq, tk=tk, ksel=ksel, T=T, hg=hg),
        grid=(B, nq),
        in_specs=[qblk(SA_WIDTH), qblk(IDX_HEADS * IDX_DIM), qblk(LANES), kblk, kblk, kblk],
        out_specs=qblk(SA_WIDTH),
        out_shape=jax.ShapeDtypeStruct((M, SA_WIDTH), BF16),
        scratch_shapes=[pltpu.VMEM((T, tq), I32), pltpu.VMEM((T, tq), F32),
                        pltpu.VMEM((IDX_HEADS, tq, LANES), BF16), pltpu.VMEM((1, tq), I32),
                        pltpu.VMEM((SA_HEADS, 1, tq), F32), pltpu.VMEM((SA_HEADS, 1, tq), F32),
                        pltpu.VMEM((SA_HEADS, LANES, tq), F32)],
        compiler_params=_cparams(("parallel", "arbitrary")),
        name="dsa_attention",
    )(qd, qi, wi, kd, vd, ki)


def _even_mixer(xf, h, w_in, w_out, e, q_norm, k_norm, lam_p, subln, conv_w, tabs64, lam_init, B, T):
    z = _wproj(h, w_in, e, name="even_in")
    qh = _segment(z, 0, DA_WIDTH, q_norm, tabs64, T, gs=64, do_norm=True, do_rope=True,
                  scale=DA_HEAD_DIM ** -0.5 * LOG2E)
    kh = _segment(z, DA_WIDTH, DA_WIDTH, k_norm, tabs64, T, gs=64, do_norm=True, do_rope=True)
    vh = _segment(z, 2 * DA_WIDTH, DA_WIDTH, None, tabs64, T, gs=64, do_norm=False, do_rope=False)
    o = _diff_attention(qh, kh, vh, lam_p, subln, lam_init, B, T)
    y = _short_conv(z, conv_w, B, T)
    return _wproj(o, w_out, e, a2=y, resid=xf, name="even_out"), z


def _odd_mixer(xf, h, w_in_p, w_out, o, mu_p, w0, w2, a0, a2, v0, v2, g2, k_k, k_a, r_k, lnx_w, lnx_b,
               q_norm, k_norm, idxk_norm, z_first, tabs64, tabs128, B, T, ksel):
    z = _matmul_fullk(h, w_in_p, name="odd_in")
    r, ld, k, v, a, g = _rwkv_prep(z, z_first, mu_p, w0, a0, v0, w2, a2, v2, g2, T)
    rw_out = _rwkv(r, ld, k, v, a, g, k_k, k_a, r_k.reshape(-1), lnx_w, lnx_b, B, T)
    qd = _segment(z, OD_Q, SA_WIDTH, q_norm, tabs128, T, gs=128, do_norm=True, do_rope=True,
                  scale=SA_HEAD_DIM ** -0.5 * LOG2E)
    kd = _segment(z, OD_KD, LANES, k_norm, tabs128, T, gs=128, do_norm=True, do_rope=True)
    vd = _segment(z, OD_VDD, LANES, None, tabs128, T, gs=128, do_norm=False, do_rope=False)
    qi = _segment(z, OD_QI, IDX_HEADS * IDX_DIM, None, tabs64, T, gs=64, do_norm=False, do_rope=True)
    ki = _segment(z, OD_KI, LANES, idxk_norm, tabs64, T, gs=64, do_norm=True, do_rope=True, pick="dup_low")
    wi = _segment(z, OD_KI, LANES, None, tabs64, T, gs=64, do_norm=False, do_rope=False,
                  scale=IDX_HEADS ** -0.5 * IDX_DIM ** -0.5, out_dtype=F32, pick="high16")
    sa_out = _dsa(qd, qi, wi, kd, vd, ki, B, T, ksel)
    return _wproj(rw_out, w_out, o, a2=sa_out, resid=xf, name="odd_out")


def kernel(x, mix_norm, ffn_norm, ffn_gate, ffn_up, ffn_down, ev_w_in, ev_w_out, da_q_norm, da_k_norm, da_lambda, da_subln, sc_conv, od_w_in, od_w_out, rw_mu, rw_w0, rw_w2, rw_a0, rw_a2, rw_v0, rw_v2, rw_g2, rw_k_k, rw_k_a, rw_r_k, rw_lnx_w, rw_lnx_b, sa_q_norm, sa_k_norm, idx_k_norm):
    B, T, D = x.shape
    M = B * T
    ksel = min(TOPK_MAX, T // 4)
    xf = x.reshape(M, D)
    tabs64 = _rope_tables(T, 64)
    tabs128 = _rope_tables(T, 128)
    z_first = None
    for i in range(DEPTH):
        h = _rmsnorm(xf, mix_norm[i])
        if i % 2 == 0:
            e = i // 2
            lam_init = 0.8 - 0.6 * math.exp(-0.3 * i)
            xf, z = _even_mixer(xf, h, ev_w_in, ev_w_out, e, da_q_norm[e],
                                da_k_norm[e], da_lambda[e], da_subln[e], sc_conv[e], tabs64, lam_init, B, T)
            if z_first is None:
                z_first = z
        else:
            o = i // 2
            xf = _odd_mixer(xf, h, _cast_pad(od_w_in, o, cols_p=OD_PAD), od_w_out, o,
                            rw_mu[o].astype(F32).reshape(1, OD_RW_END),
                            rw_w0[o], rw_w2[o], rw_a0[o], rw_a2[o], rw_v0[o], rw_v2[o], rw_g2[o],
                            rw_k_k[o], rw_k_a[o], rw_r_k[o], rw_lnx_w[o], rw_lnx_b[o],
                            sa_q_norm[o], sa_k_norm[o], idx_k_norm[o], z_first, tabs64, tabs128, B, T, ksel)
        h = _rmsnorm(xf, ffn_norm[i])
        hid = _wproj(h, ffn_gate, i, w2=ffn_up, out_dtype=BF16, name="ffn_in")
        wd = _cast_pad(ffn_down, i, tr=256)
        xf = _matmul_fullk(hid, wd, resid=xf, tm=512, tn=256, name="ffn_out")
    return xf.reshape(B, T, D)
```

```python
import functools
import math

import jax
import jax.numpy as jnp
from jax import lax
from jax.experimental import pallas as pl
from jax.experimental.pallas import tpu as pltpu

F32 = jnp.float32
BF16 = jnp.bfloat16
I32 = jnp.int32

D_MODEL = 4096
DEPTH = 4
DA_WIDTH = 2048
DA_HEADS = 16
DA_HEAD_DIM = 64
SC_WIDTH = 2048
CONV_W = 3
RW_WIDTH = 2048
RW_HEAD_DIM = 64
RW_HEADS = 32
RW_GROUP = 4
RW_GW = RW_GROUP * RW_HEAD_DIM
W_LORA, A_LORA, V_LORA, G_LORA = 96, 96, 64, 256
LNX_EPS = 64e-5
SA_WIDTH = 2048
SA_HEAD_DIM = 128
SA_HEADS = 16
IDX_HEADS = 16
IDX_DIM = 64
TOPK_MAX = 256
FFN_HIDDEN = 11008
ROPE_THETA = 10000.0
EPS = 1e-6

LANES = 128
SUBLANES = 8
V7X_VMEM_BYTES = 64 * 1024 * 1024
VMEM_LIMIT = (V7X_VMEM_BYTES * 3) // 4

FFN_PAD = 11264
OD_R, OD_K, OD_V = 0, 2048, 4096
OD_LORA = 3 * RW_WIDTH
LORA_W = W_LORA + A_LORA + V_LORA + G_LORA
OD_RW_END = OD_LORA + LORA_W
OD_Q = OD_RW_END
OD_KD = OD_Q + SA_WIDTH
OD_VDD = OD_KD + SA_HEAD_DIM
OD_QI = OD_VDD + SA_HEAD_DIM
OD_KI = OD_QI + IDX_HEADS * IDX_DIM
OD_PAD = 10240
LOG2E = math.log2(math.e)
INT_MIN = -2 ** 31
NEG_INF_KEY = -2139095041


def _cparams(sem):
    return pltpu.CompilerParams(dimension_semantics=sem, vmem_limit_bytes=VMEM_LIMIT)


def _mm_body(*refs, nk, mode):
    if mode == "swiglu":
        a_ref, b_ref, b2_ref, o_ref, acc_ref, acc2_ref = refs
    elif mode == "resid":
        a_ref, b_ref, r_ref, o_ref, acc_ref = refs
    else:
        a_ref, b_ref, o_ref, acc_ref = refs
    k = pl.program_id(2)

    @pl.when(k == 0)
    def _init():
        acc_ref[...] = jnp.zeros_like(acc_ref)
        if mode == "swiglu":
            acc2_ref[...] = jnp.zeros_like(acc2_ref)

    a = a_ref[...]
    acc_ref[...] += jnp.dot(a, b_ref[...], preferred_element_type=F32)
    if mode == "swiglu":
        acc2_ref[...] += jnp.dot(a, b2_ref[...], preferred_element_type=F32)

    @pl.when(k == nk - 1)
    def _fin():
        if mode == "swiglu":
            g = acc_ref[...]
            o_ref[...] = (g * (1.0 / (1.0 + jnp.exp(-g))) * acc2_ref[...]).astype(o_ref.dtype)
        elif mode == "resid":
            o_ref[...] = r_ref[...] + acc_ref[...]
        else:
            o_ref[...] = acc_ref[...].astype(o_ref.dtype)


def _epilogue(mode, acc, acc2, resid, dtype):
    if mode == "swiglu":
        return (acc * (1.0 / (1.0 + jnp.exp(-acc))) * acc2).astype(dtype)
    if mode == "resid":
        return resid + acc
    return acc.astype(dtype)


def _mm_fullk_body(*refs, mode, k1, bt):
    refs = list(refs)
    a_ref = refs.pop(0)
    a2_ref = refs.pop(0) if k1 else None
    b_ref = refs.pop(0)
    b2_ref = refs.pop(0) if mode == "swiglu" else None
    r_ref = refs.pop(0) if mode == "resid" else None
    o_ref = refs.pop(0)

    def product(w_ref):
        if bt:
            return _dot_nt(a_ref[...], w_ref[...])
        if k1:
            return (jnp.dot(a_ref[...], w_ref[0:k1, :], preferred_element_type=F32)
                    + jnp.dot(a2_ref[...], w_ref[k1:, :], preferred_element_type=F32))
        return jnp.dot(a_ref[...], w_ref[...], preferred_element_type=F32)

    acc = product(b_ref)
    acc2 = product(b2_ref) if mode == "swiglu" else None
    o_ref[...] = _epilogue(mode, acc, acc2, r_ref[...] if mode == "resid" else None, o_ref.dtype)


def _matmul_fullk(a, b, *, a2=None, b2=None, resid=None, b_transposed=False, out_dtype=F32, tm=1024, tn=512,
                  name="mm"):
    M, k1 = a.shape
    K = k1 + (a2.shape[1] if a2 is not None else 0)
    N = b.shape[0] if b_transposed else b.shape[1]
    assert (b.shape[1] if b_transposed else b.shape[0]) == K and not (b_transposed and a2 is not None)
    tm, tn = min(tm, M), min(tn, N)
    assert M % tm == 0 and N % tn == 0, (a.shape, b.shape, tm, tn)
    mode = "swiglu" if b2 is not None else ("resid" if resid is not None else "plain")
    bspec = pl.BlockSpec((tn, K), lambda i, j: (j, 0)) if b_transposed else pl.BlockSpec((K, tn), lambda i, j: (0, j))
    in_specs = [pl.BlockSpec((tm, k1), lambda i, j: (i, 0))]
    args = [a]
    if a2 is not None:
        in_specs.append(pl.BlockSpec((tm, K - k1), lambda i, j: (i, 0)))
        args.append(a2)
    in_specs.append(bspec)
    args.append(b)
    if b2 is not None:
        in_specs.append(bspec)
        args.append(b2)
    if resid is not None:
        in_specs.append(pl.BlockSpec((tm, tn), lambda i, j: (i, j)))
        args.append(resid)
    return pl.pallas_call(
        functools.partial(_mm_fullk_body, mode=mode, k1=k1 if a2 is not None else 0, bt=b_transposed),
        grid=(M // tm, N // tn),
        in_specs=in_specs,
        out_specs=pl.BlockSpec((tm, tn), lambda i, j: (i, j)),
        out_shape=jax.ShapeDtypeStruct((M, N), out_dtype),
        compiler_params=_cparams(("parallel", "arbitrary")),
        name=name,
    )(*args)


def _wproj_body(*refs, mode, k1):
    refs = list(refs)
    a_ref = refs.pop(0)
    a2_ref = refs.pop(0) if k1 else None
    w_ref = refs.pop(0)
    w2_ref = refs.pop(0) if mode == "swiglu" else None
    r_ref = refs.pop(0) if mode == "resid" else None
    o_ref = refs.pop(0)
    wb_ref = refs.pop(0)
    wb2_ref = refs.pop(0) if mode == "swiglu" else None

    @pl.when(pl.program_id(1) == 0)
    def _cast():
        wb_ref[...] = w_ref[...].astype(BF16)
        if mode == "swiglu":
            wb2_ref[...] = w2_ref[...].astype(BF16)

    def product(wb):
        if k1:
            return (jnp.dot(a_ref[...], wb[0:k1, :], preferred_element_type=F32)
                    + jnp.dot(a2_ref[...], wb[k1:, :], preferred_element_type=F32))
        return jnp.dot(a_ref[...], wb[...], preferred_element_type=F32)

    acc = product(wb_ref)
    acc2 = product(wb2_ref) if mode == "swiglu" else None
    o_ref[...] = _epilogue(mode, acc, acc2, r_ref[...] if mode == "resid" else None, o_ref.dtype)


def _wproj(a, w, layer, *, a2=None, w2=None, resid=None, out_dtype=F32, tm=1024, tn=256, name="wproj"):
    M, k1 = a.shape
    K, N = w.shape[1:]
    assert K == k1 + (a2.shape[1] if a2 is not None else 0)
    tm, tn = min(tm, M), min(tn, N)
    assert M % tm == 0 and N % tn == 0, (a.shape, w.shape, tm, tn)
    mode = "swiglu" if w2 is not None else ("resid" if resid is not None else "plain")
    wspec = pl.BlockSpec((None, K, tn), lambda j, i: (layer, 0, j))
    in_specs = [pl.BlockSpec((tm, k1), lambda j, i: (i, 0))]
    args = [a]
    if a2 is not None:
        in_specs.append(pl.BlockSpec((tm, K - k1), lambda j, i: (i, 0)))
        args.append(a2)
    in_specs.append(wspec)
    args.append(w)
    scratch = [pltpu.VMEM((K, tn), BF16)]
    if w2 is not None:
        in_specs.append(wspec)
        args.append(w2)
        scratch.append(pltpu.VMEM((K, tn), BF16))
    if resid is not None:
        in_specs.append(pl.BlockSpec((tm, tn), lambda j, i: (i, j)))
        args.append(resid)
    return pl.pallas_call(
        functools.partial(_wproj_body, mode=mode, k1=k1 if a2 is not None else 0),
        grid=(N // tn, M // tm),
        in_specs=in_specs,
        out_specs=pl.BlockSpec((tm, tn), lambda j, i: (i, j)),
        out_shape=jax.ShapeDtypeStruct((M, N), out_dtype),
        scratch_shapes=scratch,
        compiler_params=_cparams(("parallel", "arbitrary")),
        name=name,
    )(*args)


def _cast_body(x_ref, o_ref, *, rows, cols, tr, tc):
    x = x_ref[...]
    r = pl.program_id(0) * tr + lax.broadcasted_iota(I32, x.shape, 0)
    c = pl.program_id(1) * tc + lax.broadcasted_iota(I32, x.shape, 1)
    o_ref[...] = jnp.where((r < rows) & (c < cols), x, 0.0).astype(o_ref.dtype)


def _cast_pad(w, layer, rows_p=None, cols_p=None, *, tr=512, tc=1024):
    _, rows, cols = w.shape
    rows_p, cols_p = rows_p or rows, cols_p or cols
    tr, tc = min(tr, rows_p), min(tc, cols_p)
    assert rows_p % tr == 0 and cols_p % tc == 0, (w.shape, rows_p, cols_p)
    return pl.pallas_call(
        functools.partial(_cast_body, rows=rows, cols=cols, tr=tr, tc=tc),
        grid=(rows_p // tr, cols_p // tc),
        in_specs=[pl.BlockSpec((None, tr, tc), lambda i, j: (layer, i, j))],
        out_specs=pl.BlockSpec((tr, tc), lambda i, j: (i, j)),
        out_shape=jax.ShapeDtypeStruct((rows_p, cols_p), BF16),
        compiler_params=_cparams(("parallel", "parallel")),
        name="cast_pad",
    )(w)


def _matmul(a, b, *, b2=None, resid=None, out_dtype=F32, tm=1024, tn=1024, tk=1024, name="mm"):
    M, K = a.shape
    N = b.shape[1]
    tm, tn, tk = min(tm, M), min(tn, N), min(tk, K)
    assert M % tm == 0 and N % tn == 0 and K % tk == 0, (a.shape, b.shape, tm, tn, tk)
    nk = K // tk
    mode = "swiglu" if b2 is not None else ("resid" if resid is not None else "plain")
    in_specs = [pl.BlockSpec((tm, tk), lambda i, j, k: (i, k)),
                pl.BlockSpec((tk, tn), lambda i, j, k: (k, j))]
    args = [a, b]
    scratch = [pltpu.VMEM((tm, tn), F32)]
    if b2 is not None:
        in_specs.append(pl.BlockSpec((tk, tn), lambda i, j, k: (k, j)))
        args.append(b2)
        scratch.append(pltpu.VMEM((tm, tn), F32))
    if resid is not None:
        in_specs.append(pl.BlockSpec((tm, tn), lambda i, j, k: (i, j)))
        args.append(resid)
    return pl.pallas_call(
        functools.partial(_mm_body, nk=nk, mode=mode),
        grid=(M // tm, N // tn, nk),
        in_specs=in_specs,
        out_specs=pl.BlockSpec((tm, tn), lambda i, j, k: (i, j)),
        out_shape=jax.ShapeDtypeStruct((M, N), out_dtype),
        scratch_shapes=scratch,
        compiler_params=_cparams(("parallel", "parallel", "arbitrary")),
        name=name,
    )(*args)


def _rms_body(x_ref, g_ref, o_ref):
    x = x_ref[...]
    ms = jnp.mean(x * x, axis=-1, keepdims=True)
    o_ref[...] = (x * lax.rsqrt(ms + EPS) * g_ref[...]).astype(o_ref.dtype)


def _rmsnorm(x, g, *, tr=256):
    M, D = x.shape
    tr = min(tr, M)
    return pl.pallas_call(
        _rms_body,
        grid=(M // tr,),
        in_specs=[pl.BlockSpec((tr, D), lambda i: (i, 0)),
                  pl.BlockSpec((1, D), lambda i: (0, 0))],
        out_specs=pl.BlockSpec((tr, D), lambda i: (i, 0)),
        out_shape=jax.ShapeDtypeStruct((M, D), BF16),
        compiler_params=_cparams(("parallel",)),
        name="rmsnorm",
    )(x, g.reshape(1, D))


def _rope_tables(T, gs):
    half = gs // 2
    inv = ROPE_THETA ** (-jnp.arange(half, dtype=F32) / half)
    ang = jnp.arange(T, dtype=jnp.int32).astype(F32)[:, None] * inv[None, :]
    cos, sin = jnp.cos(ang), jnp.sin(ang)
    cosg = jnp.concatenate([cos, cos], axis=1)
    sing = jnp.concatenate([-sin, sin], axis=1)
    reps = LANES // gs
    return jnp.tile(cosg, (1, reps)), jnp.tile(sing, (1, reps))


def _group_ones(gs):
    r = jnp.arange(LANES)
    return (r[:, None] // gs == r[None, :] // gs).astype(BF16)


def _seg_body(x_ref, g_ref, cos_ref, sin_ref, bd_ref, o_ref, *, gs, do_norm, do_rope, scale, pick):
    half = gs // 2
    lane = lax.broadcasted_iota(I32, (x_ref.shape[0], LANES), 1)
    for t in range(x_ref.shape[1] // LANES):
        sl = slice(t * LANES, (t + 1) * LANES)
        x = x_ref[:, sl]
        if pick == "dup_low":
            x = jnp.where(lane < LANES // 2, x, pltpu.roll(x, LANES // 2, 1))
        elif pick == "high16":
            x = jnp.where(lane < IDX_HEADS, pltpu.roll(x, LANES // 2, 1), 0.0)
        if do_norm:
            x2 = x * x
            hi = x2.astype(BF16)
            lo = (x2 - hi.astype(F32)).astype(BF16)
            bd = bd_ref[...]
            ssum = jnp.dot(hi, bd, preferred_element_type=F32) + jnp.dot(lo, bd, preferred_element_type=F32)
            x = x * lax.rsqrt(ssum * (1.0 / gs) + EPS) * g_ref[...]
        if do_rope:
            if gs == LANES:
                rot = pltpu.roll(x, half, 1)
            else:
                rot = jnp.where((lane & (gs - 1)) < half, pltpu.roll(x, LANES - half, 1), pltpu.roll(x, half, 1))
            x = x * cos_ref[...] + rot * sin_ref[...]
        if scale != 1.0:
            x = x * scale
        o_ref[:, sl] = x.astype(o_ref.dtype)


def _segment(z, col_off, width, gain, tables, T, *, gs, do_norm, do_rope, scale=1.0, out_dtype=BF16, pick=None,
             tr=512, max_cw=1024):
    M = z.shape[0]
    tr = min(tr, T)
    nrb = T // tr
    cw = math.gcd(math.gcd(col_off, width), max_cw)
    assert cw % LANES == 0, (col_off, width)
    cb = col_off // cw
    cos, sin = tables
    if gain is None:
        gain = jnp.ones((gs,), F32)
    gt = jnp.tile(gain.astype(F32), LANES // gs).reshape(1, LANES)
    return pl.pallas_call(
        functools.partial(_seg_body, gs=gs, do_norm=do_norm, do_rope=do_rope, scale=scale, pick=pick),
        grid=(M // tr, width // cw),
        in_specs=[pl.BlockSpec((tr, cw), lambda i, j: (i, cb + j)),
                  pl.BlockSpec((1, LANES), lambda i, j: (0, 0)),
                  pl.BlockSpec((tr, LANES), lambda i, j: (i % nrb, 0)),
                  pl.BlockSpec((tr, LANES), lambda i, j: (i % nrb, 0)),
                  pl.BlockSpec((LANES, LANES), lambda i, j: (0, 0))],
        out_specs=pl.BlockSpec((tr, cw), lambda i, j: (i, j)),
        out_shape=jax.ShapeDtypeStruct((M, width), out_dtype),
        compiler_params=_cparams(("parallel", "parallel")),
        name="segment",
    )(z, gt, cos, sin, _group_ones(gs))


def _dot_nt(a, b):
    return lax.dot_general(a, b, (((1,), (1,)), ((), ())), preferred_element_type=F32)


def _softmax_steps(ss, vt, m_ref, l_ref, acc_ref, idxs, guard):
    m_prev = [m_ref[i] for i in idxs]
    m_next = [jnp.maximum(mp, jnp.max(s, axis=0, keepdims=True)) for mp, s in zip(m_prev, ss)]
    m_use = [jnp.where(mn == -jnp.inf, 0.0, mn) for mn in m_next] if guard else m_next
    ps = [jnp.exp2(s - mu) for s, mu in zip(ss, m_use)]
    alpha = [jnp.exp2(mp - mu) for mp, mu in zip(m_prev, m_use)]
    pv = [jnp.dot(vt, p.astype(BF16), preferred_element_type=F32) for p in ps]
    for n, i in enumerate(idxs):
        l_ref[i] = alpha[n] * l_ref[i] + jnp.sum(ps[n], axis=0, keepdims=True)
        acc_ref[i] = alpha[n] * acc_ref[i] + pv[n]
        m_ref[i] = m_next[n]


def _dattn_body(lam_ref, sub_ref, q_ref, k_ref, v_ref, o_ref, m_ref, l_ref, acc_ref, *, tq, tk, lam_init):
    i = pl.program_id(2)
    q = q_ref[...]
    lane = lax.broadcasted_iota(I32, q.shape, 1)
    zero = jnp.zeros_like(q)
    qs = (jnp.where(lane < DA_HEAD_DIM, q, zero), jnp.where(lane >= DA_HEAD_DIM, q, zero))
    m_ref[...] = jnp.full(m_ref.shape, -jnp.inf, F32)
    l_ref[...] = jnp.zeros(l_ref.shape, F32)
    acc_ref[...] = jnp.zeros(acc_ref.shape, F32)
    reps = tk // LANES

    def tile_step(j, masked):
        off = pl.multiple_of(j * tk, tk)
        ks = k_ref[pl.ds(off, tk), :]
        vs = v_ref[pl.ds(off, tk), :]
        if masked:
            row = i * tq + lax.broadcasted_iota(I32, (tq, tk), 0)
            col = off + lax.broadcasted_iota(I32, (tq, tk), 1)
            vis = col <= row
        cs = range(2)
        s = [_dot_nt(qs[c], ks) for c in cs]
        if masked:
            s = [jnp.where(vis, x, -jnp.inf) for x in s]
        m_prev = [m_ref[c] for c in cs]
        m_next = [jnp.maximum(m_prev[c], jnp.max(s[c], axis=-1, keepdims=True)) for c in cs]
        p = [jnp.exp2(s[c] - jnp.concatenate([m_next[c]] * reps, axis=1)) for c in cs]
        alpha = [jnp.exp2(m_prev[c] - m_next[c]) for c in cs]
        pv = [jnp.dot(p[c].astype(BF16), vs, preferred_element_type=F32) for c in cs]
        for c in cs:
            l_ref[c] = alpha[c] * l_ref[c] + jnp.sum(p[c], axis=-1, keepdims=True)
            acc_ref[c] = alpha[c] * acc_ref[c] + pv[c]
            m_ref[c] = m_next[c]

    n_full = (i * tq + 1) // tk
    n_all = (i * tq + tq + tk - 1) // tk

    def full_body(j, carry):
        tile_step(j, False)
        return carry

    def diag_body(j, carry):
        tile_step(j, True)
        return carry

    lax.fori_loop(0, n_full, full_body, 0)
    lax.fori_loop(n_full, n_all, diag_body, 0)

    lp = lam_ref[...]
    lam = (jnp.exp(jnp.sum(lp[0:1] * lp[1:2], axis=-1, keepdims=True))
           - jnp.exp(jnp.sum(lp[2:3] * lp[3:4], axis=-1, keepdims=True)) + lam_init)
    o = acc_ref[0] * (1.0 / l_ref[0]) - lam * (acc_ref[1] * (1.0 / l_ref[1]))
    ms = jnp.mean(o * o, axis=-1, keepdims=True)
    o = o * lax.rsqrt(ms + EPS) * sub_ref[...] * (1.0 - lam_init)
    o_ref[...] = o.astype(o_ref.dtype)


def _diff_attention(qh, kh, vh, lam_p, subln, lam_init, B, T, *, tq=512, tk=512):
    M = qh.shape[0]
    tq, tk = min(tq, T), min(tk, T)
    nq = T // tq
    return pl.pallas_call(
        functools.partial(_dattn_body, tq=tq, tk=tk, lam_init=lam_init),
        grid=(B, DA_HEADS, nq),
        in_specs=[pl.BlockSpec((4, DA_HEAD_DIM), lambda b, h, i: (0, 0)),
                  pl.BlockSpec((1, LANES), lambda b, h, i: (0, 0)),
                  pl.BlockSpec((tq, LANES), lambda b, h, i: (b * nq + i, h)),
                  pl.BlockSpec((T, LANES), lambda b, h, i: (b, h)),
                  pl.BlockSpec((T, LANES), lambda b, h, i: (b, h))],
        out_specs=pl.BlockSpec((tq, LANES), lambda b, h, i: (b * nq + i, h)),
        out_shape=jax.ShapeDtypeStruct((M, DA_WIDTH), BF16),
        scratch_shapes=[pltpu.VMEM((2, tq, LANES), F32)] * 3,
        compiler_params=_cparams(("parallel", "parallel", "arbitrary")),
        name="diff_attention",
    )(lam_p.astype(F32), subln.astype(F32).reshape(1, LANES), qh, kh, vh)


def _conv_body(gb_ref, gc_ref, u_ref, w_ref, o_ref, sh_ref, *, T):
    cu = gc_ref[...] * u_ref[...]
    sh_ref[0:SUBLANES, :] = jnp.zeros((SUBLANES, LANES), F32)
    sh_ref[SUBLANES:SUBLANES + T, :] = cu
    w = w_ref[...]
    conv = (sh_ref[SUBLANES - 2:SUBLANES - 2 + T, :] * w[0:1]
            + sh_ref[SUBLANES - 1:SUBLANES - 1 + T, :] * w[1:2]
            + cu * w[2:3])
    o_ref[...] = (gb_ref[...] * conv).astype(o_ref.dtype)


def _short_conv(z, conv_w, B, T):
    M = z.shape[0]
    nc = SC_WIDTH // LANES
    base = 3 * DA_WIDTH // LANES
    return pl.pallas_call(
        functools.partial(_conv_body, T=T),
        grid=(B, nc),
        in_specs=[pl.BlockSpec((T, LANES), lambda b, c: (b, base + c)),
                  pl.BlockSpec((T, LANES), lambda b, c: (b, base + nc + c)),
                  pl.BlockSpec((T, LANES), lambda b, c: (b, base + 2 * nc + c)),
                  pl.BlockSpec((CONV_W, LANES), lambda b, c: (0, c))],
        out_specs=pl.BlockSpec((T, LANES), lambda b, c: (b, c)),
        out_shape=jax.ShapeDtypeStruct((M, SC_WIDTH), BF16),
        scratch_shapes=[pltpu.VMEM((T + SUBLANES, LANES), F32)],
        compiler_params=_cparams(("parallel", "parallel")),
        name="short_conv",
    )(z, z, z, conv_w.astype(F32))


def _sigmoid(x):
    return 1.0 / (1.0 + jnp.exp(-x))


def _rwprep_body(z_ref, zh_ref, vf_ref, mu_ref, w0_ref, a0_ref, v0_ref, w2_ref, a2_ref, v2_ref, g2_ref,
                 r_o, ld_o, k_o, v_o, a_o, g_o, sh_ref, *, tr, nrb):
    i = pl.program_id(0)
    first = (i % nrb) == 0
    sh_ref[SUBLANES - 1:SUBLANES, :] = jnp.where(first, 0.0, zh_ref[SUBLANES - 1:SUBLANES, :])
    sh_ref[SUBLANES:SUBLANES + tr, :] = z_ref[...]

    def shifted(lo, hi):
        zc = z_ref[:, lo:hi]
        zp = sh_ref[SUBLANES - 1:SUBLANES - 1 + tr, lo:hi]
        return zc + (zp - zc) * mu_ref[:, lo:hi]

    r_o[...] = shifted(OD_R, OD_K)
    k_o[...] = shifted(OD_K, OD_V)
    lora = shifted(OD_LORA, OD_RW_END)
    lora_b = lora.astype(BF16)
    lw = w0_ref[...] + jnp.dot(jnp.tanh(lora).astype(BF16), w2_ref[...], preferred_element_type=F32)
    nlw = -lw
    softplus = jnp.maximum(nlw, 0.0) + jnp.log(1.0 + jnp.exp(-jnp.abs(nlw)))
    ld_o[...] = -jnp.exp(-softplus - 0.5)
    a_o[...] = _sigmoid(a0_ref[...] + jnp.dot(lora_b, a2_ref[...], preferred_element_type=F32))
    v = shifted(OD_V, OD_LORA)
    vg = _sigmoid(v0_ref[...] + jnp.dot(lora_b, v2_ref[...], preferred_element_type=F32))
    v_o[...] = v + (vf_ref[...] - v) * vg
    g_o[...] = jnp.dot(_sigmoid(lora).astype(BF16), g2_ref[...], preferred_element_type=F32)


def _place_rows(w, start):
    return jnp.pad(w, ((start, LORA_W - start - w.shape[0]), (0, 0))).astype(BF16)


def _rwkv_prep(z, z_first, mu_p, w0, a0, v0, w2, a2, v2, g2, T, *, tr=128):
    M = z.shape[0]
    tr = min(tr, T)
    nrb = T // tr
    W = OD_RW_END
    hb = tr // SUBLANES
    row = lambda a: a.astype(F32).reshape(1, RW_WIDTH)
    full = lambda shape: pl.BlockSpec(shape, lambda i: (0, 0))
    out = jax.ShapeDtypeStruct((M, RW_WIDTH), F32)
    ospec = pl.BlockSpec((tr, RW_WIDTH), lambda i: (i, 0))
    return pl.pallas_call(
        functools.partial(_rwprep_body, tr=tr, nrb=nrb),
        grid=(M // tr,),
        in_specs=[pl.BlockSpec((tr, W), lambda i: (i, 0)),
                  pl.BlockSpec((SUBLANES, W), lambda i: (jnp.maximum(i * hb - 1, 0), 0)),
                  pl.BlockSpec((tr, RW_WIDTH), lambda i: (i, 2 * DA_WIDTH // RW_WIDTH)),
                  full((1, W)), full((1, RW_WIDTH)), full((1, RW_WIDTH)), full((1, RW_WIDTH)),
                  full((LORA_W, RW_WIDTH)), full((LORA_W, RW_WIDTH)), full((LORA_W, RW_WIDTH)),
                  full((LORA_W, RW_WIDTH))],
        out_specs=[ospec] * 6,
        out_shape=[out] * 6,
        scratch_shapes=[pltpu.VMEM((tr + SUBLANES, W), F32)],
        compiler_params=_cparams(("parallel",)),
        name="rwkv_prep",
    )(z, z, z_first, mu_p, row(w0), row(a0), row(v0),
      _place_rows(w2, 0), _place_rows(a2, W_LORA), _place_rows(v2, W_LORA + A_LORA),
      _place_rows(g2, W_LORA + A_LORA + V_LORA))


def _split3(x):
    hi = x.astype(BF16)
    r1 = x - hi.astype(F32)
    mid = r1.astype(BF16)
    lo = (r1 - mid.astype(F32)).astype(BF16)
    return hi, mid, lo


def _mm(a, b):
    return jnp.dot(a.astype(BF16), b.astype(BF16), preferred_element_type=F32)


def _mm_tn(a, b):
    return jnp.dot(a.T.astype(BF16), b.astype(BF16), preferred_element_type=F32)


def _rwkv_body(r_ref, ld_ref, k_ref, v_ref, a_ref, g_ref, kk_ref, ka_ref, rk_ref, lw_ref, lb_ref,
               o_ref, st_ref, *, ng, L):
    c = pl.program_id(2)

    @pl.when(c == 0)
    def _init():
        st_ref[...] = jnp.zeros_like(st_ref)

    N = RW_HEAD_DIM
    S = RW_GROUP * L
    ri = lax.broadcasted_iota(I32, (S, S), 0)
    ci = lax.broadcasted_iota(I32, (S, S), 1)
    same = (ri // L) == (ci // L)
    incl = same & (ci <= ri)
    strict = same & (ci < ri)
    eye_s = (ci == ri).astype(F32)
    blk16 = (ri // 16) == (ci // 16)
    hmask = (lax.broadcasted_iota(I32, (S, RW_GW), 0) // L) == (lax.broadcasted_iota(I32, (S, RW_GW), 1) // N)
    rl = lax.broadcasted_iota(I32, (L, L), 0)
    cl = lax.broadcasted_iota(I32, (L, L), 1)
    tri = (cl <= rl).astype(BF16)
    rn = lax.broadcasted_iota(I32, (RW_GW, RW_GW), 0)
    cn = lax.broadcasted_iota(I32, (RW_GW, RW_GW), 1)
    eye_g = rn == cn
    ones_g = ((rn // N) == (cn // N)).astype(BF16)

    def gsum(x):
        hi = x.astype(BF16)
        lo = (x - hi.astype(F32)).astype(BF16)
        return jnp.dot(hi, ones_g, preferred_element_type=F32) + jnp.dot(lo, ones_g, preferred_element_type=F32)

    def tile(x):
        return jnp.concatenate([x] * RW_GROUP, axis=0)

    def stack(x):
        return jnp.where(hmask, tile(x), 0.0)

    G = range(ng)
    sls = [slice(gi * RW_GW, (gi + 1) * RW_GW) for gi in G]
    each = lambda f, *lists: [f(*xs) for xs in zip(*lists)]
    r = [r_ref[:, sl] for sl in sls]
    ld = [ld_ref[:, sl] for sl in sls]
    k = [k_ref[:, sl] for sl in sls]
    v = [v_ref[:, sl] for sl in sls]
    a = [a_ref[:, sl] for sl in sls]
    kk = [k[gi] * kk_ref[:, sls[gi]] for gi in G]
    kk = each(lambda x: x / jnp.maximum(jnp.sqrt(gsum(x * x)), 1e-12), kk)
    k2 = [k[gi] * (1.0 + (a[gi] - 1.0) * ka_ref[:, sls[gi]]) for gi in G]
    bv = each(lambda x, y: x * y, kk, a)
    parts = each(_split3, ld)
    cum = each(lambda p: (jnp.dot(tri, p[0], preferred_element_type=F32) + jnp.dot(tri, p[1], preferred_element_type=F32)
                          + jnp.dot(tri, p[2], preferred_element_type=F32)), parts)
    clast = each(lambda c_: c_[L - 1:L, :], cum)
    e_neg = each(lambda c_: jnp.exp(-c_), cum)
    e_l = each(lambda cl_, c_: jnp.exp(cl_ - c_), clast, cum)
    p_l = each(jnp.exp, clast)
    at = each(lambda x, c_, l_: stack(-x * jnp.exp(c_ - l_)), kk, cum, ld)
    rt = each(lambda x, c_: stack(x * jnp.exp(c_)), r, cum)
    vs = each(stack, v)
    bh = each(lambda x, e: stack(x * e), bv, e_l)
    kh = each(lambda x, e: stack(x * e), k2, e_l)
    btb = each(lambda x, e: tile(x * e).astype(BF16), bv, e_neg)
    ktb = each(lambda x, e: tile(x * e).astype(BF16), k2, e_neg)
    atb = each(lambda x: x.astype(BF16), at)
    rtb = each(lambda x: x.astype(BF16), rt)
    mab = each(lambda x, y: jnp.where(strict, _dot_nt(x, y), 0.0), atb, btb)
    mak = each(lambda x, y: jnp.where(strict, _dot_nt(x, y), 0.0), atb, ktb)
    mrb = each(lambda x, y: jnp.where(incl, _dot_nt(x, y), 0.0), rtb, btb)
    mrk = each(lambda x, y: jnp.where(incl, _dot_nt(x, y), 0.0), rtb, ktb)
    nd = each(lambda m: jnp.where(blk16, m, 0.0), mab)
    n2 = each(_mm, nd, nd)
    mv = each(_mm, mak, vs)
    n4 = each(_mm, n2, n2)
    t = each(lambda n, n2_: eye_s + n + _mm(eye_s + n, n2_), nd, n2)
    n8 = each(_mm, n4, n4)
    t = each(lambda t_, n: t_ + _mm(t_, n), t, n4)
    t = each(lambda t_, n: t_ + _mm(t_, n), t, n8)
    size = 16
    while size < L:
        off = ((ri // size) == (ci // size) + 1) & ((ri // (2 * size)) == (ci // (2 * size)))
        u = each(lambda t_, m: _mm(t_, jnp.where(off, m, 0.0)), t, mab)
        t = each(lambda t_, u_: t_ + _mm(u_, t_), t, u)
        size *= 2
    wm = each(_mm, t, at)
    ul = each(_mm, t, mv)
    qe = each(lambda x, m, w: x + _mm(m, w), rt, mrb, wm)
    yl = each(lambda m, u_, m2, x: _mm(m, u_) + _mm(m2, x), mrb, ul, mrk, vs)
    gm = each(lambda b, w, p: _mm_tn(b, w) + jnp.where(eye_g, p, 0.0), bh, wm, p_l)
    hm = each(lambda b, u_, k_, x: _mm_tn(b, u_) + _mm_tn(k_, x), bh, ul, kh, vs)
    st = [st_ref[gi] for gi in G]
    ys = each(lambda q_, s_, y_: _mm(q_, s_) + y_, qe, st, yl)
    st_new = each(lambda g_, s_, h_: _mm(g_, s_) + h_, gm, st, hm)
    for gi in G:
        st_ref[gi] = st_new[gi]
        sl = sls[gi]
        y = ys[gi][0:L]
        for hh in range(1, RW_GROUP):
            y = y + ys[gi][hh * L:(hh + 1) * L]
        mean = gsum(y) * (1.0 / N)
        d = y - mean
        var = gsum(d * d) * (1.0 / N)
        yn = d * lax.rsqrt(var + LNX_EPS) * lw_ref[:, sl] + lb_ref[:, sl]
        yn = yn + gsum(r[gi] * k2[gi] * rk_ref[:, sl]) * v[gi]
        o_ref[:, sl] = (yn * g_ref[:, sl]).astype(o_ref.dtype)


def _rwkv(r, ld, k, v, a, g, k_k, k_a, r_k, lnx_w, lnx_b, B, T, *, ng=4, L=64):
    M = r.shape[0]
    L = min(L, T)
    nc = T // L
    W = ng * RW_GW
    blk = pl.BlockSpec((L, W), lambda b, hg, c: (b * nc + c, hg))
    par = pl.BlockSpec((1, W), lambda b, hg, c: (0, hg))
    row = lambda p: p.astype(F32).reshape(1, RW_WIDTH)
    return pl.pallas_call(
        functools.partial(_rwkv_body, ng=ng, L=L),
        grid=(B, RW_WIDTH // W, nc),
        in_specs=[blk] * 6 + [par] * 5,
        out_specs=blk,
        out_shape=jax.ShapeDtypeStruct((M, RW_WIDTH), BF16),
        scratch_shapes=[pltpu.VMEM((ng, RW_GW, RW_GW), F32)],
        compiler_params=_cparams(("parallel", "parallel", "arbitrary")),
        name="rwkv7_chunk",
    )(r, ld, k, v, a, g, row(k_k), row(k_a), row(r_k), row(lnx_w), row(lnx_b))


def _dsa_body(qd_ref, qi_ref, wi_ref, kd_ref, vd_ref, ki_ref, o_ref,
              keys_ref, bias_ref, qim_ref, cut_ref, m_ref, l_ref, acc_ref, *, tq, tk, ksel, T, hg):
    i = pl.program_id(1)
    nkt = (i * tq + tq + tk - 1) // tk
    krow = lax.broadcasted_iota(I32, (tk, tq), 0)
    qpos = i * tq + lax.broadcasted_iota(I32, (tk, tq), 1)
    low_half = lax.broadcasted_iota(I32, (tq, LANES), 1) < IDX_DIM
    for h in range(IDX_HEADS):
        qt = qi_ref[:, (h // 2) * LANES:(h // 2 + 1) * LANES]
        qim_ref[h] = jnp.where(low_half if h % 2 == 0 else jnp.logical_not(low_half), qt,
                               jnp.zeros_like(qt)).astype(qim_ref.dtype)
    wit = wi_ref[...].T

    def score_tile(j, carry):
        off = pl.multiple_of(j * tk, tk)
        kt = ki_ref[pl.ds(off, tk), :]
        acc = jnp.zeros((tk, tq), F32)
        for h in range(IDX_HEADS):
            acc = acc + jnp.maximum(_dot_nt(kt, qim_ref[h]), 0.0) * wit[h:h + 1, :]
        acc = acc + 0.0
        sc = jnp.where((krow + off) <= qpos, acc, -jnp.inf)
        bits = pltpu.bitcast(sc, I32)
        keys_ref[pl.ds(off, tk), :] = bits ^ ((bits >> 31) & 0x7FFFFFFF)
        return carry

    lax.fori_loop(0, nkt, score_tile, 0)

    def count(pred):
        def body(j, c):
            off = pl.multiple_of(j * tk, tk)
            hit = jnp.where(pred(keys_ref[pl.ds(off, tk), :], krow + off), 1.0, 0.0)
            return c + jnp.sum(hit.reshape(tk // SUBLANES, SUBLANES, tq), axis=0)
        c = lax.fori_loop(0, nkt, body, jnp.zeros((SUBLANES, tq), F32))
        return jnp.sum(c, axis=0, keepdims=True)

    def bit_step(b, thr):
        cand = thr + jnp.left_shift(jnp.int32(1), 31 - b)
        cnt = count(lambda key, _: key >= cand)
        return jnp.where(cnt >= ksel, cand, thr)

    thr = lax.fori_loop(0, 32, bit_step, jnp.full((1, tq), INT_MIN, I32))

    n_gt = count(lambda key, _: key > thr)
    n_ge = count(lambda key, _: key >= thr)
    need = (n_ge > ksel) & (thr > NEG_INF_KEY)
    quota = ksel - n_gt
    cut_ref[...] = jnp.full((1, tq), T, I32)

    @pl.when(jnp.max(jnp.where(need, 1.0, 0.0)) > 0.0)
    def _ties():
        def pos_step(b, p):
            cand = p + jnp.left_shift(jnp.int32(1), (T.bit_length() - 1) - b)
            cnt = count(lambda key, pos: (key == thr) & (pos < cand))
            return jnp.where(cnt < quota, cand, p)
        p = lax.fori_loop(0, T.bit_length(), pos_step, jnp.zeros((1, tq), I32))
        cut_ref[...] = jnp.where(need, p, T)

    cut = cut_ref[...]
    thr_sel = jnp.maximum(thr, NEG_INF_KEY + 1)

    def bias_tile(j, carry):
        off = pl.multiple_of(j * tk, tk)
        key = keys_ref[pl.ds(off, tk), :]
        sel = (key > thr_sel) | ((key == thr_sel) & ((krow + off) <= cut))
        bias_ref[pl.ds(off, tk), :] = jnp.where(sel, 0.0, -jnp.inf)
        return carry

    lax.fori_loop(0, nkt, bias_tile, 0)

    m_ref[...] = jnp.full(m_ref.shape, -jnp.inf, F32)
    l_ref[...] = jnp.zeros(l_ref.shape, F32)
    acc_ref[...] = jnp.zeros(acc_ref.shape, F32)

    def attend(j, carry):
        off = pl.multiple_of(j * tk, tk)
        ks = kd_ref[pl.ds(off, tk), :]
        vt = vd_ref[pl.ds(off, tk), :].T
        bias = bias_ref[pl.ds(off, tk), :]
        for h0 in range(0, SA_HEADS, hg):
            heads = list(range(h0, h0 + hg))
            ss = [_dot_nt(ks, qd_ref[:, h * LANES:(h + 1) * LANES]) + bias for h in heads]
            _softmax_steps(ss, vt, m_ref, l_ref, acc_ref, heads, guard=True)
        return carry

    lax.fori_loop(0, nkt, attend, 0)
    for h in range(SA_HEADS):
        o_ref[:, h * LANES:(h + 1) * LANES] = (acc_ref[h] * (1.0 / l_ref[h])).T.astype(o_ref.dtype)


def _dsa(qd, qi, wi, kd, vd, ki, B, T, ksel, *, tq=256, tk=512, hg=8):
    M = qd.shape[0]
    tq, tk = min(tq, T), min(tk, T)
    nq = T // tq
    qblk = lambda w: pl.BlockSpec((tq, w), lambda b, i: (b * nq + i, 0))
    kblk = pl.BlockSpec((T, LANES), lambda b, i: (b, 0))
    return pl.pallas_call(
        functools.partial(_dsa_body, tq=tq, tk=tk, ksel=ksel, T=T, hg=hg),
        grid=(B, nq),
        in_specs=[qblk(SA_WIDTH), qblk(IDX_HEADS * IDX_DIM), qblk(LANES), kblk, kblk, kblk],
        out_specs=qblk(SA_WIDTH),
        out_shape=jax.ShapeDtypeStruct((M, SA_WIDTH), BF16),
        scratch_shapes=[pltpu.VMEM((T, tq), I32), pltpu.VMEM((T, tq), F32),
                        pltpu.VMEM((IDX_HEADS, tq, LANES), BF16), pltpu.VMEM((1, tq), I32),
                        pltpu.VMEM((SA_HEADS, 1, tq), F32), pltpu.VMEM((SA_HEADS, 1, tq), F32),
                        pltpu.VMEM((SA_HEADS, LANES, tq), F32)],
        compiler_params=_cparams(("parallel", "arbitrary")),
        name="dsa_attention",
    )(qd, qi, wi, kd, vd, ki)


def _even_mixer(xf, h, w_in, w_out, e, q_norm, k_norm, lam_p, subln, conv_w, tabs64, lam_init, B, T):
    z = _matmul_fullk(h, _cast_pad(w_in, e), name="even_in")
    qh = _segment(z, 0, DA_WIDTH, q_norm, tabs64, T, gs=64, do_norm=True, do_rope=True,
                  scale=DA_HEAD_DIM ** -0.5 * LOG2E)
    kh = _segment(z, DA_WIDTH, DA_WIDTH, k_norm, tabs64, T, gs=64, do_norm=True, do_rope=True)
    vh = _segment(z, 2 * DA_WIDTH, DA_WIDTH, None, tabs64, T, gs=64, do_norm=False, do_rope=False)
    o = _diff_attention(qh, kh, vh, lam_p, subln, lam_init, B, T)
    y = _short_conv(z, conv_w, B, T)
    return _matmul_fullk(o, _cast_pad(w_out, e), a2=y, resid=xf, name="even_out"), z


def _odd_mixer(xf, h, w_in_t, w_out, o, mu_p, w0, w2, a0, a2, v0, v2, g2, k_k, k_a, r_k, lnx_w, lnx_b,
               q_norm, k_norm, idxk_norm, z_first, tabs64, tabs128, B, T, ksel):
    z = _matmul_fullk(h, w_in_t, b_transposed=True, name="odd_in")
    r, ld, k, v, a, g = _rwkv_prep(z, z_first, mu_p, w0, a0, v0, w2, a2, v2, g2, T)
    rw_out = _rwkv(r, ld, k, v, a, g, k_k, k_a, r_k.reshape(-1), lnx_w, lnx_b, B, T)
    qd = _segment(z, OD_Q, SA_WIDTH, q_norm, tabs128, T, gs=128, do_norm=True, do_rope=True,
                  scale=SA_HEAD_DIM ** -0.5 * LOG2E)
    kd = _segment(z, OD_KD, LANES, k_norm, tabs128, T, gs=128, do_norm=True, do_rope=True)
    vd = _segment(z, OD_VDD, LANES, None, tabs128, T, gs=128, do_norm=False, do_rope=False)
    qi = _segment(z, OD_QI, IDX_HEADS * IDX_DIM, None, tabs64, T, gs=64, do_norm=False, do_rope=True)
    ki = _segment(z, OD_KI, LANES, idxk_norm, tabs64, T, gs=64, do_norm=True, do_rope=True, pick="dup_low")
    wi = _segment(z, OD_KI, LANES, None, tabs64, T, gs=64, do_norm=False, do_rope=False,
                  scale=IDX_HEADS ** -0.5 * IDX_DIM ** -0.5, out_dtype=F32, pick="high16")
    sa_out = _dsa(qd, qi, wi, kd, vd, ki, B, T, ksel)
    return _matmul_fullk(rw_out, _cast_pad(w_out, o), a2=sa_out, resid=xf, name="odd_out")


def kernel(x, mix_norm, ffn_norm, ffn_gate, ffn_up, ffn_down, ev_w_in, ev_w_out, da_q_norm, da_k_norm, da_lambda, da_subln, sc_conv, od_w_in, od_w_out, rw_mu, rw_w0, rw_w2, rw_a0, rw_a2, rw_v0, rw_v2, rw_g2, rw_k_k, rw_k_a, rw_r_k, rw_lnx_w, rw_lnx_b, sa_q_norm, sa_k_norm, idx_k_norm):
    B, T, D = x.shape
    M = B * T
    ksel = min(TOPK_MAX, T // 4)
    xf = x.reshape(M, D)
    tabs64 = _rope_tables(T, 64)
    tabs128 = _rope_tables(T, 128)
    z_first = None
    for i in range(DEPTH):
        h = _rmsnorm(xf, mix_norm[i])
        if i % 2 == 0:
            e = i // 2
            lam_init = 0.8 - 0.6 * math.exp(-0.3 * i)
            xf, z = _even_mixer(xf, h, ev_w_in, ev_w_out, e, da_q_norm[e],
                                da_k_norm[e], da_lambda[e], da_subln[e], sc_conv[e], tabs64, lam_init, B, T)
            if z_first is None:
                z_first = z
        else:
            o = i // 2
            xf = _odd_mixer(xf, h, _cast_pad(jnp.swapaxes(od_w_in, 1, 2), o, rows_p=OD_PAD), od_w_out, o,
                            rw_mu[o].astype(F32).reshape(1, OD_RW_END),
                            rw_w0[o], rw_w2[o], rw_a0[o], rw_a2[o], rw_v0[o], rw_v2[o], rw_g2[o],
                            rw_k_k[o], rw_k_a[o], rw_r_k[o], rw_lnx_w[o], rw_lnx_b[o],
                            sa_q_norm[o], sa_k_norm[o], idx_k_norm[o], z_first, tabs64, tabs128, B, T, ksel)
        h = _rmsnorm(xf, ffn_norm[i])
        hid = _wproj(h, ffn_gate, i, w2=ffn_up, out_dtype=BF16, name="ffn_in")
        wd = _cast_pad(ffn_down, i, tr=256, tc=2048)
        xf = _matmul_fullk(hid, wd, resid=xf, tm=512, tn=256, name="ffn_out")
    return xf.reshape(B, T, D)
```

```python
import functools
import math

import jax
import jax.numpy as jnp
from jax import lax
from jax.experimental import pallas as pl
from jax.experimental.pallas import tpu as pltpu

F32 = jnp.float32
BF16 = jnp.bfloat16
I32 = jnp.int32

D_MODEL = 4096
DEPTH = 4
DA_WIDTH = 2048
DA_HEADS = 16
DA_HEAD_DIM = 64
SC_WIDTH = 2048
CONV_W = 3
RW_WIDTH = 2048
RW_HEAD_DIM = 64
RW_HEADS = 32
RW_GROUP = 4
RW_GW = RW_GROUP * RW_HEAD_DIM
W_LORA, A_LORA, V_LORA, G_LORA = 96, 96, 64, 256
LNX_EPS = 64e-5
SA_WIDTH = 2048
SA_HEAD_DIM = 128
SA_HEADS = 16
IDX_HEADS = 16
IDX_DIM = 64
TOPK_MAX = 256
FFN_HIDDEN = 11008
ROPE_THETA = 10000.0
EPS = 1e-6

LANES = 128
SUBLANES = 8
V7X_VMEM_BYTES = 64 * 1024 * 1024
VMEM_LIMIT = (V7X_VMEM_BYTES * 3) // 4

OD_R, OD_K, OD_V = 0, 2048, 4096
OD_LORA = 3 * RW_WIDTH
LORA_W = W_LORA + A_LORA + V_LORA + G_LORA
OD_RW_END = OD_LORA + LORA_W
OD_Q = OD_RW_END
OD_KD = OD_Q + SA_WIDTH
OD_VDD = OD_KD + SA_HEAD_DIM
OD_QI = OD_VDD + SA_HEAD_DIM
OD_KI = OD_QI + IDX_HEADS * IDX_DIM
OD_PAD = 10240
LOG2E = math.log2(math.e)
INT_MIN = -2 ** 31
NEG_INF_KEY = -2139095041


def _cparams(sem):
    return pltpu.CompilerParams(dimension_semantics=sem, vmem_limit_bytes=VMEM_LIMIT)


def _mm_body(*refs, k1, bt, has_resid):
    refs = list(refs)
    a_ref = refs.pop(0)
    a2_ref = refs.pop(0) if k1 else None
    b_ref = refs.pop(0)
    r_ref = refs.pop(0) if has_resid else None
    o_ref = refs.pop(0)
    if bt:
        acc = _dot_nt(a_ref[...], b_ref[...])
    elif k1:
        acc = (jnp.dot(a_ref[...], b_ref[0:k1, :], preferred_element_type=F32)
               + jnp.dot(a2_ref[...], b_ref[k1:, :], preferred_element_type=F32))
    else:
        acc = jnp.dot(a_ref[...], b_ref[...], preferred_element_type=F32)
    o_ref[...] = (r_ref[...] + acc) if has_resid else acc.astype(o_ref.dtype)


def _matmul(a, b, *, a2=None, resid=None, b_transposed=False, tm=1024, tn=512, name="mm"):
    M, k1 = a.shape
    K = k1 + (a2.shape[1] if a2 is not None else 0)
    N = b.shape[0] if b_transposed else b.shape[1]
    assert (b.shape[1] if b_transposed else b.shape[0]) == K and not (b_transposed and a2 is not None)
    tm, tn = min(tm, M), min(tn, N)
    assert M % tm == 0 and N % tn == 0, (a.shape, b.shape, tm, tn)
    in_specs = [pl.BlockSpec((tm, k1), lambda i, j: (i, 0))]
    args = [a]
    if a2 is not None:
        in_specs.append(pl.BlockSpec((tm, K - k1), lambda i, j: (i, 0)))
        args.append(a2)
    in_specs.append(pl.BlockSpec((tn, K), lambda i, j: (j, 0)) if b_transposed
                    else pl.BlockSpec((K, tn), lambda i, j: (0, j)))
    args.append(b)
    if resid is not None:
        in_specs.append(pl.BlockSpec((tm, tn), lambda i, j: (i, j)))
        args.append(resid)
    return pl.pallas_call(
        functools.partial(_mm_body, k1=k1 if a2 is not None else 0, bt=b_transposed, has_resid=resid is not None),
        grid=(M // tm, N // tn),
        in_specs=in_specs,
        out_specs=pl.BlockSpec((tm, tn), lambda i, j: (i, j)),
        out_shape=jax.ShapeDtypeStruct((M, N), F32),
        compiler_params=_cparams(("parallel", "arbitrary")),
        name=name,
    )(*args)


def _swiglu_body(a_ref, wg_ref, wu_ref, o_ref, gb_ref, ub_ref):
    @pl.when(pl.program_id(1) == 0)
    def _cast():
        gb_ref[...] = wg_ref[...].astype(BF16)
        ub_ref[...] = wu_ref[...].astype(BF16)

    a = a_ref[...]
    g = jnp.dot(a, gb_ref[...], preferred_element_type=F32)
    u = jnp.dot(a, ub_ref[...], preferred_element_type=F32)
    o_ref[...] = (g * (1.0 / (1.0 + jnp.exp(-g))) * u).astype(o_ref.dtype)


def _swiglu_proj(a, w_gate, w_up, layer, *, tm=1024, tn=256, name="ffn_in"):
    M, K = a.shape
    N = w_gate.shape[2]
    tm, tn = min(tm, M), min(tn, N)
    assert M % tm == 0 and N % tn == 0 and w_gate.shape[1] == K, (a.shape, w_gate.shape, tm, tn)
    wspec = pl.BlockSpec((None, K, tn), lambda j, i: (layer, 0, j))
    return pl.pallas_call(
        _swiglu_body,
        grid=(N // tn, M // tm),
        in_specs=[pl.BlockSpec((tm, K), lambda j, i: (i, 0)), wspec, wspec],
        out_specs=pl.BlockSpec((tm, tn), lambda j, i: (i, j)),
        out_shape=jax.ShapeDtypeStruct((M, N), BF16),
        scratch_shapes=[pltpu.VMEM((K, tn), BF16), pltpu.VMEM((K, tn), BF16)],
        compiler_params=_cparams(("parallel", "arbitrary")),
        name=name,
    )(a, w_gate, w_up)


def _cast_body(x_ref, o_ref, *, rows, cols, tr, tc):
    x = x_ref[...]
    r = pl.program_id(0) * tr + lax.broadcasted_iota(I32, x.shape, 0)
    c = pl.program_id(1) * tc + lax.broadcasted_iota(I32, x.shape, 1)
    o_ref[...] = jnp.where((r < rows) & (c < cols), x, 0.0).astype(o_ref.dtype)


def _cast_pad(w, layer, rows_p=None, cols_p=None, *, tr=512, tc=1024):
    _, rows, cols = w.shape
    rows_p, cols_p = rows_p or rows, cols_p or cols
    tr, tc = min(tr, rows_p), min(tc, cols_p)
    assert rows_p % tr == 0 and cols_p % tc == 0, (w.shape, rows_p, cols_p)
    return pl.pallas_call(
        functools.partial(_cast_body, rows=rows, cols=cols, tr=tr, tc=tc),
        grid=(rows_p // tr, cols_p // tc),
        in_specs=[pl.BlockSpec((None, tr, tc), lambda i, j: (layer, i, j))],
        out_specs=pl.BlockSpec((tr, tc), lambda i, j: (i, j)),
        out_shape=jax.ShapeDtypeStruct((rows_p, cols_p), BF16),
        compiler_params=_cparams(("parallel", "parallel")),
        name="cast_pad",
    )(w)


def _rms_body(x_ref, g_ref, o_ref):
    x = x_ref[...]
    ms = jnp.mean(x * x, axis=-1, keepdims=True)
    o_ref[...] = (x * lax.rsqrt(ms + EPS) * g_ref[...]).astype(o_ref.dtype)


def _rmsnorm(x, g, *, tr=256):
    M, D = x.shape
    tr = min(tr, M)
    return pl.pallas_call(
        _rms_body,
        grid=(M // tr,),
        in_specs=[pl.BlockSpec((tr, D), lambda i: (i, 0)),
                  pl.BlockSpec((1, D), lambda i: (0, 0))],
        out_specs=pl.BlockSpec((tr, D), lambda i: (i, 0)),
        out_shape=jax.ShapeDtypeStruct((M, D), BF16),
        compiler_params=_cparams(("parallel",)),
        name="rmsnorm",
    )(x, g.reshape(1, D))


def _rope_tables(T, gs):
    half = gs // 2
    inv = ROPE_THETA ** (-jnp.arange(half, dtype=F32) / half)
    ang = jnp.arange(T, dtype=jnp.int32).astype(F32)[:, None] * inv[None, :]
    cos, sin = jnp.cos(ang), jnp.sin(ang)
    cosg = jnp.concatenate([cos, cos], axis=1)
    sing = jnp.concatenate([-sin, sin], axis=1)
    reps = LANES // gs
    return jnp.tile(cosg, (1, reps)), jnp.tile(sing, (1, reps))


def _group_ones(gs):
    r = jnp.arange(LANES)
    return (r[:, None] // gs == r[None, :] // gs).astype(BF16)


def _seg_body(x_ref, g_ref, cos_ref, sin_ref, bd_ref, o_ref, *, gs, do_norm, do_rope, scale, pick):
    half = gs // 2
    lane = lax.broadcasted_iota(I32, (x_ref.shape[0], LANES), 1)
    for t in range(x_ref.shape[1] // LANES):
        sl = slice(t * LANES, (t + 1) * LANES)
        x = x_ref[:, sl]
        if pick == "dup_low":
            x = jnp.where(lane < LANES // 2, x, pltpu.roll(x, LANES // 2, 1))
        elif pick == "high16":
            x = jnp.where(lane < IDX_HEADS, pltpu.roll(x, LANES // 2, 1), 0.0)
        if do_norm:
            x2 = x * x
            hi = x2.astype(BF16)
            lo = (x2 - hi.astype(F32)).astype(BF16)
            bd = bd_ref[...]
            ssum = jnp.dot(hi, bd, preferred_element_type=F32) + jnp.dot(lo, bd, preferred_element_type=F32)
            x = x * lax.rsqrt(ssum * (1.0 / gs) + EPS) * g_ref[...]
        if do_rope:
            if gs == LANES:
                rot = pltpu.roll(x, half, 1)
            else:
                rot = jnp.where((lane & (gs - 1)) < half, pltpu.roll(x, LANES - half, 1), pltpu.roll(x, half, 1))
            x = x * cos_ref[...] + rot * sin_ref[...]
        if scale != 1.0:
            x = x * scale
        o_ref[:, sl] = x.astype(o_ref.dtype)


def _segment(z, col_off, width, gain, tables, T, *, gs, do_norm, do_rope, scale=1.0, out_dtype=BF16, pick=None,
             tr=512, max_cw=1024):
    M = z.shape[0]
    tr = min(tr, T)
    nrb = T // tr
    cw = math.gcd(math.gcd(col_off, width), max_cw)
    assert cw % LANES == 0, (col_off, width)
    cb = col_off // cw
    cos, sin = tables
    if gain is None:
        gain = jnp.ones((gs,), F32)
    gt = jnp.tile(gain.astype(F32), LANES // gs).reshape(1, LANES)
    return pl.pallas_call(
        functools.partial(_seg_body, gs=gs, do_norm=do_norm, do_rope=do_rope, scale=scale, pick=pick),
        grid=(M // tr, width // cw),
        in_specs=[pl.BlockSpec((tr, cw), lambda i, j: (i, cb + j)),
                  pl.BlockSpec((1, LANES), lambda i, j: (0, 0)),
                  pl.BlockSpec((tr, LANES), lambda i, j: (i % nrb, 0)),
                  pl.BlockSpec((tr, LANES), lambda i, j: (i % nrb, 0)),
                  pl.BlockSpec((LANES, LANES), lambda i, j: (0, 0))],
        out_specs=pl.BlockSpec((tr, cw), lambda i, j: (i, j)),
        out_shape=jax.ShapeDtypeStruct((M, width), out_dtype),
        compiler_params=_cparams(("parallel", "parallel")),
        name="segment",
    )(z, gt, cos, sin, _group_ones(gs))


def _dot_nt(a, b):
    return lax.dot_general(a, b, (((1,), (1,)), ((), ())), preferred_element_type=F32)


def _softmax_steps(ss, vt, m_ref, l_ref, acc_ref, idxs, guard):
    m_prev = [m_ref[i] for i in idxs]
    m_next = [jnp.maximum(mp, jnp.max(s, axis=0, keepdims=True)) for mp, s in zip(m_prev, ss)]
    m_use = [jnp.where(mn == -jnp.inf, 0.0, mn) for mn in m_next] if guard else m_next
    ps = [jnp.exp2(s - mu) for s, mu in zip(ss, m_use)]
    alpha = [jnp.exp2(mp - mu) for mp, mu in zip(m_prev, m_use)]
    pv = [jnp.dot(vt, p.astype(BF16), preferred_element_type=F32) for p in ps]
    for n, i in enumerate(idxs):
        l_ref[i] = alpha[n] * l_ref[i] + jnp.sum(ps[n], axis=0, keepdims=True)
        acc_ref[i] = alpha[n] * acc_ref[i] + pv[n]
        m_ref[i] = m_next[n]


def _dattn_body(lam_ref, sub_ref, q_ref, k_ref, v_ref, o_ref, m_ref, l_ref, acc_ref, *, tq, tk, lam_init):
    i = pl.program_id(2)
    q = q_ref[...]
    lane = lax.broadcasted_iota(I32, q.shape, 1)
    zero = jnp.zeros_like(q)
    qs = (jnp.where(lane < DA_HEAD_DIM, q, zero), jnp.where(lane >= DA_HEAD_DIM, q, zero))
    m_ref[...] = jnp.full(m_ref.shape, -jnp.inf, F32)
    l_ref[...] = jnp.zeros(l_ref.shape, F32)
    acc_ref[...] = jnp.zeros(acc_ref.shape, F32)
    reps = tk // LANES

    def tile_step(j, masked):
        off = pl.multiple_of(j * tk, tk)
        ks = k_ref[pl.ds(off, tk), :]
        vs = v_ref[pl.ds(off, tk), :]
        if masked:
            row = i * tq + lax.broadcasted_iota(I32, (tq, tk), 0)
            col = off + lax.broadcasted_iota(I32, (tq, tk), 1)
            vis = col <= row
        cs = range(2)
        s = [_dot_nt(qs[c], ks) for c in cs]
        if masked:
            s = [jnp.where(vis, x, -jnp.inf) for x in s]
        m_prev = [m_ref[c] for c in cs]
        m_next = [jnp.maximum(m_prev[c], jnp.max(s[c], axis=-1, keepdims=True)) for c in cs]
        p = [jnp.exp2(s[c] - jnp.concatenate([m_next[c]] * reps, axis=1)) for c in cs]
        alpha = [jnp.exp2(m_prev[c] - m_next[c]) for c in cs]
        pv = [jnp.dot(p[c].astype(BF16), vs, preferred_element_type=F32) for c in cs]
        for c in cs:
            l_ref[c] = alpha[c] * l_ref[c] + jnp.sum(p[c], axis=-1, keepdims=True)
            acc_ref[c] = alpha[c] * acc_ref[c] + pv[c]
            m_ref[c] = m_next[c]

    n_full = (i * tq + 1) // tk
    n_all = (i * tq + tq + tk - 1) // tk

    def full_body(j, carry):
        tile_step(j, False)
        return carry

    def diag_body(j, carry):
        tile_step(j, True)
        return carry

    lax.fori_loop(0, n_full, full_body, 0)
    lax.fori_loop(n_full, n_all, diag_body, 0)

    lp = lam_ref[...]
    lam = (jnp.exp(jnp.sum(lp[0:1] * lp[1:2], axis=-1, keepdims=True))
           - jnp.exp(jnp.sum(lp[2:3] * lp[3:4], axis=-1, keepdims=True)) + lam_init)
    o = acc_ref[0] * (1.0 / l_ref[0]) - lam * (acc_ref[1] * (1.0 / l_ref[1]))
    ms = jnp.mean(o * o, axis=-1, keepdims=True)
    o = o * lax.rsqrt(ms + EPS) * sub_ref[...] * (1.0 - lam_init)
    o_ref[...] = o.astype(o_ref.dtype)


def _diff_attention(qh, kh, vh, lam_p, subln, lam_init, B, T, *, tq=512, tk=512):
    M = qh.shape[0]
    tq, tk = min(tq, T), min(tk, T)
    nq = T // tq
    return pl.pallas_call(
        functools.partial(_dattn_body, tq=tq, tk=tk, lam_init=lam_init),
        grid=(B, DA_HEADS, nq),
        in_specs=[pl.BlockSpec((4, DA_HEAD_DIM), lambda b, h, i: (0, 0)),
                  pl.BlockSpec((1, LANES), lambda b, h, i: (0, 0)),
                  pl.BlockSpec((tq, LANES), lambda b, h, i: (b * nq + i, h)),
                  pl.BlockSpec((T, LANES), lambda b, h, i: (b, h)),
                  pl.BlockSpec((T, LANES), lambda b, h, i: (b, h))],
        out_specs=pl.BlockSpec((tq, LANES), lambda b, h, i: (b * nq + i, h)),
        out_shape=jax.ShapeDtypeStruct((M, DA_WIDTH), BF16),
        scratch_shapes=[pltpu.VMEM((2, tq, LANES), F32)] * 3,
        compiler_params=_cparams(("parallel", "parallel", "arbitrary")),
        name="diff_attention",
    )(lam_p.astype(F32), subln.astype(F32).reshape(1, LANES), qh, kh, vh)


def _conv_body(gb_ref, gc_ref, u_ref, w_ref, o_ref, sh_ref, *, T):
    cu = gc_ref[...] * u_ref[...]
    sh_ref[0:SUBLANES, :] = jnp.zeros((SUBLANES, LANES), F32)
    sh_ref[SUBLANES:SUBLANES + T, :] = cu
    w = w_ref[...]
    conv = (sh_ref[SUBLANES - 2:SUBLANES - 2 + T, :] * w[0:1]
            + sh_ref[SUBLANES - 1:SUBLANES - 1 + T, :] * w[1:2]
            + cu * w[2:3])
    o_ref[...] = (gb_ref[...] * conv).astype(o_ref.dtype)


def _short_conv(z, conv_w, B, T):
    M = z.shape[0]
    nc = SC_WIDTH // LANES
    base = 3 * DA_WIDTH // LANES
    return pl.pallas_call(
        functools.partial(_conv_body, T=T),
        grid=(B, nc),
        in_specs=[pl.BlockSpec((T, LANES), lambda b, c: (b, base + c)),
                  pl.BlockSpec((T, LANES), lambda b, c: (b, base + nc + c)),
                  pl.BlockSpec((T, LANES), lambda b, c: (b, base + 2 * nc + c)),
                  pl.BlockSpec((CONV_W, LANES), lambda b, c: (0, c))],
        out_specs=pl.BlockSpec((T, LANES), lambda b, c: (b, c)),
        out_shape=jax.ShapeDtypeStruct((M, SC_WIDTH), BF16),
        scratch_shapes=[pltpu.VMEM((T + SUBLANES, LANES), F32)],
        compiler_params=_cparams(("parallel", "parallel")),
        name="short_conv",
    )(z, z, z, conv_w.astype(F32))


def _sigmoid(x):
    return 1.0 / (1.0 + jnp.exp(-x))


def _rwprep_body(z_ref, zh_ref, vf_ref, mu_ref, w0_ref, a0_ref, v0_ref, w2_ref, a2_ref, v2_ref, g2_ref,
                 r_o, ld_o, k_o, v_o, a_o, g_o, sh_ref, *, tr, nrb):
    i = pl.program_id(0)
    first = (i % nrb) == 0
    sh_ref[SUBLANES - 1:SUBLANES, :] = jnp.where(first, 0.0, zh_ref[SUBLANES - 1:SUBLANES, :])
    sh_ref[SUBLANES:SUBLANES + tr, :] = z_ref[...]

    def shifted(lo, hi):
        zc = z_ref[:, lo:hi]
        zp = sh_ref[SUBLANES - 1:SUBLANES - 1 + tr, lo:hi]
        return zc + (zp - zc) * mu_ref[:, lo:hi]

    r_o[...] = shifted(OD_R, OD_K)
    k_o[...] = shifted(OD_K, OD_V)
    lora = shifted(OD_LORA, OD_RW_END)
    lora_b = lora.astype(BF16)
    lw = w0_ref[...] + jnp.dot(jnp.tanh(lora).astype(BF16), w2_ref[...], preferred_element_type=F32)
    nlw = -lw
    softplus = jnp.maximum(nlw, 0.0) + jnp.log(1.0 + jnp.exp(-jnp.abs(nlw)))
    ld_o[...] = -jnp.exp(-softplus - 0.5)
    a_o[...] = _sigmoid(a0_ref[...] + jnp.dot(lora_b, a2_ref[...], preferred_element_type=F32))
    v = shifted(OD_V, OD_LORA)
    vg = _sigmoid(v0_ref[...] + jnp.dot(lora_b, v2_ref[...], preferred_element_type=F32))
    v_o[...] = v + (vf_ref[...] - v) * vg
    g_o[...] = jnp.dot(_sigmoid(lora).astype(BF16), g2_ref[...], preferred_element_type=F32)


def _place_rows(w, start):
    return jnp.pad(w, ((start, LORA_W - start - w.shape[0]), (0, 0))).astype(BF16)


def _rwkv_prep(z, z_first, mu_p, w0, a0, v0, w2, a2, v2, g2, T, *, tr=128):
    M = z.shape[0]
    tr = min(tr, T)
    nrb = T // tr
    W = OD_RW_END
    hb = tr // SUBLANES
    row = lambda a: a.astype(F32).reshape(1, RW_WIDTH)
    full = lambda shape: pl.BlockSpec(shape, lambda i: (0, 0))
    out = jax.ShapeDtypeStruct((M, RW_WIDTH), F32)
    ospec = pl.BlockSpec((tr, RW_WIDTH), lambda i: (i, 0))
    return pl.pallas_call(
        functools.partial(_rwprep_body, tr=tr, nrb=nrb),
        grid=(M // tr,),
        in_specs=[pl.BlockSpec((tr, W), lambda i: (i, 0)),
                  pl.BlockSpec((SUBLANES, W), lambda i: (jnp.maximum(i * hb - 1, 0), 0)),
                  pl.BlockSpec((tr, RW_WIDTH), lambda i: (i, 2 * DA_WIDTH // RW_WIDTH)),
                  full((1, W)), full((1, RW_WIDTH)), full((1, RW_WIDTH)), full((1, RW_WIDTH)),
                  full((LORA_W, RW_WIDTH)), full((LORA_W, RW_WIDTH)), full((LORA_W, RW_WIDTH)),
                  full((LORA_W, RW_WIDTH))],
        out_specs=[ospec] * 6,
        out_shape=[out] * 6,
        scratch_shapes=[pltpu.VMEM((tr + SUBLANES, W), F32)],
        compiler_params=_cparams(("parallel",)),
        name="rwkv_prep",
    )(z, z, z_first, mu_p, row(w0), row(a0), row(v0),
      _place_rows(w2, 0), _place_rows(a2, W_LORA), _place_rows(v2, W_LORA + A_LORA),
      _place_rows(g2, W_LORA + A_LORA + V_LORA))


def _split3(x):
    hi = x.astype(BF16)
    r1 = x - hi.astype(F32)
    mid = r1.astype(BF16)
    lo = (r1 - mid.astype(F32)).astype(BF16)
    return hi, mid, lo


def _mm(a, b):
    return jnp.dot(a.astype(BF16), b.astype(BF16), preferred_element_type=F32)


def _mm_tn(a, b):
    return jnp.dot(a.T.astype(BF16), b.astype(BF16), preferred_element_type=F32)


def _rwkv_body(r_ref, ld_ref, k_ref, v_ref, a_ref, g_ref, kk_ref, ka_ref, rk_ref, lw_ref, lb_ref,
               o_ref, st_ref, *, ng, L):
    c = pl.program_id(2)

    @pl.when(c == 0)
    def _init():
        st_ref[...] = jnp.zeros_like(st_ref)

    N = RW_HEAD_DIM
    S = RW_GROUP * L
    ri = lax.broadcasted_iota(I32, (S, S), 0)
    ci = lax.broadcasted_iota(I32, (S, S), 1)
    same = (ri // L) == (ci // L)
    incl = same & (ci <= ri)
    strict = same & (ci < ri)
    eye_s = (ci == ri).astype(F32)
    blk16 = (ri // 16) == (ci // 16)
    hmask = (lax.broadcasted_iota(I32, (S, RW_GW), 0) // L) == (lax.broadcasted_iota(I32, (S, RW_GW), 1) // N)
    rl = lax.broadcasted_iota(I32, (L, L), 0)
    cl = lax.broadcasted_iota(I32, (L, L), 1)
    tri = (cl <= rl).astype(BF16)
    rn = lax.broadcasted_iota(I32, (RW_GW, RW_GW), 0)
    cn = lax.broadcasted_iota(I32, (RW_GW, RW_GW), 1)
    eye_g = rn == cn
    ones_g = ((rn // N) == (cn // N)).astype(BF16)

    def gsum(x):
        hi = x.astype(BF16)
        lo = (x - hi.astype(F32)).astype(BF16)
        return jnp.dot(hi, ones_g, preferred_element_type=F32) + jnp.dot(lo, ones_g, preferred_element_type=F32)

    def tile(x):
        return jnp.concatenate([x] * RW_GROUP, axis=0)

    def stack(x):
        return jnp.where(hmask, tile(x), 0.0)

    G = range(ng)
    sls = [slice(gi * RW_GW, (gi + 1) * RW_GW) for gi in G]
    each = lambda f, *lists: [f(*xs) for xs in zip(*lists)]
    r = [r_ref[:, sl] for sl in sls]
    ld = [ld_ref[:, sl] for sl in sls]
    k = [k_ref[:, sl] for sl in sls]
    v = [v_ref[:, sl] for sl in sls]
    a = [a_ref[:, sl] for sl in sls]
    kk = [k[gi] * kk_ref[:, sls[gi]] for gi in G]
    kk = each(lambda x: x / jnp.maximum(jnp.sqrt(gsum(x * x)), 1e-12), kk)
    k2 = [k[gi] * (1.0 + (a[gi] - 1.0) * ka_ref[:, sls[gi]]) for gi in G]
    bv = each(lambda x, y: x * y, kk, a)
    parts = each(_split3, ld)
    cum = each(lambda p: (jnp.dot(tri, p[0], preferred_element_type=F32) + jnp.dot(tri, p[1], preferred_element_type=F32)
                          + jnp.dot(tri, p[2], preferred_element_type=F32)), parts)
    clast = each(lambda c_: c_[L - 1:L, :], cum)
    e_neg = each(lambda c_: jnp.exp(-c_), cum)
    e_l = each(lambda cl_, c_: jnp.exp(cl_ - c_), clast, cum)
    p_l = each(jnp.exp, clast)
    at = each(lambda x, c_, l_: stack(-x * jnp.exp(c_ - l_)), kk, cum, ld)
    rt = each(lambda x, c_: stack(x * jnp.exp(c_)), r, cum)
    vs = each(stack, v)
    bh = each(lambda x, e: stack(x * e), bv, e_l)
    kh = each(lambda x, e: stack(x * e), k2, e_l)
    btb = each(lambda x, e: tile(x * e).astype(BF16), bv, e_neg)
    ktb = each(lambda x, e: tile(x * e).astype(BF16), k2, e_neg)
    atb = each(lambda x: x.astype(BF16), at)
    rtb = each(lambda x: x.astype(BF16), rt)
    mab = each(lambda x, y: jnp.where(strict, _dot_nt(x, y), 0.0), atb, btb)
    mak = each(lambda x, y: jnp.where(strict, _dot_nt(x, y), 0.0), atb, ktb)
    mrb = each(lambda x, y: jnp.where(incl, _dot_nt(x, y), 0.0), rtb, btb)
    mrk = each(lambda x, y: jnp.where(incl, _dot_nt(x, y), 0.0), rtb, ktb)
    nd = each(lambda m: jnp.where(blk16, m, 0.0), mab)
    n2 = each(_mm, nd, nd)
    mv = each(_mm, mak, vs)
    n4 = each(_mm, n2, n2)
    t = each(lambda n, n2_: eye_s + n + _mm(eye_s + n, n2_), nd, n2)
    n8 = each(_mm, n4, n4)
    t = each(lambda t_, n: t_ + _mm(t_, n), t, n4)
    t = each(lambda t_, n: t_ + _mm(t_, n), t, n8)
    size = 16
    while size < L:
        off = ((ri // size) == (ci // size) + 1) & ((ri // (2 * size)) == (ci // (2 * size)))
        u = each(lambda t_, m: _mm(t_, jnp.where(off, m, 0.0)), t, mab)
        t = each(lambda t_, u_: t_ + _mm(u_, t_), t, u)
        size *= 2
    wm = each(_mm, t, at)
    ul = each(_mm, t, mv)
    qe = each(lambda x, m, w: x + _mm(m, w), rt, mrb, wm)
    yl = each(lambda m, u_, m2, x: _mm(m, u_) + _mm(m2, x), mrb, ul, mrk, vs)
    gm = each(lambda b, w, p: _mm_tn(b, w) + jnp.where(eye_g, p, 0.0), bh, wm, p_l)
    hm = each(lambda b, u_, k_, x: _mm_tn(b, u_) + _mm_tn(k_, x), bh, ul, kh, vs)
    st = [st_ref[gi] for gi in G]
    ys = each(lambda q_, s_, y_: _mm(q_, s_) + y_, qe, st, yl)
    st_new = each(lambda g_, s_, h_: _mm(g_, s_) + h_, gm, st, hm)
    for gi in G:
        st_ref[gi] = st_new[gi]
        sl = sls[gi]
        y = ys[gi][0:L]
        for hh in range(1, RW_GROUP):
            y = y + ys[gi][hh * L:(hh + 1) * L]
        mean = gsum(y) * (1.0 / N)
        d = y - mean
        var = gsum(d * d) * (1.0 / N)
        yn = d * lax.rsqrt(var + LNX_EPS) * lw_ref[:, sl] + lb_ref[:, sl]
        yn = yn + gsum(r[gi] * k2[gi] * rk_ref[:, sl]) * v[gi]
        o_ref[:, sl] = (yn * g_ref[:, sl]).astype(o_ref.dtype)


def _rwkv(r, ld, k, v, a, g, k_k, k_a, r_k, lnx_w, lnx_b, B, T, *, ng=4, L=64):
    M = r.shape[0]
    L = min(L, T)
    nc = T // L
    W = ng * RW_GW
    blk = pl.BlockSpec((L, W), lambda b, hg, c: (b * nc + c, hg))
    par = pl.BlockSpec((1, W), lambda b, hg, c: (0, hg))
    row = lambda p: p.astype(F32).reshape(1, RW_WIDTH)
    return pl.pallas_call(
        functools.partial(_rwkv_body, ng=ng, L=L),
        grid=(B, RW_WIDTH // W, nc),
        in_specs=[blk] * 6 + [par] * 5,
        out_specs=blk,
        out_shape=jax.ShapeDtypeStruct((M, RW_WIDTH), BF16),
        scratch_shapes=[pltpu.VMEM((ng, RW_GW, RW_GW), F32)],
        compiler_params=_cparams(("parallel", "parallel", "arbitrary")),
        name="rwkv7_chunk",
    )(r, ld, k, v, a, g, row(k_k), row(k_a), row(r_k), row(lnx_w), row(lnx_b))


def _dsa_body(qd_ref, qi_ref, wi_ref, kd_ref, vd_ref, ki_ref, o_ref,
              keys_ref, bias_ref, qim_ref, cut_ref, m_ref, l_ref, acc_ref, *, tq, tk, ksel, T, hg):
    i = pl.program_id(1)
    nkt = (i * tq + tq + tk - 1) // tk
    krow = lax.broadcasted_iota(I32, (tk, tq), 0)
    qpos = i * tq + lax.broadcasted_iota(I32, (tk, tq), 1)
    low_half = lax.broadcasted_iota(I32, (tq, LANES), 1) < IDX_DIM
    for h in range(IDX_HEADS):
        qt = qi_ref[:, (h // 2) * LANES:(h // 2 + 1) * LANES]
        qim_ref[h] = jnp.where(low_half if h % 2 == 0 else jnp.logical_not(low_half), qt,
                               jnp.zeros_like(qt)).astype(qim_ref.dtype)
    wit = wi_ref[...].T

    def score_tile(j, carry):
        off = pl.multiple_of(j * tk, tk)
        kt = ki_ref[pl.ds(off, tk), :]
        acc = jnp.zeros((tk, tq), F32)
        for h in range(IDX_HEADS):
            acc = acc + jnp.maximum(_dot_nt(kt, qim_ref[h]), 0.0) * wit[h:h + 1, :]
        acc = acc + 0.0
        sc = jnp.where((krow + off) <= qpos, acc, -jnp.inf)
        bits = pltpu.bitcast(sc, I32)
        keys_ref[pl.ds(off, tk), :] = bits ^ ((bits >> 31) & 0x7FFFFFFF)
        return carry

    lax.fori_loop(0, nkt, score_tile, 0)

    def count(pred):
        def body(j, c):
            off = pl.multiple_of(j * tk, tk)
            hit = jnp.where(pred(keys_ref[pl.ds(off, tk), :], krow + off), 1.0, 0.0)
            return c + jnp.sum(hit.reshape(tk // SUBLANES, SUBLANES, tq), axis=0)
        c = lax.fori_loop(0, nkt, body, jnp.zeros((SUBLANES, tq), F32))
        return jnp.sum(c, axis=0, keepdims=True)

    def bit_step(b, thr):
        cand = thr + jnp.left_shift(jnp.int32(1), 31 - b)
        cnt = count(lambda key, _: key >= cand)
        return jnp.where(cnt >= ksel, cand, thr)

    thr = lax.fori_loop(0, 32, bit_step, jnp.full((1, tq), INT_MIN, I32))

    n_gt = count(lambda key, _: key > thr)
    n_ge = count(lambda key, _: key >= thr)
    need = (n_ge > ksel) & (thr > NEG_INF_KEY)
    quota = ksel - n_gt
    cut_ref[...] = jnp.full((1, tq), T, I32)

    @pl.when(jnp.max(jnp.where(need, 1.0, 0.0)) > 0.0)
    def _ties():
        def pos_step(b, p):
            cand = p + jnp.left_shift(jnp.int32(1), (T.bit_length() - 1) - b)
            cnt = count(lambda key, pos: (key == thr) & (pos < cand))
            return jnp.where(cnt < quota, cand, p)
        p = lax.fori_loop(0, T.bit_length(), pos_step, jnp.zeros((1, tq), I32))
        cut_ref[...] = jnp.where(need, p, T)

    cut = cut_ref[...]
    thr_sel = jnp.maximum(thr, NEG_INF_KEY + 1)

    def bias_tile(j, carry):
        off = pl.multiple_of(j * tk, tk)
        key = keys_ref[pl.ds(off, tk), :]
        sel = (key > thr_sel) | ((key == thr_sel) & ((krow + off) <= cut))
        bias_ref[pl.ds(off, tk), :] = jnp.where(sel, 0.0, -jnp.inf)
        return carry

    lax.fori_loop(0, nkt, bias_tile, 0)

    m_ref[...] = jnp.full(m_ref.shape, -jnp.inf, F32)
    l_ref[...] = jnp.zeros(l_ref.shape, F32)
    acc_ref[...] = jnp.zeros(acc_ref.shape, F32)

    def attend(j, carry):
        off = pl.multiple_of(j * tk, tk)
        ks = kd_ref[pl.ds(off, tk), :]
        vt = vd_ref[pl.ds(off, tk), :].T
        bias = bias_ref[pl.ds(off, tk), :]
        for h0 in range(0, SA_HEADS, hg):
            heads = list(range(h0, h0 + hg))
            ss = [_dot_nt(ks, qd_ref[:, h * LANES:(h + 1) * LANES]) + bias for h in heads]
            _softmax_steps(ss, vt, m_ref, l_ref, acc_ref, heads, guard=True)
        return carry

    lax.fori_loop(0, nkt, attend, 0)
    for h in range(SA_HEADS):
        o_ref[:, h * LANES:(h + 1) * LANES] = (acc_ref[h] * (1.0 / l_ref[h])).T.astype(o_ref.dtype)


def _dsa(qd, qi, wi, kd, vd, ki, B, T, ksel, *, tq=512, tk=512, hg=8):
    M = qd.shape[0]
    tq, tk = min(tq, T), min(tk, T)
    nq = T // tq
    qblk = lambda w: pl.BlockSpec((tq, w), lambda b, i: (b * nq + i, 0))
    kblk = pl.BlockSpec((T, LANES), lambda b, i: (b, 0))
    return pl.pallas_call(
        functools.partial(_dsa_body, tq=tq, tk=tk, ksel=ksel, T=T, hg=hg),
        grid=(B, nq),
        in_specs=[qblk(SA_WIDTH), qblk(IDX_HEADS * IDX_DIM), qblk(LANES), kblk, kblk, kblk],
        out_specs=qblk(SA_WIDTH),
        out_shape=jax.ShapeDtypeStruct((M, SA_WIDTH), BF16),
        scratch_shapes=[pltpu.VMEM((T, tq), I32), pltpu.VMEM((T, tq), F32),
                        pltpu.VMEM((IDX_HEADS, tq, LANES), BF16), pltpu.VMEM((1, tq), I32),
                        pltpu.VMEM((SA_HEADS, 1, tq), F32), pltpu.VMEM((SA_HEADS, 1, tq), F32),
                        pltpu.VMEM((SA_HEADS, LANES, tq), F32)],
        compiler_params=_cparams(("parallel", "arbitrary")),
        name="dsa_attention",
    )(qd, qi, wi, kd, vd, ki)


def _even_mixer(xf, h, w_in, w_out, e, q_norm, k_norm, lam_p, subln, conv_w, tabs64, lam_init, B, T):
    z = _matmul(h, _cast_pad(w_in, e), name="even_in")
    qh = _segment(z, 0, DA_WIDTH, q_norm, tabs64, T, gs=64, do_norm=True, do_rope=True,
                  scale=DA_HEAD_DIM ** -0.5 * LOG2E)
    kh = _segment(z, DA_WIDTH, DA_WIDTH, k_norm, tabs64, T, gs=64, do_norm=True, do_rope=True)
    vh = _segment(z, 2 * DA_WIDTH, DA_WIDTH, None, tabs64, T, gs=64, do_norm=False, do_rope=False)
    o = _diff_attention(qh, kh, vh, lam_p, subln, lam_init, B, T)
    y = _short_conv(z, conv_w, B, T)
    return _matmul(o, _cast_pad(w_out, e), a2=y, resid=xf, name="even_out"), z


def _odd_mixer(xf, h, w_in_t, w_out, o, mu_p, w0, w2, a0, a2, v0, v2, g2, k_k, k_a, r_k, lnx_w, lnx_b,
               q_norm, k_norm, idxk_norm, z_first, tabs64, tabs128, B, T, ksel):
    z = _matmul(h, w_in_t, b_transposed=True, name="odd_in")
    r, ld, k, v, a, g = _rwkv_prep(z, z_first, mu_p, w0, a0, v0, w2, a2, v2, g2, T)
    rw_out = _rwkv(r, ld, k, v, a, g, k_k, k_a, r_k.reshape(-1), lnx_w, lnx_b, B, T)
    qd = _segment(z, OD_Q, SA_WIDTH, q_norm, tabs128, T, gs=128, do_norm=True, do_rope=True,
                  scale=SA_HEAD_DIM ** -0.5 * LOG2E)
    kd = _segment(z, OD_KD, LANES, k_norm, tabs128, T, gs=128, do_norm=True, do_rope=True)
    vd = _segment(z, OD_VDD, LANES, None, tabs128, T, gs=128, do_norm=False, do_rope=False)
    qi = _segment(z, OD_QI, IDX_HEADS * IDX_DIM, None, tabs64, T, gs=64, do_norm=False, do_rope=True)
    ki = _segment(z, OD_KI, LANES, idxk_norm, tabs64, T, gs=64, do_norm=True, do_rope=True, pick="dup_low")
    wi = _segment(z, OD_KI, LANES, None, tabs64, T, gs=64, do_norm=False, do_rope=False,
                  scale=IDX_HEADS ** -0.5 * IDX_DIM ** -0.5, out_dtype=F32, pick="high16")
    sa_out = _dsa(qd, qi, wi, kd, vd, ki, B, T, ksel)
    return _matmul(rw_out, _cast_pad(w_out, o), a2=sa_out, resid=xf, name="odd_out")


def kernel(x, mix_norm, ffn_norm, ffn_gate, ffn_up, ffn_down, ev_w_in, ev_w_out, da_q_norm, da_k_norm, da_lambda, da_subln, sc_conv, od_w_in, od_w_out, rw_mu, rw_w0, rw_w2, rw_a0, rw_a2, rw_v0, rw_v2, rw_g2, rw_k_k, rw_k_a, rw_r_k, rw_lnx_w, rw_lnx_b, sa_q_norm, sa_k_norm, idx_k_norm):
    B, T, D = x.shape
    M = B * T
    ksel = min(TOPK_MAX, T // 4)
    xf = x.reshape(M, D)
    tabs64 = _rope_tables(T, 64)
    tabs128 = _rope_tables(T, 128)
    z_first = None
    for i in range(DEPTH):
        h = _rmsnorm(xf, mix_norm[i])
        if i % 2 == 0:
            e = i // 2
            lam_init = 0.8 - 0.6 * math.exp(-0.3 * i)
            xf, z = _even_mixer(xf, h, ev_w_in, ev_w_out, e, da_q_norm[e],
                                da_k_norm[e], da_lambda[e], da_subln[e], sc_conv[e], tabs64, lam_init, B, T)
            if z_first is None:
                z_first = z
        else:
            o = i // 2
            xf = _odd_mixer(xf, h, _cast_pad(jnp.swapaxes(od_w_in, 1, 2), o, rows_p=OD_PAD), od_w_out, o,
                            rw_mu[o].astype(F32).reshape(1, OD_RW_END),
                            rw_w0[o], rw_w2[o], rw_a0[o], rw_a2[o], rw_v0[o], rw_v2[o], rw_g2[o],
                            rw_k_k[o], rw_k_a[o], rw_r_k[o], rw_lnx_w[o], rw_lnx_b[o],
                            sa_q_norm[o], sa_k_norm[o], idx_k_norm[o], z_first, tabs64, tabs128, B, T, ksel)
        h = _rmsnorm(xf, ffn_norm[i])
        hid = _swiglu_proj(h, ffn_gate, ffn_up, i)
        wd = _cast_pad(ffn_down, i, tr=256, tc=2048)
        xf = _matmul(hid, wd, resid=xf, tm=512, tn=256, name="ffn_out")
    return xf.reshape(B, T, D)
```

```python
import functools
import math

import jax
import jax.numpy as jnp
from jax import lax
from jax.experimental import pallas as pl
from jax.experimental.pallas import tpu as pltpu

F32 = jnp.float32
BF16 = jnp.bfloat16
I32 = jnp.int32

D_MODEL = 4096
DEPTH = 4
DA_WIDTH = 2048
DA_HEADS = 16
DA_HEAD_DIM = 64
SC_WIDTH = 2048
CONV_W = 3
RW_WIDTH = 2048
RW_HEAD_DIM = 64
RW_HEADS = 32
RW_GROUP = 4
RW_GW = RW_GROUP * RW_HEAD_DIM
W_LORA, A_LORA, V_LORA, G_LORA = 96, 96, 64, 256
LNX_EPS = 64e-5
SA_WIDTH = 2048
SA_HEAD_DIM = 128
SA_HEADS = 16
IDX_HEADS = 16
IDX_DIM = 64
TOPK_MAX = 256
FFN_HIDDEN = 11008
ROPE_THETA = 10000.0
EPS = 1e-6

LANES = 128
SUBLANES = 8
V7X_VMEM_BYTES = 64 * 1024 * 1024
VMEM_LIMIT = (V7X_VMEM_BYTES * 3) // 4

OD_R, OD_K, OD_V = 0, 2048, 4096
OD_LORA = 3 * RW_WIDTH
LORA_W = W_LORA + A_LORA + V_LORA + G_LORA
OD_RW_END = OD_LORA + LORA_W
OD_Q = OD_RW_END
OD_KD = OD_Q + SA_WIDTH
OD_VDD = OD_KD + SA_HEAD_DIM
OD_QI = OD_VDD + SA_HEAD_DIM
OD_KI = OD_QI + IDX_HEADS * IDX_DIM
OD_PAD = 10240
LOG2E = math.log2(math.e)
INT_MIN = -2 ** 31
NEG_INF_KEY = -2139095041


def _cparams(sem):
    return pltpu.CompilerParams(dimension_semantics=sem, vmem_limit_bytes=VMEM_LIMIT)


def _mm_body(*refs, k1, bt, has_resid):
    refs = list(refs)
    a_ref = refs.pop(0)
    a2_ref = refs.pop(0) if k1 else None
    b_ref = refs.pop(0)
    r_ref = refs.pop(0) if has_resid else None
    o_ref = refs.pop(0)
    if bt:
        acc = _dot_nt(a_ref[...], b_ref[...])
    elif k1:
        acc = (jnp.dot(a_ref[...], b_ref[0:k1, :], preferred_element_type=F32)
               + jnp.dot(a2_ref[...], b_ref[k1:, :], preferred_element_type=F32))
    else:
        acc = jnp.dot(a_ref[...], b_ref[...], preferred_element_type=F32)
    o_ref[...] = (r_ref[...] + acc) if has_resid else acc.astype(o_ref.dtype)


def _matmul(a, b, *, a2=None, resid=None, b_transposed=False, tm=1024, tn=512, name="mm"):
    M, k1 = a.shape
    K = k1 + (a2.shape[1] if a2 is not None else 0)
    N = b.shape[0] if b_transposed else b.shape[1]
    assert (b.shape[1] if b_transposed else b.shape[0]) == K and not (b_transposed and a2 is not None)
    tm, tn = min(tm, M), min(tn, N)
    assert M % tm == 0 and N % tn == 0, (a.shape, b.shape, tm, tn)
    in_specs = [pl.BlockSpec((tm, k1), lambda i, j: (i, 0))]
    args = [a]
    if a2 is not None:
        in_specs.append(pl.BlockSpec((tm, K - k1), lambda i, j: (i, 0)))
        args.append(a2)
    in_specs.append(pl.BlockSpec((tn, K), lambda i, j: (j, 0)) if b_transposed
                    else pl.BlockSpec((K, tn), lambda i, j: (0, j)))
    args.append(b)
    if resid is not None:
        in_specs.append(pl.BlockSpec((tm, tn), lambda i, j: (i, j)))
        args.append(resid)
    return pl.pallas_call(
        functools.partial(_mm_body, k1=k1 if a2 is not None else 0, bt=b_transposed, has_resid=resid is not None),
        grid=(M // tm, N // tn),
        in_specs=in_specs,
        out_specs=pl.BlockSpec((tm, tn), lambda i, j: (i, j)),
        out_shape=jax.ShapeDtypeStruct((M, N), F32),
        compiler_params=_cparams(("parallel", "arbitrary")),
        name=name,
    )(*args)


def _swiglu_body(a_ref, wg_ref, wu_ref, o_ref, gb_ref, ub_ref):
    @pl.when(pl.program_id(1) == 0)
    def _cast():
        gb_ref[...] = wg_ref[...].astype(BF16)
        ub_ref[...] = wu_ref[...].astype(BF16)

    a = a_ref[...]
    g = jnp.dot(a, gb_ref[...], preferred_element_type=F32)
    u = jnp.dot(a, ub_ref[...], preferred_element_type=F32)
    o_ref[...] = (g * (1.0 / (1.0 + jnp.exp(-g))) * u).astype(o_ref.dtype)


def _swiglu_proj(a, w_gate, w_up, layer, *, tm=1024, tn=256, name="ffn_in"):
    M, K = a.shape
    N = w_gate.shape[2]
    tm, tn = min(tm, M), min(tn, N)
    assert M % tm == 0 and N % tn == 0 and w_gate.shape[1] == K, (a.shape, w_gate.shape, tm, tn)
    wspec = pl.BlockSpec((None, K, tn), lambda j, i: (layer, 0, j))
    return pl.pallas_call(
        _swiglu_body,
        grid=(N // tn, M // tm),
        in_specs=[pl.BlockSpec((tm, K), lambda j, i: (i, 0)), wspec, wspec],
        out_specs=pl.BlockSpec((tm, tn), lambda j, i: (i, j)),
        out_shape=jax.ShapeDtypeStruct((M, N), BF16),
        scratch_shapes=[pltpu.VMEM((K, tn), BF16), pltpu.VMEM((K, tn), BF16)],
        compiler_params=_cparams(("parallel", "arbitrary")),
        name=name,
    )(a, w_gate, w_up)


def _cast_body(x_ref, o_ref, *, rows, cols, tr, tc):
    x = x_ref[...]
    r = pl.program_id(0) * tr + lax.broadcasted_iota(I32, x.shape, 0)
    c = pl.program_id(1) * tc + lax.broadcasted_iota(I32, x.shape, 1)
    o_ref[...] = jnp.where((r < rows) & (c < cols), x, 0.0).astype(o_ref.dtype)


def _cast_pad(w, layer, rows_p=None, cols_p=None, *, tr=512, tc=1024):
    _, rows, cols = w.shape
    rows_p, cols_p = rows_p or rows, cols_p or cols
    tr, tc = min(tr, rows_p), min(tc, cols_p)
    assert rows_p % tr == 0 and cols_p % tc == 0, (w.shape, rows_p, cols_p)
    return pl.pallas_call(
        functools.partial(_cast_body, rows=rows, cols=cols, tr=tr, tc=tc),
        grid=(rows_p // tr, cols_p // tc),
        in_specs=[pl.BlockSpec((None, tr, tc), lambda i, j: (layer, i, j))],
        out_specs=pl.BlockSpec((tr, tc), lambda i, j: (i, j)),
        out_shape=jax.ShapeDtypeStruct((rows_p, cols_p), BF16),
        compiler_params=_cparams(("parallel", "parallel")),
        name="cast_pad",
    )(w)


def _rms_body(x_ref, g_ref, o_ref):
    x = x_ref[...]
    ms = jnp.mean(x * x, axis=-1, keepdims=True)
    o_ref[...] = (x * lax.rsqrt(ms + EPS) * g_ref[...]).astype(o_ref.dtype)


def _rmsnorm(x, g, *, tr=256):
    M, D = x.shape
    tr = min(tr, M)
    return pl.pallas_call(
        _rms_body,
        grid=(M // tr,),
        in_specs=[pl.BlockSpec((tr, D), lambda i: (i, 0)),
                  pl.BlockSpec((1, D), lambda i: (0, 0))],
        out_specs=pl.BlockSpec((tr, D), lambda i: (i, 0)),
        out_shape=jax.ShapeDtypeStruct((M, D), BF16),
        compiler_params=_cparams(("parallel",)),
        name="rmsnorm",
    )(x, g.reshape(1, D))


def _rope_tables(T, gs):
    half = gs // 2
    inv = ROPE_THETA ** (-jnp.arange(half, dtype=F32) / half)
    ang = jnp.arange(T, dtype=jnp.int32).astype(F32)[:, None] * inv[None, :]
    cos, sin = jnp.cos(ang), jnp.sin(ang)
    cosg = jnp.concatenate([cos, cos], axis=1)
    sing = jnp.concatenate([-sin, sin], axis=1)
    reps = LANES // gs
    return jnp.tile(cosg, (1, reps)), jnp.tile(sing, (1, reps))


def _group_ones(gs):
    r = jnp.arange(LANES)
    return (r[:, None] // gs == r[None, :] // gs).astype(BF16)


def _seg_body(x_ref, g_ref, cos_ref, sin_ref, bd_ref, o_ref, *, gs, do_norm, do_rope, scale, pick):
    half = gs // 2
    lane = lax.broadcasted_iota(I32, (x_ref.shape[0], LANES), 1)
    for t in range(x_ref.shape[1] // LANES):
        sl = slice(t * LANES, (t + 1) * LANES)
        x = x_ref[:, sl]
        if pick == "dup_low":
            x = jnp.where(lane < LANES // 2, x, pltpu.roll(x, LANES // 2, 1))
        elif pick == "high16":
            x = jnp.where(lane < IDX_HEADS, pltpu.roll(x, LANES // 2, 1), 0.0)
        if do_norm:
            x2 = x * x
            hi = x2.astype(BF16)
            lo = (x2 - hi.astype(F32)).astype(BF16)
            bd = bd_ref[...]
            ssum = jnp.dot(hi, bd, preferred_element_type=F32) + jnp.dot(lo, bd, preferred_element_type=F32)
            x = x * lax.rsqrt(ssum * (1.0 / gs) + EPS) * g_ref[...]
        if do_rope:
            if gs == LANES:
                rot = pltpu.roll(x, half, 1)
            else:
                rot = jnp.where((lane & (gs - 1)) < half, pltpu.roll(x, LANES - half, 1), pltpu.roll(x, half, 1))
            x = x * cos_ref[...] + rot * sin_ref[...]
        if scale != 1.0:
            x = x * scale
        o_ref[:, sl] = x.astype(o_ref.dtype)


def _segment(z, col_off, width, gain, tables, T, *, gs, do_norm, do_rope, scale=1.0, out_dtype=BF16, pick=None,
             tr=512, max_cw=1024):
    M = z.shape[0]
    tr = min(tr, T)
    nrb = T // tr
    cw = math.gcd(math.gcd(col_off, width), max_cw)
    assert cw % LANES == 0, (col_off, width)
    cb = col_off // cw
    cos, sin = tables
    if gain is None:
        gain = jnp.ones((gs,), F32)
    gt = jnp.tile(gain.astype(F32), LANES // gs).reshape(1, LANES)
    return pl.pallas_call(
        functools.partial(_seg_body, gs=gs, do_norm=do_norm, do_rope=do_rope, scale=scale, pick=pick),
        grid=(M // tr, width // cw),
        in_specs=[pl.BlockSpec((tr, cw), lambda i, j: (i, cb + j)),
                  pl.BlockSpec((1, LANES), lambda i, j: (0, 0)),
                  pl.BlockSpec((tr, LANES), lambda i, j: (i % nrb, 0)),
                  pl.BlockSpec((tr, LANES), lambda i, j: (i % nrb, 0)),
                  pl.BlockSpec((LANES, LANES), lambda i, j: (0, 0))],
        out_specs=pl.BlockSpec((tr, cw), lambda i, j: (i, j)),
        out_shape=jax.ShapeDtypeStruct((M, width), out_dtype),
        compiler_params=_cparams(("parallel", "parallel")),
        name="segment",
    )(z, gt, cos, sin, _group_ones(gs))


def _dot_nt(a, b):
    return lax.dot_general(a, b, (((1,), (1,)), ((), ())), preferred_element_type=F32)


def _softmax_steps(ss, vt, m_ref, l_ref, acc_ref, idxs, guard):
    m_prev = [m_ref[i] for i in idxs]
    m_next = [jnp.maximum(mp, jnp.max(s, axis=0, keepdims=True)) for mp, s in zip(m_prev, ss)]
    m_use = [jnp.where(mn == -jnp.inf, 0.0, mn) for mn in m_next] if guard else m_next
    ps = [jnp.exp2(s - mu) for s, mu in zip(ss, m_use)]
    alpha = [jnp.exp2(mp - mu) for mp, mu in zip(m_prev, m_use)]
    pv = [jnp.dot(vt, p.astype(BF16), preferred_element_type=F32) for p in ps]
    for n, i in enumerate(idxs):
        l_ref[i] = alpha[n] * l_ref[i] + jnp.sum(ps[n], axis=0, keepdims=True)
        acc_ref[i] = alpha[n] * acc_ref[i] + pv[n]
        m_ref[i] = m_next[n]


def _dattn_body(lam_ref, sub_ref, q_ref, k_ref, v_ref, o_ref, m_ref, l_ref, acc_ref, *, tq, tk, nh, lam_init):
    i = pl.program_id(2)
    lane = lax.broadcasted_iota(I32, (tq, LANES), 1)
    qs = []
    for h in range(nh):
        q = q_ref[:, h * LANES:(h + 1) * LANES]
        zero = jnp.zeros_like(q)
        qs += [jnp.where(lane < DA_HEAD_DIM, q, zero), jnp.where(lane >= DA_HEAD_DIM, q, zero)]
    m_ref[...] = jnp.full(m_ref.shape, -jnp.inf, F32)
    l_ref[...] = jnp.zeros(l_ref.shape, F32)
    acc_ref[...] = jnp.zeros(acc_ref.shape, F32)
    reps = tk // LANES

    def tile_step(j, masked):
        off = pl.multiple_of(j * tk, tk)
        ks = [k_ref[pl.ds(off, tk), h * LANES:(h + 1) * LANES] for h in range(nh)]
        vs = [v_ref[pl.ds(off, tk), h * LANES:(h + 1) * LANES] for h in range(nh)]
        if masked:
            row = i * tq + lax.broadcasted_iota(I32, (tq, tk), 0)
            col = off + lax.broadcasted_iota(I32, (tq, tk), 1)
            vis = col <= row
        cs = range(2 * nh)
        s = [_dot_nt(qs[c], ks[c // 2]) for c in cs]
        if masked:
            s = [jnp.where(vis, x, -jnp.inf) for x in s]
        m_prev = [m_ref[c] for c in cs]
        m_next = [jnp.maximum(m_prev[c], jnp.max(s[c], axis=-1, keepdims=True)) for c in cs]
        p = [jnp.exp2(s[c] - jnp.concatenate([m_next[c]] * reps, axis=1)) for c in cs]
        alpha = [jnp.exp2(m_prev[c] - m_next[c]) for c in cs]
        pv = [jnp.dot(p[c].astype(BF16), vs[c // 2], preferred_element_type=F32) for c in cs]
        for c in cs:
            l_ref[c] = alpha[c] * l_ref[c] + jnp.sum(p[c], axis=-1, keepdims=True)
            acc_ref[c] = alpha[c] * acc_ref[c] + pv[c]
            m_ref[c] = m_next[c]

    n_full = (i * tq + 1) // tk
    n_all = (i * tq + tq + tk - 1) // tk

    def full_body(j, carry):
        tile_step(j, False)
        return carry

    def diag_body(j, carry):
        tile_step(j, True)
        return carry

    lax.fori_loop(0, n_full, full_body, 0)
    lax.fori_loop(n_full, n_all, diag_body, 0)

    lp = lam_ref[...]
    lam = (jnp.exp(jnp.sum(lp[0:1] * lp[1:2], axis=-1, keepdims=True))
           - jnp.exp(jnp.sum(lp[2:3] * lp[3:4], axis=-1, keepdims=True)) + lam_init)
    for h in range(nh):
        c = 2 * h
        o = acc_ref[c] * (1.0 / l_ref[c]) - lam * (acc_ref[c + 1] * (1.0 / l_ref[c + 1]))
        ms = jnp.mean(o * o, axis=-1, keepdims=True)
        o = o * lax.rsqrt(ms + EPS) * sub_ref[...] * (1.0 - lam_init)
        o_ref[:, h * LANES:(h + 1) * LANES] = o.astype(o_ref.dtype)


def _diff_attention(qh, kh, vh, lam_p, subln, lam_init, B, T, *, tq=512, tk=512, nh=4):
    M = qh.shape[0]
    tq, tk = min(tq, T), min(tk, T)
    nq = T // tq
    W = nh * LANES
    return pl.pallas_call(
        functools.partial(_dattn_body, tq=tq, tk=tk, nh=nh, lam_init=lam_init),
        grid=(B, DA_HEADS // nh, nq),
        in_specs=[pl.BlockSpec((4, DA_HEAD_DIM), lambda b, h, i: (0, 0)),
                  pl.BlockSpec((1, LANES), lambda b, h, i: (0, 0)),
                  pl.BlockSpec((tq, W), lambda b, h, i: (b * nq + i, h)),
                  pl.BlockSpec((T, W), lambda b, h, i: (b, h)),
                  pl.BlockSpec((T, W), lambda b, h, i: (b, h))],
        out_specs=pl.BlockSpec((tq, W), lambda b, h, i: (b * nq + i, h)),
        out_shape=jax.ShapeDtypeStruct((M, DA_WIDTH), BF16),
        scratch_shapes=[pltpu.VMEM((2 * nh, tq, LANES), F32)] * 3,
        compiler_params=_cparams(("parallel", "parallel", "arbitrary")),
        name="diff_attention",
    )(lam_p.astype(F32), subln.astype(F32).reshape(1, LANES), qh, kh, vh)


def _conv_body(gb_ref, gc_ref, u_ref, w_ref, o_ref, sh_ref, *, T):
    cu = gc_ref[...] * u_ref[...]
    sh_ref[0:SUBLANES, :] = jnp.zeros((SUBLANES, LANES), F32)
    sh_ref[SUBLANES:SUBLANES + T, :] = cu
    w = w_ref[...]
    conv = (sh_ref[SUBLANES - 2:SUBLANES - 2 + T, :] * w[0:1]
            + sh_ref[SUBLANES - 1:SUBLANES - 1 + T, :] * w[1:2]
            + cu * w[2:3])
    o_ref[...] = (gb_ref[...] * conv).astype(o_ref.dtype)


def _short_conv(z, conv_w, B, T):
    M = z.shape[0]
    nc = SC_WIDTH // LANES
    base = 3 * DA_WIDTH // LANES
    return pl.pallas_call(
        functools.partial(_conv_body, T=T),
        grid=(B, nc),
        in_specs=[pl.BlockSpec((T, LANES), lambda b, c: (b, base + c)),
                  pl.BlockSpec((T, LANES), lambda b, c: (b, base + nc + c)),
                  pl.BlockSpec((T, LANES), lambda b, c: (b, base + 2 * nc + c)),
                  pl.BlockSpec((CONV_W, LANES), lambda b, c: (0, c))],
        out_specs=pl.BlockSpec((T, LANES), lambda b, c: (b, c)),
        out_shape=jax.ShapeDtypeStruct((M, SC_WIDTH), BF16),
        scratch_shapes=[pltpu.VMEM((T + SUBLANES, LANES), F32)],
        compiler_params=_cparams(("parallel", "parallel")),
        name="short_conv",
    )(z, z, z, conv_w.astype(F32))


def _sigmoid(x):
    return 1.0 / (1.0 + jnp.exp(-x))


def _rwprep_body(z_ref, zh_ref, vf_ref, mu_ref, w0_ref, a0_ref, v0_ref, w2_ref, a2_ref, v2_ref, g2_ref,
                 r_o, ld_o, k_o, v_o, a_o, g_o, sh_ref, *, tr, nrb):
    i = pl.program_id(0)
    first = (i % nrb) == 0
    sh_ref[SUBLANES - 1:SUBLANES, :] = jnp.where(first, 0.0, zh_ref[SUBLANES - 1:SUBLANES, :])
    sh_ref[SUBLANES:SUBLANES + tr, :] = z_ref[...]

    def shifted(lo, hi):
        zc = z_ref[:, lo:hi]
        zp = sh_ref[SUBLANES - 1:SUBLANES - 1 + tr, lo:hi]
        return zc + (zp - zc) * mu_ref[:, lo:hi]

    r_o[...] = shifted(OD_R, OD_K)
    k_o[...] = shifted(OD_K, OD_V)
    lora = shifted(OD_LORA, OD_RW_END)
    lora_b = lora.astype(BF16)
    lw = w0_ref[...] + jnp.dot(jnp.tanh(lora).astype(BF16), w2_ref[...], preferred_element_type=F32)
    nlw = -lw
    softplus = jnp.maximum(nlw, 0.0) + jnp.log(1.0 + jnp.exp(-jnp.abs(nlw)))
    ld_o[...] = -jnp.exp(-softplus - 0.5)
    a_o[...] = _sigmoid(a0_ref[...] + jnp.dot(lora_b, a2_ref[...], preferred_element_type=F32))
    v = shifted(OD_V, OD_LORA)
    vg = _sigmoid(v0_ref[...] + jnp.dot(lora_b, v2_ref[...], preferred_element_type=F32))
    v_o[...] = v + (vf_ref[...] - v) * vg
    g_o[...] = jnp.dot(_sigmoid(lora).astype(BF16), g2_ref[...], preferred_element_type=F32)


def _place_rows(w, start):
    return jnp.pad(w, ((start, LORA_W - start - w.shape[0]), (0, 0))).astype(BF16)


def _rwkv_prep(z, z_first, mu_p, w0, a0, v0, w2, a2, v2, g2, T, *, tr=128):
    M = z.shape[0]
    tr = min(tr, T)
    nrb = T // tr
    W = OD_RW_END
    hb = tr // SUBLANES
    row = lambda a: a.astype(F32).reshape(1, RW_WIDTH)
    full = lambda shape: pl.BlockSpec(shape, lambda i: (0, 0))
    out = jax.ShapeDtypeStruct((M, RW_WIDTH), F32)
    ospec = pl.BlockSpec((tr, RW_WIDTH), lambda i: (i, 0))
    return pl.pallas_call(
        functools.partial(_rwprep_body, tr=tr, nrb=nrb),
        grid=(M // tr,),
        in_specs=[pl.BlockSpec((tr, W), lambda i: (i, 0)),
                  pl.BlockSpec((SUBLANES, W), lambda i: (jnp.maximum(i * hb - 1, 0), 0)),
                  pl.BlockSpec((tr, RW_WIDTH), lambda i: (i, 2 * DA_WIDTH // RW_WIDTH)),
                  full((1, W)), full((1, RW_WIDTH)), full((1, RW_WIDTH)), full((1, RW_WIDTH)),
                  full((LORA_W, RW_WIDTH)), full((LORA_W, RW_WIDTH)), full((LORA_W, RW_WIDTH)),
                  full((LORA_W, RW_WIDTH))],
        out_specs=[ospec] * 6,
        out_shape=[out] * 6,
        scratch_shapes=[pltpu.VMEM((tr + SUBLANES, W), F32)],
        compiler_params=_cparams(("parallel",)),
        name="rwkv_prep",
    )(z, z, z_first, mu_p, row(w0), row(a0), row(v0),
      _place_rows(w2, 0), _place_rows(a2, W_LORA), _place_rows(v2, W_LORA + A_LORA),
      _place_rows(g2, W_LORA + A_LORA + V_LORA))


def _split3(x):
    hi = x.astype(BF16)
    r1 = x - hi.astype(F32)
    mid = r1.astype(BF16)
    lo = (r1 - mid.astype(F32)).astype(BF16)
    return hi, mid, lo


def _mm(a, b):
    return jnp.dot(a.astype(BF16), b.astype(BF16), preferred_element_type=F32)


def _mm_tn(a, b):
    return jnp.dot(a.T.astype(BF16), b.astype(BF16), preferred_element_type=F32)


def _rwkv_body(r_ref, ld_ref, k_ref, v_ref, a_ref, g_ref, kk_ref, ka_ref, rk_ref, lw_ref, lb_ref,
               o_ref, st_ref, *, ng, L):
    c = pl.program_id(2)

    @pl.when(c == 0)
    def _init():
        st_ref[...] = jnp.zeros_like(st_ref)

    N = RW_HEAD_DIM
    S = RW_GROUP * L
    ri = lax.broadcasted_iota(I32, (S, S), 0)
    ci = lax.broadcasted_iota(I32, (S, S), 1)
    same = (ri // L) == (ci // L)
    incl = same & (ci <= ri)
    strict = same & (ci < ri)
    eye_s = (ci == ri).astype(F32)
    blk16 = (ri // 16) == (ci // 16)
    hmask = (lax.broadcasted_iota(I32, (S, RW_GW), 0) // L) == (lax.broadcasted_iota(I32, (S, RW_GW), 1) // N)
    rl = lax.broadcasted_iota(I32, (L, L), 0)
    cl = lax.broadcasted_iota(I32, (L, L), 1)
    tri = (cl <= rl).astype(BF16)
    rn = lax.broadcasted_iota(I32, (RW_GW, RW_GW), 0)
    cn = lax.broadcasted_iota(I32, (RW_GW, RW_GW), 1)
    eye_g = rn == cn
    ones_g = ((rn // N) == (cn // N)).astype(BF16)

    def gsum(x):
        hi = x.astype(BF16)
        lo = (x - hi.astype(F32)).astype(BF16)
        return jnp.dot(hi, ones_g, preferred_element_type=F32) + jnp.dot(lo, ones_g, preferred_element_type=F32)

    def tile(x):
        return jnp.concatenate([x] * RW_GROUP, axis=0)

    def stack(x):
        return jnp.where(hmask, tile(x), 0.0)

    G = range(ng)
    sls = [slice(gi * RW_GW, (gi + 1) * RW_GW) for gi in G]
    each = lambda f, *lists: [f(*xs) for xs in zip(*lists)]
    r = [r_ref[:, sl] for sl in sls]
    ld = [ld_ref[:, sl] for sl in sls]
    k = [k_ref[:, sl] for sl in sls]
    v = [v_ref[:, sl] for sl in sls]
    a = [a_ref[:, sl] for sl in sls]
    kk = [k[gi] * kk_ref[:, sls[gi]] for gi in G]
    kk = each(lambda x: x / jnp.maximum(jnp.sqrt(gsum(x * x)), 1e-12), kk)
    k2 = [k[gi] * (1.0 + (a[gi] - 1.0) * ka_ref[:, sls[gi]]) for gi in G]
    bv = each(lambda x, y: x * y, kk, a)
    parts = each(_split3, ld)
    cum = each(lambda p: (jnp.dot(tri, p[0], preferred_element_type=F32) + jnp.dot(tri, p[1], preferred_element_type=F32)
                          + jnp.dot(tri, p[2], preferred_element_type=F32)), parts)
    clast = each(lambda c_: c_[L - 1:L, :], cum)
    e_neg = each(lambda c_: jnp.exp(-c_), cum)
    e_l = each(lambda cl_, c_: jnp.exp(cl_ - c_), clast, cum)
    p_l = each(jnp.exp, clast)
    at = each(lambda x, c_, l_: stack(-x * jnp.exp(c_ - l_)), kk, cum, ld)
    rt = each(lambda x, c_: stack(x * jnp.exp(c_)), r, cum)
    vs = each(stack, v)
    bh = each(lambda x, e: stack(x * e), bv, e_l)
    kh = each(lambda x, e: stack(x * e), k2, e_l)
    btb = each(lambda x, e: tile(x * e).astype(BF16), bv, e_neg)
    ktb = each(lambda x, e: tile(x * e).astype(BF16), k2, e_neg)
    atb = each(lambda x: x.astype(BF16), at)
    rtb = each(lambda x: x.astype(BF16), rt)
    mab = each(lambda x, y: jnp.where(strict, _dot_nt(x, y), 0.0), atb, btb)
    mak = each(lambda x, y: jnp.where(strict, _dot_nt(x, y), 0.0), atb, ktb)
    mrb = each(lambda x, y: jnp.where(incl, _dot_nt(x, y), 0.0), rtb, btb)
    mrk = each(lambda x, y: jnp.where(incl, _dot_nt(x, y), 0.0), rtb, ktb)
    nd = each(lambda m: jnp.where(blk16, m, 0.0), mab)
    n2 = each(_mm, nd, nd)
    mv = each(_mm, mak, vs)
    n4 = each(_mm, n2, n2)
    t = each(lambda n, n2_: eye_s + n + _mm(eye_s + n, n2_), nd, n2)
    n8 = each(_mm, n4, n4)
    t = each(lambda t_, n: t_ + _mm(t_, n), t, n4)
    t = each(lambda t_, n: t_ + _mm(t_, n), t, n8)
    size = 16
    while size < L:
        off = ((ri // size) == (ci // size) + 1) & ((ri // (2 * size)) == (ci // (2 * size)))
        u = each(lambda t_, m: _mm(t_, jnp.where(off, m, 0.0)), t, mab)
        t = each(lambda t_, u_: t_ + _mm(u_, t_), t, u)
        size *= 2
    wm = each(_mm, t, at)
    ul = each(_mm, t, mv)
    qe = each(lambda x, m, w: x + _mm(m, w), rt, mrb, wm)
    yl = each(lambda m, u_, m2, x: _mm(m, u_) + _mm(m2, x), mrb, ul, mrk, vs)
    gm = each(lambda b, w, p: _mm_tn(b, w) + jnp.where(eye_g, p, 0.0), bh, wm, p_l)
    hm = each(lambda b, u_, k_, x: _mm_tn(b, u_) + _mm_tn(k_, x), bh, ul, kh, vs)
    st = [st_ref[gi] for gi in G]
    ys = each(lambda q_, s_, y_: _mm(q_, s_) + y_, qe, st, yl)
    st_new = each(lambda g_, s_, h_: _mm(g_, s_) + h_, gm, st, hm)
    for gi in G:
        st_ref[gi] = st_new[gi]
        sl = sls[gi]
        y = ys[gi][0:L]
        for hh in range(1, RW_GROUP):
            y = y + ys[gi][hh * L:(hh + 1) * L]
        mean = gsum(y) * (1.0 / N)
        d = y - mean
        var = gsum(d * d) * (1.0 / N)
        yn = d * lax.rsqrt(var + LNX_EPS) * lw_ref[:, sl] + lb_ref[:, sl]
        yn = yn + gsum(r[gi] * k2[gi] * rk_ref[:, sl]) * v[gi]
        o_ref[:, sl] = (yn * g_ref[:, sl]).astype(o_ref.dtype)


def _rwkv(r, ld, k, v, a, g, k_k, k_a, r_k, lnx_w, lnx_b, B, T, *, ng=4, L=64):
    M = r.shape[0]
    L = min(L, T)
    nc = T // L
    W = ng * RW_GW
    blk = pl.BlockSpec((L, W), lambda b, hg, c: (b * nc + c, hg))
    par = pl.BlockSpec((1, W), lambda b, hg, c: (0, hg))
    row = lambda p: p.astype(F32).reshape(1, RW_WIDTH)
    return pl.pallas_call(
        functools.partial(_rwkv_body, ng=ng, L=L),
        grid=(B, RW_WIDTH // W, nc),
        in_specs=[blk] * 6 + [par] * 5,
        out_specs=blk,
        out_shape=jax.ShapeDtypeStruct((M, RW_WIDTH), BF16),
        scratch_shapes=[pltpu.VMEM((ng, RW_GW, RW_GW), F32)],
        compiler_params=_cparams(("parallel", "parallel", "arbitrary")),
        name="rwkv7_chunk",
    )(r, ld, k, v, a, g, row(k_k), row(k_a), row(r_k), row(lnx_w), row(lnx_b))


def _dsa_body(qd_ref, qi_ref, wi_ref, kd_ref, vd_ref, ki_ref, o_ref,
              keys_ref, bias_ref, qim_ref, cut_ref, m_ref, l_ref, acc_ref, *, tq, tk, ksel, T):
    i = pl.program_id(1)
    nkt = (i * tq + tq + tk - 1) // tk
    krow = lax.broadcasted_iota(I32, (tk, tq), 0)
    qpos = i * tq + lax.broadcasted_iota(I32, (tk, tq), 1)
    low_half = lax.broadcasted_iota(I32, (tq, LANES), 1) < IDX_DIM
    for h in range(IDX_HEADS):
        qt = qi_ref[:, (h // 2) * LANES:(h // 2 + 1) * LANES]
        qim_ref[h] = jnp.where(low_half if h % 2 == 0 else jnp.logical_not(low_half), qt,
                               jnp.zeros_like(qt)).astype(qim_ref.dtype)
    wit = wi_ref[...].T

    def score_tile(j, carry):
        off = pl.multiple_of(j * tk, tk)
        kt = ki_ref[pl.ds(off, tk), :]
        acc = jnp.zeros((tk, tq), F32)
        for h in range(IDX_HEADS):
            acc = acc + jnp.maximum(_dot_nt(kt, qim_ref[h]), 0.0) * wit[h:h + 1, :]
        acc = acc + 0.0
        sc = jnp.where((krow + off) <= qpos, acc, -jnp.inf)
        bits = pltpu.bitcast(sc, I32)
        keys_ref[pl.ds(off, tk), :] = bits ^ ((bits >> 31) & 0x7FFFFFFF)
        return carry

    lax.fori_loop(0, nkt, score_tile, 0)

    def count(pred):
        def body(j, c):
            off = pl.multiple_of(j * tk, tk)
            hit = jnp.where(pred(keys_ref[pl.ds(off, tk), :], krow + off), 1.0, 0.0)
            return c + jnp.sum(hit.reshape(tk // SUBLANES, SUBLANES, tq), axis=0)
        c = lax.fori_loop(0, nkt, body, jnp.zeros((SUBLANES, tq), F32))
        return jnp.sum(c, axis=0, keepdims=True)

    def bit_step(b, thr):
        cand = thr + jnp.left_shift(jnp.int32(1), 31 - b)
        cnt = count(lambda key, _: key >= cand)
        return jnp.where(cnt >= ksel, cand, thr)

    thr = lax.fori_loop(0, 32, bit_step, jnp.full((1, tq), INT_MIN, I32))

    n_gt = count(lambda key, _: key > thr)
    n_ge = count(lambda key, _: key >= thr)
    need = (n_ge > ksel) & (thr > NEG_INF_KEY)
    quota = ksel - n_gt
    cut_ref[...] = jnp.full((1, tq), T, I32)

    @pl.when(jnp.max(jnp.where(need, 1.0, 0.0)) > 0.0)
    def _ties():
        def pos_step(b, p):
            cand = p + jnp.left_shift(jnp.int32(1), (T.bit_length() - 1) - b)
            cnt = count(lambda key, pos: (key == thr) & (pos < cand))
            return jnp.where(cnt < quota, cand, p)
        p = lax.fori_loop(0, T.bit_length(), pos_step, jnp.zeros((1, tq), I32))
        cut_ref[...] = jnp.where(need, p, T)

    cut = cut_ref[...]
    thr_sel = jnp.maximum(thr, NEG_INF_KEY + 1)

    def bias_tile(j, carry):
        off = pl.multiple_of(j * tk, tk)
        key = keys_ref[pl.ds(off, tk), :]
        sel = (key > thr_sel) | ((key == thr_sel) & ((krow + off) <= cut))
        bias_ref[pl.ds(off, tk), :] = jnp.where(sel, 0.0, -jnp.inf)
        return carry

    lax.fori_loop(0, nkt, bias_tile, 0)

    m_ref[...] = jnp.full(m_ref.shape, -jnp.inf, F32)
    l_ref[...] = jnp.zeros(l_ref.shape, F32)
    acc_ref[...] = jnp.zeros(acc_ref.shape, F32)

    def attend(j, carry):
        off = pl.multiple_of(j * tk, tk)
        ks = kd_ref[pl.ds(off, tk), :]
        vt = vd_ref[pl.ds(off, tk), :].T
        bias = bias_ref[pl.ds(off, tk), :]
        heads = list(range(SA_HEADS))
        ss = [_dot_nt(ks, qd_ref[:, h * LANES:(h + 1) * LANES]) + bias for h in heads]
        _softmax_steps(ss, vt, m_ref, l_ref, acc_ref, heads, guard=True)
        return carry

    lax.fori_loop(0, nkt, attend, 0)
    for h in range(SA_HEADS):
        o_ref[:, h * LANES:(h + 1) * LANES] = (acc_ref[h] * (1.0 / l_ref[h])).T.astype(o_ref.dtype)


def _dsa(qd, qi, wi, kd, vd, ki, B, T, ksel, *, tq=512, tk=512):
    M = qd.shape[0]
    tq, tk = min(tq, T), min(tk, T)
    nq = T // tq
    qblk = lambda w: pl.BlockSpec((tq, w), lambda b, i: (b * nq + i, 0))
    kblk = pl.BlockSpec((T, LANES), lambda b, i: (b, 0))
    return pl.pallas_call(
        functools.partial(_dsa_body, tq=tq, tk=tk, ksel=ksel, T=T),
        grid=(B, nq),
        in_specs=[qblk(SA_WIDTH), qblk(IDX_HEADS * IDX_DIM), qblk(LANES), kblk, kblk, kblk],
        out_specs=qblk(SA_WIDTH),
        out_shape=jax.ShapeDtypeStruct((M, SA_WIDTH), BF16),
        scratch_shapes=[pltpu.VMEM((T, tq), I32), pltpu.VMEM((T, tq), F32),
                        pltpu.VMEM((IDX_HEADS, tq, LANES), BF16), pltpu.VMEM((1, tq), I32),
                        pltpu.VMEM((SA_HEADS, 1, tq), F32), pltpu.VMEM((SA_HEADS, 1, tq), F32),
                        pltpu.VMEM((SA_HEADS, LANES, tq), F32)],
        compiler_params=_cparams(("parallel", "arbitrary")),
        name="dsa_attention",
    )(qd, qi, wi, kd, vd, ki)


def _even_mixer(xf, h, w_in, w_out, e, q_norm, k_norm, lam_p, subln, conv_w, tabs64, lam_init, B, T):
    z = _matmul(h, _cast_pad(w_in, e), name="even_in")
    qh = _segment(z, 0, DA_WIDTH, q_norm, tabs64, T, gs=64, do_norm=True, do_rope=True,
                  scale=DA_HEAD_DIM ** -0.5 * LOG2E)
    kh = _segment(z, DA_WIDTH, DA_WIDTH, k_norm, tabs64, T, gs=64, do_norm=True, do_rope=True)
    vh = _segment(z, 2 * DA_WIDTH, DA_WIDTH, None, tabs64, T, gs=64, do_norm=False, do_rope=False)
    o = _diff_attention(qh, kh, vh, lam_p, subln, lam_init, B, T)
    y = _short_conv(z, conv_w, B, T)
    return _matmul(o, _cast_pad(w_out, e), a2=y, resid=xf, name="even_out"), z


def _odd_mixer(xf, h, w_in_t, w_out, o, mu_p, w0, w2, a0, a2, v0, v2, g2, k_k, k_a, r_k, lnx_w, lnx_b,
               q_norm, k_norm, idxk_norm, z_first, tabs64, tabs128, B, T, ksel):
    z = _matmul(h, w_in_t, b_transposed=True, name="odd_in")
    r, ld, k, v, a, g = _rwkv_prep(z, z_first, mu_p, w0, a0, v0, w2, a2, v2, g2, T)
    rw_out = _rwkv(r, ld, k, v, a, g, k_k, k_a, r_k.reshape(-1), lnx_w, lnx_b, B, T)
    qd = _segment(z, OD_Q, SA_WIDTH, q_norm, tabs128, T, gs=128, do_norm=True, do_rope=True,
                  scale=SA_HEAD_DIM ** -0.5 * LOG2E)
    kd = _segment(z, OD_KD, LANES, k_norm, tabs128, T, gs=128, do_norm=True, do_rope=True)
    vd = _segment(z, OD_VDD, LANES, None, tabs128, T, gs=128, do_norm=False, do_rope=False)
    qi = _segment(z, OD_QI, IDX_HEADS * IDX_DIM, None, tabs64, T, gs=64, do_norm=False, do_rope=True)
    ki = _segment(z, OD_KI, LANES, idxk_norm, tabs64, T, gs=64, do_norm=True, do_rope=True, pick="dup_low")
    wi = _segment(z, OD_KI, LANES, None, tabs64, T, gs=64, do_norm=False, do_rope=False,
                  scale=IDX_HEADS ** -0.5 * IDX_DIM ** -0.5, out_dtype=F32, pick="high16")
    sa_out = _dsa(qd, qi, wi, kd, vd, ki, B, T, ksel)
    return _matmul(rw_out, _cast_pad(w_out, o), a2=sa_out, resid=xf, name="odd_out")


def kernel(x, mix_norm, ffn_norm, ffn_gate, ffn_up, ffn_down, ev_w_in, ev_w_out, da_q_norm, da_k_norm, da_lambda, da_subln, sc_conv, od_w_in, od_w_out, rw_mu, rw_w0, rw_w2, rw_a0, rw_a2, rw_v0, rw_v2, rw_g2, rw_k_k, rw_k_a, rw_r_k, rw_lnx_w, rw_lnx_b, sa_q_norm, sa_k_norm, idx_k_norm):
    B, T, D = x.shape
    M = B * T
    ksel = min(TOPK_MAX, T // 4)
    xf = x.reshape(M, D)
    tabs64 = _rope_tables(T, 64)
    tabs128 = _rope_tables(T, 128)
    z_first = None
    for i in range(DEPTH):
        h = _rmsnorm(xf, mix_norm[i])
        if i % 2 == 0:
            e = i // 2
            lam_init = 0.8 - 0.6 * math.exp(-0.3 * i)
            xf, z = _even_mixer(xf, h, ev_w_in, ev_w_out, e, da_q_norm[e],
                                da_k_norm[e], da_lambda[e], da_subln[e], sc_conv[e], tabs64, lam_init, B, T)
            if z_first is None:
                z_first = z
        else:
            o = i // 2
            xf = _odd_mixer(xf, h, _cast_pad(jnp.swapaxes(od_w_in, 1, 2), o, rows_p=OD_PAD), od_w_out, o,
                            rw_mu[o].astype(F32).reshape(1, OD_RW_END),
                            rw_w0[o], rw_w2[o], rw_a0[o], rw_a2[o], rw_v0[o], rw_v2[o], rw_g2[o],
                            rw_k_k[o], rw_k_a[o], rw_r_k[o], rw_lnx_w[o], rw_lnx_b[o],
                            sa_q_norm[o], sa_k_norm[o], idx_k_norm[o], z_first, tabs64, tabs128, B, T, ksel)
        h = _rmsnorm(xf, ffn_norm[i])
        hid = _swiglu_proj(h, ffn_gate, ffn_up, i)
        wd = _cast_pad(ffn_down, i, tr=256, tc=2048)
        xf = _matmul(hid, wd, resid=xf, tm=512, tn=256, name="ffn_out")
    return xf.reshape(B, T, D)
```

```python
import functools
import math

import jax
import jax.numpy as jnp
from jax import lax
from jax.experimental import pallas as pl
from jax.experimental.pallas import tpu as pltpu

F32 = jnp.float32
BF16 = jnp.bfloat16
I32 = jnp.int32

D_MODEL = 4096
DEPTH = 4
DA_WIDTH = 2048
DA_HEADS = 16
DA_HEAD_DIM = 64
SC_WIDTH = 2048
CONV_W = 3
RW_WIDTH = 2048
RW_HEAD_DIM = 64
RW_HEADS = 32
W_LORA, A_LORA, V_LORA, G_LORA = 96, 96, 64, 256
LNX_EPS = 64e-5
SA_WIDTH = 2048
SA_HEAD_DIM = 128
SA_HEADS = 16
IDX_HEADS = 16
IDX_DIM = 64
TOPK_MAX = 256
FFN_HIDDEN = 11008
ROPE_THETA = 10000.0
EPS = 1e-6

LANES = 128
SUBLANES = 8
V7X_VMEM_BYTES = 64 * 1024 * 1024
V7X_MXU_DEPTH = 256
VMEM_LIMIT = (V7X_VMEM_BYTES * 3) // 4

RW_GROUP = V7X_MXU_DEPTH // RW_HEAD_DIM
RW_GW = RW_GROUP * RW_HEAD_DIM
OD_R, OD_K, OD_V = 0, RW_WIDTH, 2 * RW_WIDTH
OD_LORA = 3 * RW_WIDTH
LORA_W = W_LORA + A_LORA + V_LORA + G_LORA
OD_RW_END = OD_LORA + LORA_W
OD_Q = OD_RW_END
OD_KD = OD_Q + SA_WIDTH
OD_VDD = OD_KD + SA_HEAD_DIM
OD_QI = OD_VDD + SA_HEAD_DIM
OD_KI = OD_QI + IDX_HEADS * IDX_DIM
OD_IN = OD_KI + IDX_DIM + IDX_HEADS
OD_PAD = -(-OD_IN // 512) * 512
LOG2E = math.log2(math.e)
INT_MIN = -2 ** 31
NEG_INF_KEY = -2139095041


def _cparams(sem):
    return pltpu.CompilerParams(dimension_semantics=sem, vmem_limit_bytes=VMEM_LIMIT)


def _mm_body(*refs, k1, bt, has_resid):
    refs = list(refs)
    a_ref = refs.pop(0)
    a2_ref = refs.pop(0) if k1 else None
    b_ref = refs.pop(0)
    r_ref = refs.pop(0) if has_resid else None
    o_ref = refs.pop(0)
    if bt:
        acc = _dot_nt(a_ref[...], b_ref[...])
    elif k1:
        acc = (jnp.dot(a_ref[...], b_ref[0:k1, :], preferred_element_type=F32)
               + jnp.dot(a2_ref[...], b_ref[k1:, :], preferred_element_type=F32))
    else:
        acc = jnp.dot(a_ref[...], b_ref[...], preferred_element_type=F32)
    o_ref[...] = (r_ref[...] + acc) if has_resid else acc.astype(o_ref.dtype)


def _matmul(a, b, *, a2=None, resid=None, b_transposed=False, tm=1024, tn=512, name="mm"):
    M, k1 = a.shape
    K = k1 + (a2.shape[1] if a2 is not None else 0)
    N = b.shape[0] if b_transposed else b.shape[1]
    assert (b.shape[1] if b_transposed else b.shape[0]) == K and not (b_transposed and a2 is not None)
    tm, tn = min(tm, M), min(tn, N)
    assert M % tm == 0 and N % tn == 0, (a.shape, b.shape, tm, tn)
    in_specs = [pl.BlockSpec((tm, k1), lambda i, j: (i, 0))]
    args = [a]
    if a2 is not None:
        in_specs.append(pl.BlockSpec((tm, K - k1), lambda i, j: (i, 0)))
        args.append(a2)
    in_specs.append(pl.BlockSpec((tn, K), lambda i, j: (j, 0)) if b_transposed
                    else pl.BlockSpec((K, tn), lambda i, j: (0, j)))
    args.append(b)
    if resid is not None:
        in_specs.append(pl.BlockSpec((tm, tn), lambda i, j: (i, j)))
        args.append(resid)
    return pl.pallas_call(
        functools.partial(_mm_body, k1=k1 if a2 is not None else 0, bt=b_transposed, has_resid=resid is not None),
        grid=(M // tm, N // tn),
        in_specs=in_specs,
        out_specs=pl.BlockSpec((tm, tn), lambda i, j: (i, j)),
        out_shape=jax.ShapeDtypeStruct((M, N), F32),
        compiler_params=_cparams(("parallel", "arbitrary")),
        name=name,
    )(*args)


def _swiglu_body(a_ref, wg_ref, wu_ref, o_ref, gb_ref, ub_ref):
    @pl.when(pl.program_id(1) == 0)
    def _cast():
        gb_ref[...] = wg_ref[...].astype(BF16)
        ub_ref[...] = wu_ref[...].astype(BF16)

    a = a_ref[...]
    g = jnp.dot(a, gb_ref[...], preferred_element_type=F32)
    u = jnp.dot(a, ub_ref[...], preferred_element_type=F32)
    o_ref[...] = (g * (1.0 / (1.0 + jnp.exp(-g))) * u).astype(o_ref.dtype)


def _swiglu_proj(a, w_gate, w_up, layer, *, tm=1024, tn=256, name="ffn_in"):
    M, K = a.shape
    N = w_gate.shape[2]
    tm, tn = min(tm, M), min(tn, N)
    assert M % tm == 0 and N % tn == 0 and w_gate.shape[1] == K, (a.shape, w_gate.shape, tm, tn)
    wspec = pl.BlockSpec((None, K, tn), lambda j, i: (layer, 0, j))
    return pl.pallas_call(
        _swiglu_body,
        grid=(N // tn, M // tm),
        in_specs=[pl.BlockSpec((tm, K), lambda j, i: (i, 0)), wspec, wspec],
        out_specs=pl.BlockSpec((tm, tn), lambda j, i: (i, j)),
        out_shape=jax.ShapeDtypeStruct((M, N), BF16),
        scratch_shapes=[pltpu.VMEM((K, tn), BF16), pltpu.VMEM((K, tn), BF16)],
        compiler_params=_cparams(("parallel", "arbitrary")),
        name=name,
    )(a, w_gate, w_up)


def _cast_body(x_ref, o_ref, *, rows, cols, tr, tc):
    x = x_ref[...]
    r = pl.program_id(0) * tr + lax.broadcasted_iota(I32, x.shape, 0)
    c = pl.program_id(1) * tc + lax.broadcasted_iota(I32, x.shape, 1)
    o_ref[...] = jnp.where((r < rows) & (c < cols), x, 0.0).astype(o_ref.dtype)


def _cast_pad(w, layer, rows_p=None, cols_p=None, *, tr=512, tc=1024):
    _, rows, cols = w.shape
    rows_p, cols_p = rows_p or rows, cols_p or cols
    tr, tc = min(tr, rows_p), min(tc, cols_p)
    assert rows_p % tr == 0 and cols_p % tc == 0, (w.shape, rows_p, cols_p)
    return pl.pallas_call(
        functools.partial(_cast_body, rows=rows, cols=cols, tr=tr, tc=tc),
        grid=(rows_p // tr, cols_p // tc),
        in_specs=[pl.BlockSpec((None, tr, tc), lambda i, j: (layer, i, j))],
        out_specs=pl.BlockSpec((tr, tc), lambda i, j: (i, j)),
        out_shape=jax.ShapeDtypeStruct((rows_p, cols_p), BF16),
        compiler_params=_cparams(("parallel", "parallel")),
        name="cast_pad",
    )(w)


def _rms_body(x_ref, g_ref, o_ref):
    x = x_ref[...]
    ms = jnp.mean(x * x, axis=-1, keepdims=True)
    o_ref[...] = (x * lax.rsqrt(ms + EPS) * g_ref[...]).astype(o_ref.dtype)


def _rmsnorm(x, g, *, tr=256):
    M, D = x.shape
    tr = min(tr, M)
    return pl.pallas_call(
        _rms_body,
        grid=(M // tr,),
        in_specs=[pl.BlockSpec((tr, D), lambda i: (i, 0)),
                  pl.BlockSpec((1, D), lambda i: (0, 0))],
        out_specs=pl.BlockSpec((tr, D), lambda i: (i, 0)),
        out_shape=jax.ShapeDtypeStruct((M, D), BF16),
        compiler_params=_cparams(("parallel",)),
        name="rmsnorm",
    )(x, g.reshape(1, D))


def _rope_tables(T, gs):
    half = gs // 2
    inv = ROPE_THETA ** (-jnp.arange(half, dtype=F32) / half)
    ang = jnp.arange(T, dtype=jnp.int32).astype(F32)[:, None] * inv[None, :]
    cos, sin = jnp.cos(ang), jnp.sin(ang)
    cosg = jnp.concatenate([cos, cos], axis=1)
    sing = jnp.concatenate([-sin, sin], axis=1)
    reps = LANES // gs
    return jnp.tile(cosg, (1, reps)), jnp.tile(sing, (1, reps))


def _group_ones(gs):
    r = jnp.arange(LANES)
    return (r[:, None] // gs == r[None, :] // gs).astype(BF16)


def _seg_body(x_ref, g_ref, cos_ref, sin_ref, bd_ref, o_ref, *, gs, do_norm, do_rope, scale, pick):
    half = gs // 2
    lane = lax.broadcasted_iota(I32, (x_ref.shape[0], LANES), 1)
    for t in range(x_ref.shape[1] // LANES):
        sl = slice(t * LANES, (t + 1) * LANES)
        x = x_ref[:, sl]
        if pick == "dup_low":
            x = jnp.where(lane < LANES // 2, x, pltpu.roll(x, LANES // 2, 1))
        elif pick == "high16":
            x = jnp.where(lane < IDX_HEADS, pltpu.roll(x, LANES // 2, 1), 0.0)
        if do_norm:
            x2 = x * x
            hi = x2.astype(BF16)
            lo = (x2 - hi.astype(F32)).astype(BF16)
            bd = bd_ref[...]
            ssum = jnp.dot(hi, bd, preferred_element_type=F32) + jnp.dot(lo, bd, preferred_element_type=F32)
            x = x * lax.rsqrt(ssum * (1.0 / gs) + EPS) * g_ref[...]
        if do_rope:
            if gs == LANES:
                rot = pltpu.roll(x, half, 1)
            else:
                rot = jnp.where((lane & (gs - 1)) < half, pltpu.roll(x, LANES - half, 1), pltpu.roll(x, half, 1))
            x = x * cos_ref[...] + rot * sin_ref[...]
        if scale != 1.0:
            x = x * scale
        o_ref[:, sl] = x.astype(o_ref.dtype)


def _segment(z, col_off, width, gain, tables, T, *, gs, do_norm, do_rope, scale=1.0, out_dtype=BF16, pick=None,
             tr=512, max_cw=1024):
    M = z.shape[0]
    tr = min(tr, T)
    nrb = T // tr
    cw = math.gcd(math.gcd(col_off, width), max_cw)
    assert cw % LANES == 0, (col_off, width)
    cb = col_off // cw
    cos, sin = tables
    if gain is None:
        gain = jnp.ones((gs,), F32)
    gt = jnp.tile(gain.astype(F32), LANES // gs).reshape(1, LANES)
    return pl.pallas_call(
        functools.partial(_seg_body, gs=gs, do_norm=do_norm, do_rope=do_rope, scale=scale, pick=pick),
        grid=(M // tr, width // cw),
        in_specs=[pl.BlockSpec((tr, cw), lambda i, j: (i, cb + j)),
                  pl.BlockSpec((1, LANES), lambda i, j: (0, 0)),
                  pl.BlockSpec((tr, LANES), lambda i, j: (i % nrb, 0)),
                  pl.BlockSpec((tr, LANES), lambda i, j: (i % nrb, 0)),
                  pl.BlockSpec((LANES, LANES), lambda i, j: (0, 0))],
        out_specs=pl.BlockSpec((tr, cw), lambda i, j: (i, j)),
        out_shape=jax.ShapeDtypeStruct((M, width), out_dtype),
        compiler_params=_cparams(("parallel", "parallel")),
        name="segment",
    )(z, gt, cos, sin, _group_ones(gs))


def _dot_nt(a, b):
    return lax.dot_general(a, b, (((1,), (1,)), ((), ())), preferred_element_type=F32)


def _softmax_steps(ss, vt, m_ref, l_ref, acc_ref, idxs, guard):
    m_prev = [m_ref[i] for i in idxs]
    m_next = [jnp.maximum(mp, jnp.max(s, axis=0, keepdims=True)) for mp, s in zip(m_prev, ss)]
    m_use = [jnp.where(mn == -jnp.inf, 0.0, mn) for mn in m_next] if guard else m_next
    ps = [jnp.exp2(s - mu) for s, mu in zip(ss, m_use)]
    alpha = [jnp.exp2(mp - mu) for mp, mu in zip(m_prev, m_use)]
    pv = [jnp.dot(vt, p.astype(BF16), preferred_element_type=F32) for p in ps]
    for n, i in enumerate(idxs):
        l_ref[i] = alpha[n] * l_ref[i] + jnp.sum(ps[n], axis=0, keepdims=True)
        acc_ref[i] = alpha[n] * acc_ref[i] + pv[n]
        m_ref[i] = m_next[n]


def _dattn_body(lam_ref, sub_ref, q_ref, k_ref, v_ref, o_ref, m_ref, l_ref, acc_ref, *, tq, tk, nh, lam_init):
    i = pl.program_id(2)
    lane = lax.broadcasted_iota(I32, (tq, LANES), 1)
    qs = []
    for h in range(nh):
        q = q_ref[:, h * LANES:(h + 1) * LANES]
        zero = jnp.zeros_like(q)
        qs += [jnp.where(lane < DA_HEAD_DIM, q, zero), jnp.where(lane >= DA_HEAD_DIM, q, zero)]
    m_ref[...] = jnp.full(m_ref.shape, -jnp.inf, F32)
    l_ref[...] = jnp.zeros(l_ref.shape, F32)
    acc_ref[...] = jnp.zeros(acc_ref.shape, F32)
    reps = tk // LANES

    def tile_step(j, masked):
        off = pl.multiple_of(j * tk, tk)
        ks = [k_ref[pl.ds(off, tk), h * LANES:(h + 1) * LANES] for h in range(nh)]
        vs = [v_ref[pl.ds(off, tk), h * LANES:(h + 1) * LANES] for h in range(nh)]
        if masked:
            row = i * tq + lax.broadcasted_iota(I32, (tq, tk), 0)
            col = off + lax.broadcasted_iota(I32, (tq, tk), 1)
            vis = col <= row
        cs = range(2 * nh)
        s = [_dot_nt(qs[c], ks[c // 2]) for c in cs]
        if masked:
            s = [jnp.where(vis, x, -jnp.inf) for x in s]
        m_prev = [m_ref[c] for c in cs]
        m_next = [jnp.maximum(m_prev[c], jnp.max(s[c], axis=-1, keepdims=True)) for c in cs]
        p = [jnp.exp2(s[c] - jnp.concatenate([m_next[c]] * reps, axis=1)) for c in cs]
        alpha = [jnp.exp2(m_prev[c] - m_next[c]) for c in cs]
        pv = [jnp.dot(p[c].astype(BF16), vs[c // 2], preferred_element_type=F32) for c in cs]
        for c in cs:
            l_ref[c] = alpha[c] * l_ref[c] + jnp.sum(p[c], axis=-1, keepdims=True)
            acc_ref[c] = alpha[c] * acc_ref[c] + pv[c]
            m_ref[c] = m_next[c]

    n_full = (i * tq + 1) // tk
    n_all = (i * tq + tq + tk - 1) // tk

    def full_body(j, carry):
        tile_step(j, False)
        return carry

    def diag_body(j, carry):
        tile_step(j, True)
        return carry

    lax.fori_loop(0, n_full, full_body, 0)
    lax.fori_loop(n_full, n_all, diag_body, 0)

    lp = lam_ref[...]
    lam = (jnp.exp(jnp.sum(lp[0:1] * lp[1:2], axis=-1, keepdims=True))
           - jnp.exp(jnp.sum(lp[2:3] * lp[3:4], axis=-1, keepdims=True)) + lam_init)
    for h in range(nh):
        c = 2 * h
        o = acc_ref[c] * (1.0 / l_ref[c]) - lam * (acc_ref[c + 1] * (1.0 / l_ref[c + 1]))
        ms = jnp.mean(o * o, axis=-1, keepdims=True)
        o = o * lax.rsqrt(ms + EPS) * sub_ref[...] * (1.0 - lam_init)
        o_ref[:, h * LANES:(h + 1) * LANES] = o.astype(o_ref.dtype)


def _diff_attention(qh, kh, vh, lam_p, subln, lam_init, B, T, *, tq=512, tk=512, nh=4):
    M = qh.shape[0]
    tq, tk = min(tq, T), min(tk, T)
    nq = T // tq
    W = nh * LANES
    return pl.pallas_call(
        functools.partial(_dattn_body, tq=tq, tk=tk, nh=nh, lam_init=lam_init),
        grid=(B, DA_HEADS // nh, nq),
        in_specs=[pl.BlockSpec((4, DA_HEAD_DIM), lambda b, h, i: (0, 0)),
                  pl.BlockSpec((1, LANES), lambda b, h, i: (0, 0)),
                  pl.BlockSpec((tq, W), lambda b, h, i: (b * nq + i, h)),
                  pl.BlockSpec((T, W), lambda b, h, i: (b, h)),
                  pl.BlockSpec((T, W), lambda b, h, i: (b, h))],
        out_specs=pl.BlockSpec((tq, W), lambda b, h, i: (b * nq + i, h)),
        out_shape=jax.ShapeDtypeStruct((M, DA_WIDTH), BF16),
        scratch_shapes=[pltpu.VMEM((2 * nh, tq, LANES), F32)] * 3,
        compiler_params=_cparams(("parallel", "parallel", "arbitrary")),
        name="diff_attention",
    )(lam_p.astype(F32), subln.astype(F32).reshape(1, LANES), qh, kh, vh)


def _conv_body(gb_ref, gc_ref, u_ref, w_ref, o_ref, sh_ref, *, T):
    cu = gc_ref[...] * u_ref[...]
    sh_ref[0:SUBLANES, :] = jnp.zeros((SUBLANES, LANES), F32)
    sh_ref[SUBLANES:SUBLANES + T, :] = cu
    w = w_ref[...]
    conv = (sh_ref[SUBLANES - 2:SUBLANES - 2 + T, :] * w[0:1]
            + sh_ref[SUBLANES - 1:SUBLANES - 1 + T, :] * w[1:2]
            + cu * w[2:3])
    o_ref[...] = (gb_ref[...] * conv).astype(o_ref.dtype)


def _short_conv(z, conv_w, B, T):
    M = z.shape[0]
    nc = SC_WIDTH // LANES
    base = 3 * DA_WIDTH // LANES
    return pl.pallas_call(
        functools.partial(_conv_body, T=T),
        grid=(B, nc),
        in_specs=[pl.BlockSpec((T, LANES), lambda b, c: (b, base + c)),
                  pl.BlockSpec((T, LANES), lambda b, c: (b, base + nc + c)),
                  pl.BlockSpec((T, LANES), lambda b, c: (b, base + 2 * nc + c)),
                  pl.BlockSpec((CONV_W, LANES), lambda b, c: (0, c))],
        out_specs=pl.BlockSpec((T, LANES), lambda b, c: (b, c)),
        out_shape=jax.ShapeDtypeStruct((M, SC_WIDTH), BF16),
        scratch_shapes=[pltpu.VMEM((T + SUBLANES, LANES), F32)],
        compiler_params=_cparams(("parallel", "parallel")),
        name="short_conv",
    )(z, z, z, conv_w.astype(F32))


def _sigmoid(x):
    return 1.0 / (1.0 + jnp.exp(-x))


def _rwprep_body(z_ref, zh_ref, vf_ref, mu_ref, w0_ref, a0_ref, v0_ref, w2_ref, a2_ref, v2_ref, g2_ref,
                 r_o, ld_o, k_o, v_o, a_o, g_o, sh_ref, *, tr, nrb):
    i = pl.program_id(0)
    first = (i % nrb) == 0
    sh_ref[SUBLANES - 1:SUBLANES, :] = jnp.where(first, 0.0, zh_ref[SUBLANES - 1:SUBLANES, :])
    sh_ref[SUBLANES:SUBLANES + tr, :] = z_ref[...]

    def shifted(lo, hi):
        zc = z_ref[:, lo:hi]
        zp = sh_ref[SUBLANES - 1:SUBLANES - 1 + tr, lo:hi]
        return zc + (zp - zc) * mu_ref[:, lo:hi]

    r_o[...] = shifted(OD_R, OD_K)
    k_o[...] = shifted(OD_K, OD_V)
    lora = shifted(OD_LORA, OD_RW_END)
    lora_b = lora.astype(BF16)
    lw = w0_ref[...] + jnp.dot(jnp.tanh(lora).astype(BF16), w2_ref[...], preferred_element_type=F32)
    nlw = -lw
    softplus = jnp.maximum(nlw, 0.0) + jnp.log(1.0 + jnp.exp(-jnp.abs(nlw)))
    ld_o[...] = -jnp.exp(-softplus - 0.5)
    a_o[...] = _sigmoid(a0_ref[...] + jnp.dot(lora_b, a2_ref[...], preferred_element_type=F32))
    v = shifted(OD_V, OD_LORA)
    vg = _sigmoid(v0_ref[...] + jnp.dot(lora_b, v2_ref[...], preferred_element_type=F32))
    v_o[...] = v + (vf_ref[...] - v) * vg
    g_o[...] = jnp.dot(_sigmoid(lora).astype(BF16), g2_ref[...], preferred_element_type=F32)


def _place_rows(w, start):
    return jnp.pad(w, ((start, LORA_W - start - w.shape[0]), (0, 0))).astype(BF16)


def _rwkv_prep(z, z_first, mu_p, w0, a0, v0, w2, a2, v2, g2, T, *, tr=128):
    M = z.shape[0]
    tr = min(tr, T)
    nrb = T // tr
    W = OD_RW_END
    hb = tr // SUBLANES
    row = lambda a: a.astype(F32).reshape(1, RW_WIDTH)
    full = lambda shape: pl.BlockSpec(shape, lambda i: (0, 0))
    out = jax.ShapeDtypeStruct((M, RW_WIDTH), F32)
    ospec = pl.BlockSpec((tr, RW_WIDTH), lambda i: (i, 0))
    return pl.pallas_call(
        functools.partial(_rwprep_body, tr=tr, nrb=nrb),
        grid=(M // tr,),
        in_specs=[pl.BlockSpec((tr, W), lambda i: (i, 0)),
                  pl.BlockSpec((SUBLANES, W), lambda i: (jnp.maximum(i * hb - 1, 0), 0)),
                  pl.BlockSpec((tr, RW_WIDTH), lambda i: (i, 2 * DA_WIDTH // RW_WIDTH)),
                  full((1, W)), full((1, RW_WIDTH)), full((1, RW_WIDTH)), full((1, RW_WIDTH)),
                  full((LORA_W, RW_WIDTH)), full((LORA_W, RW_WIDTH)), full((LORA_W, RW_WIDTH)),
                  full((LORA_W, RW_WIDTH))],
        out_specs=[ospec] * 6,
        out_shape=[out] * 6,
        scratch_shapes=[pltpu.VMEM((tr + SUBLANES, W), F32)],
        compiler_params=_cparams(("parallel",)),
        name="rwkv_prep",
    )(z, z, z_first, mu_p, row(w0), row(a0), row(v0),
      _place_rows(w2, 0), _place_rows(a2, W_LORA), _place_rows(v2, W_LORA + A_LORA),
      _place_rows(g2, W_LORA + A_LORA + V_LORA))


def _split3(x):
    hi = x.astype(BF16)
    r1 = x - hi.astype(F32)
    mid = r1.astype(BF16)
    lo = (r1 - mid.astype(F32)).astype(BF16)
    return hi, mid, lo


def _mm(a, b):
    return jnp.dot(a.astype(BF16), b.astype(BF16), preferred_element_type=F32)


def _mm_tn(a, b):
    return jnp.dot(a.T.astype(BF16), b.astype(BF16), preferred_element_type=F32)


def _rwkv_body(r_ref, ld_ref, k_ref, v_ref, a_ref, g_ref, kk_ref, ka_ref, rk_ref, lw_ref, lb_ref,
               o_ref, st_ref, *, ng, L):
    c = pl.program_id(2)

    @pl.when(c == 0)
    def _init():
        st_ref[...] = jnp.zeros_like(st_ref)

    N = RW_HEAD_DIM
    S = RW_GROUP * L
    ri = lax.broadcasted_iota(I32, (S, S), 0)
    ci = lax.broadcasted_iota(I32, (S, S), 1)
    same = (ri // L) == (ci // L)
    incl = same & (ci <= ri)
    strict = same & (ci < ri)
    eye_s = (ci == ri).astype(F32)
    blk16 = (ri // 16) == (ci // 16)
    hmask = (lax.broadcasted_iota(I32, (S, RW_GW), 0) // L) == (lax.broadcasted_iota(I32, (S, RW_GW), 1) // N)
    rl = lax.broadcasted_iota(I32, (L, L), 0)
    cl = lax.broadcasted_iota(I32, (L, L), 1)
    tri = (cl <= rl).astype(BF16)
    rn = lax.broadcasted_iota(I32, (RW_GW, RW_GW), 0)
    cn = lax.broadcasted_iota(I32, (RW_GW, RW_GW), 1)
    eye_g = rn == cn
    ones_g = ((rn // N) == (cn // N)).astype(BF16)

    def gsum(x):
        hi = x.astype(BF16)
        lo = (x - hi.astype(F32)).astype(BF16)
        return jnp.dot(hi, ones_g, preferred_element_type=F32) + jnp.dot(lo, ones_g, preferred_element_type=F32)

    def tile(x):
        return jnp.concatenate([x] * RW_GROUP, axis=0)

    def stack(x):
        return jnp.where(hmask, tile(x), 0.0)

    G = range(ng)
    sls = [slice(gi * RW_GW, (gi + 1) * RW_GW) for gi in G]
    each = lambda f, *lists: [f(*xs) for xs in zip(*lists)]
    r = [r_ref[:, sl] for sl in sls]
    ld = [ld_ref[:, sl] for sl in sls]
    k = [k_ref[:, sl] for sl in sls]
    v = [v_ref[:, sl] for sl in sls]
    a = [a_ref[:, sl] for sl in sls]
    kk = [k[gi] * kk_ref[:, sls[gi]] for gi in G]
    kk = each(lambda x: x / jnp.maximum(jnp.sqrt(gsum(x * x)), 1e-12), kk)
    k2 = [k[gi] * (1.0 + (a[gi] - 1.0) * ka_ref[:, sls[gi]]) for gi in G]
    bv = each(lambda x, y: x * y, kk, a)
    parts = each(_split3, ld)
    cum = each(lambda p: (jnp.dot(tri, p[0], preferred_element_type=F32) + jnp.dot(tri, p[1], preferred_element_type=F32)
                          + jnp.dot(tri, p[2], preferred_element_type=F32)), parts)
    clast = each(lambda c_: c_[L - 1:L, :], cum)
    e_neg = each(lambda c_: jnp.exp(-c_), cum)
    e_l = each(lambda cl_, c_: jnp.exp(cl_ - c_), clast, cum)
    p_l = each(jnp.exp, clast)
    at = each(lambda x, c_, l_: stack(-x * jnp.exp(c_ - l_)), kk, cum, ld)
    rt = each(lambda x, c_: stack(x * jnp.exp(c_)), r, cum)
    vs = each(stack, v)
    bh = each(lambda x, e: stack(x * e), bv, e_l)
    kh = each(lambda x, e: stack(x * e), k2, e_l)
    btb = each(lambda x, e: tile(x * e).astype(BF16), bv, e_neg)
    ktb = each(lambda x, e: tile(x * e).astype(BF16), k2, e_neg)
    atb = each(lambda x: x.astype(BF16), at)
    rtb = each(lambda x: x.astype(BF16), rt)
    mab = each(lambda x, y: jnp.where(strict, _dot_nt(x, y), 0.0), atb, btb)
    mak = each(lambda x, y: jnp.where(strict, _dot_nt(x, y), 0.0), atb, ktb)
    mrb = each(lambda x, y: jnp.where(incl, _dot_nt(x, y), 0.0), rtb, btb)
    mrk = each(lambda x, y: jnp.where(incl, _dot_nt(x, y), 0.0), rtb, ktb)
    nd = each(lambda m: jnp.where(blk16, m, 0.0), mab)
    n2 = each(_mm, nd, nd)
    mv = each(_mm, mak, vs)
    n4 = each(_mm, n2, n2)
    t = each(lambda n, n2_: eye_s + n + _mm(eye_s + n, n2_), nd, n2)
    n8 = each(_mm, n4, n4)
    t = each(lambda t_, n: t_ + _mm(t_, n), t, n4)
    t = each(lambda t_, n: t_ + _mm(t_, n), t, n8)
    size = 16
    while size < L:
        off = ((ri // size) == (ci // size) + 1) & ((ri // (2 * size)) == (ci // (2 * size)))
        u = each(lambda t_, m: _mm(t_, jnp.where(off, m, 0.0)), t, mab)
        t = each(lambda t_, u_: t_ + _mm(u_, t_), t, u)
        size *= 2
    wm = each(_mm, t, at)
    ul = each(_mm, t, mv)
    qe = each(lambda x, m, w: x + _mm(m, w), rt, mrb, wm)
    yl = each(lambda m, u_, m2, x: _mm(m, u_) + _mm(m2, x), mrb, ul, mrk, vs)
    gm = each(lambda b, w, p: _mm_tn(b, w) + jnp.where(eye_g, p, 0.0), bh, wm, p_l)
    hm = each(lambda b, u_, k_, x: _mm_tn(b, u_) + _mm_tn(k_, x), bh, ul, kh, vs)
    st = [st_ref[gi] for gi in G]
    ys = each(lambda q_, s_, y_: _mm(q_, s_) + y_, qe, st, yl)
    st_new = each(lambda g_, s_, h_: _mm(g_, s_) + h_, gm, st, hm)
    for gi in G:
        st_ref[gi] = st_new[gi]
        sl = sls[gi]
        y = ys[gi][0:L]
        for hh in range(1, RW_GROUP):
            y = y + ys[gi][hh * L:(hh + 1) * L]
        mean = gsum(y) * (1.0 / N)
        d = y - mean
        var = gsum(d * d) * (1.0 / N)
        yn = d * lax.rsqrt(var + LNX_EPS) * lw_ref[:, sl] + lb_ref[:, sl]
        yn = yn + gsum(r[gi] * k2[gi] * rk_ref[:, sl]) * v[gi]
        o_ref[:, sl] = (yn * g_ref[:, sl]).astype(o_ref.dtype)


def _rwkv(r, ld, k, v, a, g, k_k, k_a, r_k, lnx_w, lnx_b, B, T, *, ng=4, L=64):
    M = r.shape[0]
    L = min(L, T)
    nc = T // L
    W = ng * RW_GW
    blk = pl.BlockSpec((L, W), lambda b, hg, c: (b * nc + c, hg))
    par = pl.BlockSpec((1, W), lambda b, hg, c: (0, hg))
    row = lambda p: p.astype(F32).reshape(1, RW_WIDTH)
    return pl.pallas_call(
        functools.partial(_rwkv_body, ng=ng, L=L),
        grid=(B, RW_WIDTH // W, nc),
        in_specs=[blk] * 6 + [par] * 5,
        out_specs=blk,
        out_shape=jax.ShapeDtypeStruct((M, RW_WIDTH), BF16),
        scratch_shapes=[pltpu.VMEM((ng, RW_GW, RW_GW), F32)],
        compiler_params=_cparams(("parallel", "parallel", "arbitrary")),
        name="rwkv7_chunk",
    )(r, ld, k, v, a, g, row(k_k), row(k_a), row(r_k), row(lnx_w), row(lnx_b))


def _dsa_body(qd_ref, qi_ref, wi_ref, kd_ref, vd_ref, ki_ref, o_ref,
              keys_ref, bias_ref, qim_ref, cut_ref, m_ref, l_ref, acc_ref, *, tq, tk, ksel, T):
    i = pl.program_id(1)
    nkt = (i * tq + tq + tk - 1) // tk
    krow = lax.broadcasted_iota(I32, (tk, tq), 0)
    qpos = i * tq + lax.broadcasted_iota(I32, (tk, tq), 1)
    low_half = lax.broadcasted_iota(I32, (tq, LANES), 1) < IDX_DIM
    for h in range(IDX_HEADS):
        qt = qi_ref[:, (h // 2) * LANES:(h // 2 + 1) * LANES]
        qim_ref[h] = jnp.where(low_half if h % 2 == 0 else jnp.logical_not(low_half), qt,
                               jnp.zeros_like(qt)).astype(qim_ref.dtype)
    wit = wi_ref[...].T

    def score_tile(j, carry):
        off = pl.multiple_of(j * tk, tk)
        kt = ki_ref[pl.ds(off, tk), :]
        acc = jnp.zeros((tk, tq), F32)
        for h in range(IDX_HEADS):
            acc = acc + jnp.maximum(_dot_nt(kt, qim_ref[h]), 0.0) * wit[h:h + 1, :]
        acc = acc + 0.0
        sc = jnp.where((krow + off) <= qpos, acc, -jnp.inf)
        bits = pltpu.bitcast(sc, I32)
        keys_ref[pl.ds(off, tk), :] = bits ^ ((bits >> 31) & 0x7FFFFFFF)
        return carry

    lax.fori_loop(0, nkt, score_tile, 0)

    def count(pred):
        def body(j, c):
            off = pl.multiple_of(j * tk, tk)
            hit = jnp.where(pred(keys_ref[pl.ds(off, tk), :], krow + off), 1.0, 0.0)
            return c + jnp.sum(hit.reshape(tk // SUBLANES, SUBLANES, tq), axis=0)
        c = lax.fori_loop(0, nkt, body, jnp.zeros((SUBLANES, tq), F32))
        return jnp.sum(c, axis=0, keepdims=True)

    def bit_step(b, thr):
        cand = thr + jnp.left_shift(jnp.int32(1), 31 - b)
        cnt = count(lambda key, _: key >= cand)
        return jnp.where(cnt >= ksel, cand, thr)

    thr = lax.fori_loop(0, 32, bit_step, jnp.full((1, tq), INT_MIN, I32))

    n_gt = count(lambda key, _: key > thr)
    n_ge = count(lambda key, _: key >= thr)
    need = (n_ge > ksel) & (thr > NEG_INF_KEY)
    quota = ksel - n_gt
    cut_ref[...] = jnp.full((1, tq), T, I32)

    @pl.when(jnp.max(jnp.where(need, 1.0, 0.0)) > 0.0)
    def _ties():
        def pos_step(b, p):
            cand = p + jnp.left_shift(jnp.int32(1), (T.bit_length() - 1) - b)
            cnt = count(lambda key, pos: (key == thr) & (pos < cand))
            return jnp.where(cnt < quota, cand, p)
        p = lax.fori_loop(0, T.bit_length(), pos_step, jnp.zeros((1, tq), I32))
        cut_ref[...] = jnp.where(need, p, T)

    cut = cut_ref[...]
    thr_sel = jnp.maximum(thr, NEG_INF_KEY + 1)

    def bias_tile(j, carry):
        off = pl.multiple_of(j * tk, tk)
        key = keys_ref[pl.ds(off, tk), :]
        sel = (key > thr_sel) | ((key == thr_sel) & ((krow + off) <= cut))
        bias_ref[pl.ds(off, tk), :] = jnp.where(sel, 0.0, -jnp.inf)
        return carry

    lax.fori_loop(0, nkt, bias_tile, 0)

    m_ref[...] = jnp.full(m_ref.shape, -jnp.inf, F32)
    l_ref[...] = jnp.zeros(l_ref.shape, F32)
    acc_ref[...] = jnp.zeros(acc_ref.shape, F32)

    def attend(j, carry):
        off = pl.multiple_of(j * tk, tk)
        ks = kd_ref[pl.ds(off, tk), :]
        vt = vd_ref[pl.ds(off, tk), :].T
        bias = bias_ref[pl.ds(off, tk), :]
        heads = list(range(SA_HEADS))
        ss = [_dot_nt(ks, qd_ref[:, h * LANES:(h + 1) * LANES]) + bias for h in heads]
        _softmax_steps(ss, vt, m_ref, l_ref, acc_ref, heads, guard=True)
        return carry

    lax.fori_loop(0, nkt, attend, 0)
    for h in range(SA_HEADS):
        o_ref[:, h * LANES:(h + 1) * LANES] = (acc_ref[h] * (1.0 / l_ref[h])).T.astype(o_ref.dtype)


def _dsa(qd, qi, wi, kd, vd, ki, B, T, ksel, *, tq=512, tk=512):
    M = qd.shape[0]
    tq, tk = min(tq, T), min(tk, T)
    nq = T // tq
    qblk = lambda w: pl.BlockSpec((tq, w), lambda b, i: (b * nq + i, 0))
    kblk = pl.BlockSpec((T, LANES), lambda b, i: (b, 0))
    return pl.pallas_call(
        functools.partial(_dsa_body, tq=tq, tk=tk, ksel=ksel, T=T),
        grid=(B, nq),
        in_specs=[qblk(SA_WIDTH), qblk(IDX_HEADS * IDX_DIM), qblk(LANES), kblk, kblk, kblk],
        out_specs=qblk(SA_WIDTH),
        out_shape=jax.ShapeDtypeStruct((M, SA_WIDTH), BF16),
        scratch_shapes=[pltpu.VMEM((T, tq), I32), pltpu.VMEM((T, tq), F32),
                        pltpu.VMEM((IDX_HEADS, tq, LANES), BF16), pltpu.VMEM((1, tq), I32),
                        pltpu.VMEM((SA_HEADS, 1, tq), F32), pltpu.VMEM((SA_HEADS, 1, tq), F32),
                        pltpu.VMEM((SA_HEADS, LANES, tq), F32)],
        compiler_params=_cparams(("parallel", "arbitrary")),
        name="dsa_attention",
    )(qd, qi, wi, kd, vd, ki)


def _even_mixer(xf, h, w_in, w_out, e, q_norm, k_norm, lam_p, subln, conv_w, tabs64, lam_init, B, T):
    z = _matmul(h, _cast_pad(w_in, e), name="even_in")
    qh = _segment(z, 0, DA_WIDTH, q_norm, tabs64, T, gs=64, do_norm=True, do_rope=True,
                  scale=DA_HEAD_DIM ** -0.5 * LOG2E)
    kh = _segment(z, DA_WIDTH, DA_WIDTH, k_norm, tabs64, T, gs=64, do_norm=True, do_rope=True)
    vh = _segment(z, 2 * DA_WIDTH, DA_WIDTH, None, tabs64, T, gs=64, do_norm=False, do_rope=False)
    o = _diff_attention(qh, kh, vh, lam_p, subln, lam_init, B, T)
    y = _short_conv(z, conv_w, B, T)
    return _matmul(o, _cast_pad(w_out, e), a2=y, resid=xf, name="even_out"), z


def _odd_mixer(xf, h, w_in_t, w_out, o, mu_p, w0, w2, a0, a2, v0, v2, g2, k_k, k_a, r_k, lnx_w, lnx_b,
               q_norm, k_norm, idxk_norm, z_first, tabs64, tabs128, B, T, ksel):
    z = _matmul(h, w_in_t, b_transposed=True, name="odd_in")
    r, ld, k, v, a, g = _rwkv_prep(z, z_first, mu_p, w0, a0, v0, w2, a2, v2, g2, T)
    rw_out = _rwkv(r, ld, k, v, a, g, k_k, k_a, r_k.reshape(-1), lnx_w, lnx_b, B, T)
    qd = _segment(z, OD_Q, SA_WIDTH, q_norm, tabs128, T, gs=128, do_norm=True, do_rope=True,
                  scale=SA_HEAD_DIM ** -0.5 * LOG2E)
    kd = _segment(z, OD_KD, LANES, k_norm, tabs128, T, gs=128, do_norm=True, do_rope=True)
    vd = _segment(z, OD_VDD, LANES, None, tabs128, T, gs=128, do_norm=False, do_rope=False)
    qi = _segment(z, OD_QI, IDX_HEADS * IDX_DIM, None, tabs64, T, gs=64, do_norm=False, do_rope=True)
    ki = _segment(z, OD_KI, LANES, idxk_norm, tabs64, T, gs=64, do_norm=True, do_rope=True, pick="dup_low")
    wi = _segment(z, OD_KI, LANES, None, tabs64, T, gs=64, do_norm=False, do_rope=False,
                  scale=IDX_HEADS ** -0.5 * IDX_DIM ** -0.5, out_dtype=F32, pick="high16")
    sa_out = _dsa(qd, qi, wi, kd, vd, ki, B, T, ksel)
    return _matmul(rw_out, _cast_pad(w_out, o), a2=sa_out, resid=xf, name="odd_out")


def kernel(x, mix_norm, ffn_norm, ffn_gate, ffn_up, ffn_down, ev_w_in, ev_w_out, da_q_norm, da_k_norm, da_lambda, da_subln, sc_conv, od_w_in, od_w_out, rw_mu, rw_w0, rw_w2, rw_a0, rw_a2, rw_v0, rw_v2, rw_g2, rw_k_k, rw_k_a, rw_r_k, rw_lnx_w, rw_lnx_b, sa_q_norm, sa_k_norm, idx_k_norm):
    B, T, D = x.shape
    M = B * T
    ksel = min(TOPK_MAX, T // 4)
    xf = x.reshape(M, D)
    tabs64 = _rope_tables(T, 64)
    tabs128 = _rope_tables(T, 128)
    z_first = None
    for i in range(DEPTH):
        h = _rmsnorm(xf, mix_norm[i])
        if i % 2 == 0:
            e = i // 2
            lam_init = 0.8 - 0.6 * math.exp(-0.3 * i)
            xf, z = _even_mixer(xf, h, ev_w_in, ev_w_out, e, da_q_norm[e],
                                da_k_norm[e], da_lambda[e], da_subln[e], sc_conv[e], tabs64, lam_init, B, T)
            if z_first is None:
                z_first = z
        else:
            o = i // 2
            xf = _odd_mixer(xf, h, _cast_pad(jnp.swapaxes(od_w_in, 1, 2), o, rows_p=OD_PAD), od_w_out, o,
                            rw_mu[o].astype(F32).reshape(1, OD_RW_END),
                            rw_w0[o], rw_w2[o], rw_a0[o], rw_a2[o], rw_v0[o], rw_v2[o], rw_g2[o],
                            rw_k_k[o], rw_k_a[o], rw_r_k[o], rw_lnx_w[o], rw_lnx_b[o],
                            sa_q_norm[o], sa_k_norm[o], idx_k_norm[o], z_first, tabs64, tabs128, B, T, ksel)
        h = _rmsnorm(xf, ffn_norm[i])
        hid = _swiglu_proj(h, ffn_gate, ffn_up, i)
        wd = _cast_pad(ffn_down, i, tr=256, tc=2048)
        xf = _matmul(hid, wd, resid=xf, tm=512, tn=256, name="ffn_out")
    return xf.reshape(B, T, D)
```

```python
import functools
import math

import jax
import jax.numpy as jnp
from jax import lax
from jax.experimental import pallas as pl
from jax.experimental.pallas import tpu as pltpu

F32 = jnp.float32
BF16 = jnp.bfloat16
I32 = jnp.int32

D_MODEL = 4096
DEPTH = 4
DA_WIDTH = 2048
DA_HEADS = 16
DA_HEAD_DIM = 64
SC_WIDTH = 2048
CONV_W = 3
RW_WIDTH = 2048
RW_HEAD_DIM = 64
RW_HEADS = 32
W_LORA, A_LORA, V_LORA, G_LORA = 96, 96, 64, 256
LNX_EPS = 64e-5
SA_WIDTH = 2048
SA_HEAD_DIM = 128
SA_HEADS = 16
IDX_HEADS = 16
IDX_DIM = 64
TOPK_MAX = 256
FFN_HIDDEN = 11008
ROPE_THETA = 10000.0
EPS = 1e-6

LANES = 128
SUBLANES = 8
V7X_VMEM_BYTES = 64 * 1024 * 1024
V7X_MXU_DEPTH = 256
VMEM_LIMIT = (V7X_VMEM_BYTES * 3) // 4

RW_GROUP = V7X_MXU_DEPTH // RW_HEAD_DIM
RW_GW = RW_GROUP * RW_HEAD_DIM
OD_R, OD_K, OD_V = 0, RW_WIDTH, 2 * RW_WIDTH
OD_LORA = 3 * RW_WIDTH
LORA_W = W_LORA + A_LORA + V_LORA + G_LORA
OD_RW_END = OD_LORA + LORA_W
OD_Q = OD_RW_END
OD_KD = OD_Q + SA_WIDTH
OD_VDD = OD_KD + SA_HEAD_DIM
OD_QI = OD_VDD + SA_HEAD_DIM
OD_KI = OD_QI + IDX_HEADS * IDX_DIM
OD_IN = OD_KI + IDX_DIM + IDX_HEADS
OD_PAD = -(-OD_IN // 512) * 512
LOG2E = math.log2(math.e)
INT_MIN = -2 ** 31
NEG_INF_KEY = -2139095041


def _cparams(sem):
    return pltpu.CompilerParams(dimension_semantics=sem, vmem_limit_bytes=VMEM_LIMIT)


def _mm_body(*refs, k1, bt, has_resid):
    refs = list(refs)
    a_ref = refs.pop(0)
    a2_ref = refs.pop(0) if k1 else None
    b_ref = refs.pop(0)
    r_ref = refs.pop(0) if has_resid else None
    o_ref = refs.pop(0)
    if bt:
        acc = _dot_nt(a_ref[...], b_ref[...])
    elif k1:
        acc = (jnp.dot(a_ref[...], b_ref[0:k1, :], preferred_element_type=F32)
               + jnp.dot(a2_ref[...], b_ref[k1:, :], preferred_element_type=F32))
    else:
        acc = jnp.dot(a_ref[...], b_ref[...], preferred_element_type=F32)
    o_ref[...] = (r_ref[...] + acc) if has_resid else acc.astype(o_ref.dtype)


def _matmul(a, b, *, a2=None, resid=None, b_transposed=False, tm=1024, tn=512, single_buffer_a=False, name="mm"):
    M, k1 = a.shape
    K = k1 + (a2.shape[1] if a2 is not None else 0)
    N = b.shape[0] if b_transposed else b.shape[1]
    assert (b.shape[1] if b_transposed else b.shape[0]) == K and not (b_transposed and a2 is not None)
    tm, tn = min(tm, M), min(tn, N)
    assert M % tm == 0 and N % tn == 0, (a.shape, b.shape, tm, tn)
    in_specs = [pl.BlockSpec((tm, k1), lambda i, j: (i, 0), pipeline_mode=pl.Buffered(1 if single_buffer_a else 2))]
    args = [a]
    if a2 is not None:
        in_specs.append(pl.BlockSpec((tm, K - k1), lambda i, j: (i, 0)))
        args.append(a2)
    in_specs.append(pl.BlockSpec((tn, K), lambda i, j: (j, 0)) if b_transposed
                    else pl.BlockSpec((K, tn), lambda i, j: (0, j)))
    args.append(b)
    if resid is not None:
        in_specs.append(pl.BlockSpec((tm, tn), lambda i, j: (i, j)))
        args.append(resid)
    return pl.pallas_call(
        functools.partial(_mm_body, k1=k1 if a2 is not None else 0, bt=b_transposed, has_resid=resid is not None),
        grid=(M // tm, N // tn),
        in_specs=in_specs,
        out_specs=pl.BlockSpec((tm, tn), lambda i, j: (i, j)),
        out_shape=jax.ShapeDtypeStruct((M, N), F32),
        compiler_params=_cparams(("parallel", "arbitrary")),
        name=name,
    )(*args)


def _swiglu_body(a_ref, wg_ref, wu_ref, o_ref, gb_ref, ub_ref):
    @pl.when(pl.program_id(1) == 0)
    def _cast():
        gb_ref[...] = wg_ref[...].astype(BF16)
        ub_ref[...] = wu_ref[...].astype(BF16)

    a = a_ref[...]
    g = jnp.dot(a, gb_ref[...], preferred_element_type=F32)
    u = jnp.dot(a, ub_ref[...], preferred_element_type=F32)
    o_ref[...] = (g * (1.0 / (1.0 + jnp.exp(-g))) * u).astype(o_ref.dtype)


def _swiglu_proj(a, w_gate, w_up, layer, *, tm=1024, tn=256, name="ffn_in"):
    M, K = a.shape
    N = w_gate.shape[2]
    tm, tn = min(tm, M), min(tn, N)
    assert M % tm == 0 and N % tn == 0 and w_gate.shape[1] == K, (a.shape, w_gate.shape, tm, tn)
    wspec = pl.BlockSpec((None, K, tn), lambda j, i: (layer, 0, j))
    return pl.pallas_call(
        _swiglu_body,
        grid=(N // tn, M // tm),
        in_specs=[pl.BlockSpec((tm, K), lambda j, i: (i, 0)), wspec, wspec],
        out_specs=pl.BlockSpec((tm, tn), lambda j, i: (i, j)),
        out_shape=jax.ShapeDtypeStruct((M, N), BF16),
        scratch_shapes=[pltpu.VMEM((K, tn), BF16), pltpu.VMEM((K, tn), BF16)],
        compiler_params=_cparams(("parallel", "arbitrary")),
        name=name,
    )(a, w_gate, w_up)


def _cast_body(x_ref, o_ref, *, rows, cols, tr, tc):
    x = x_ref[...]
    r = pl.program_id(0) * tr + lax.broadcasted_iota(I32, x.shape, 0)
    c = pl.program_id(1) * tc + lax.broadcasted_iota(I32, x.shape, 1)
    o_ref[...] = jnp.where((r < rows) & (c < cols), x, 0.0).astype(o_ref.dtype)


def _cast_pad(w, layer, rows_p=None, cols_p=None, *, tr=512, tc=1024):
    _, rows, cols = w.shape
    rows_p, cols_p = rows_p or rows, cols_p or cols
    tr, tc = min(tr, rows_p), min(tc, cols_p)
    assert rows_p % tr == 0 and cols_p % tc == 0, (w.shape, rows_p, cols_p)
    return pl.pallas_call(
        functools.partial(_cast_body, rows=rows, cols=cols, tr=tr, tc=tc),
        grid=(rows_p // tr, cols_p // tc),
        in_specs=[pl.BlockSpec((None, tr, tc), lambda i, j: (layer, i, j))],
        out_specs=pl.BlockSpec((tr, tc), lambda i, j: (i, j)),
        out_shape=jax.ShapeDtypeStruct((rows_p, cols_p), BF16),
        compiler_params=_cparams(("parallel", "parallel")),
        name="cast_pad",
    )(w)


def _rms_body(x_ref, g_ref, o_ref):
    x = x_ref[...]
    ms = jnp.mean(x * x, axis=-1, keepdims=True)
    o_ref[...] = (x * lax.rsqrt(ms + EPS) * g_ref[...]).astype(o_ref.dtype)


def _rmsnorm(x, g, *, tr=256):
    M, D = x.shape
    tr = min(tr, M)
    return pl.pallas_call(
        _rms_body,
        grid=(M // tr,),
        in_specs=[pl.BlockSpec((tr, D), lambda i: (i, 0)),
                  pl.BlockSpec((1, D), lambda i: (0, 0))],
        out_specs=pl.BlockSpec((tr, D), lambda i: (i, 0)),
        out_shape=jax.ShapeDtypeStruct((M, D), BF16),
        compiler_params=_cparams(("parallel",)),
        name="rmsnorm",
    )(x, g.reshape(1, D))


def _rope_tables(T, gs):
    half = gs // 2
    inv = ROPE_THETA ** (-jnp.arange(half, dtype=F32) / half)
    ang = jnp.arange(T, dtype=jnp.int32).astype(F32)[:, None] * inv[None, :]
    cos, sin = jnp.cos(ang), jnp.sin(ang)
    cosg = jnp.concatenate([cos, cos], axis=1)
    sing = jnp.concatenate([-sin, sin], axis=1)
    reps = LANES // gs
    return jnp.tile(cosg, (1, reps)), jnp.tile(sing, (1, reps))


def _group_ones(gs):
    r = jnp.arange(LANES)
    return (r[:, None] // gs == r[None, :] // gs).astype(BF16)


def _seg_body(x_ref, g_ref, cos_ref, sin_ref, bd_ref, o_ref, *, gs, do_norm, do_rope, scale, pick):
    half = gs // 2
    lane = lax.broadcasted_iota(I32, (x_ref.shape[0], LANES), 1)
    for t in range(x_ref.shape[1] // LANES):
        sl = slice(t * LANES, (t + 1) * LANES)
        x = x_ref[:, sl]
        if pick == "dup_low":
            x = jnp.where(lane < LANES // 2, x, pltpu.roll(x, LANES // 2, 1))
        elif pick == "high16":
            x = jnp.where(lane < IDX_HEADS, pltpu.roll(x, LANES // 2, 1), 0.0)
        if do_norm:
            x2 = x * x
            hi = x2.astype(BF16)
            lo = (x2 - hi.astype(F32)).astype(BF16)
            bd = bd_ref[...]
            ssum = jnp.dot(hi, bd, preferred_element_type=F32) + jnp.dot(lo, bd, preferred_element_type=F32)
            x = x * lax.rsqrt(ssum * (1.0 / gs) + EPS) * g_ref[...]
        if do_rope:
            if gs == LANES:
                rot = pltpu.roll(x, half, 1)
            else:
                rot = jnp.where((lane & (gs - 1)) < half, pltpu.roll(x, LANES - half, 1), pltpu.roll(x, half, 1))
            x = x * cos_ref[...] + rot * sin_ref[...]
        if scale != 1.0:
            x = x * scale
        o_ref[:, sl] = x.astype(o_ref.dtype)


def _segment(z, col_off, width, gain, tables, T, *, gs, do_norm, do_rope, scale=1.0, out_dtype=BF16, pick=None,
             tr=512, max_cw=1024):
    M = z.shape[0]
    tr = min(tr, T)
    nrb = T // tr
    cw = math.gcd(math.gcd(col_off, width), max_cw)
    assert cw % LANES == 0, (col_off, width)
    cb = col_off // cw
    cos, sin = tables
    if gain is None:
        gain = jnp.ones((gs,), F32)
    gt = jnp.tile(gain.astype(F32), LANES // gs).reshape(1, LANES)
    return pl.pallas_call(
        functools.partial(_seg_body, gs=gs, do_norm=do_norm, do_rope=do_rope, scale=scale, pick=pick),
        grid=(M // tr, width // cw),
        in_specs=[pl.BlockSpec((tr, cw), lambda i, j: (i, cb + j)),
                  pl.BlockSpec((1, LANES), lambda i, j: (0, 0)),
                  pl.BlockSpec((tr, LANES), lambda i, j: (i % nrb, 0)),
                  pl.BlockSpec((tr, LANES), lambda i, j: (i % nrb, 0)),
                  pl.BlockSpec((LANES, LANES), lambda i, j: (0, 0))],
        out_specs=pl.BlockSpec((tr, cw), lambda i, j: (i, j)),
        out_shape=jax.ShapeDtypeStruct((M, width), out_dtype),
        compiler_params=_cparams(("parallel", "parallel")),
        name="segment",
    )(z, gt, cos, sin, _group_ones(gs))


def _dot_nt(a, b):
    return lax.dot_general(a, b, (((1,), (1,)), ((), ())), preferred_element_type=F32)


def _softmax_steps(ss, vt, m_ref, l_ref, acc_ref, idxs, guard):
    m_prev = [m_ref[i] for i in idxs]
    m_next = [jnp.maximum(mp, jnp.max(s, axis=0, keepdims=True)) for mp, s in zip(m_prev, ss)]
    m_use = [jnp.where(mn == -jnp.inf, 0.0, mn) for mn in m_next] if guard else m_next
    ps = [jnp.exp2(s - mu) for s, mu in zip(ss, m_use)]
    alpha = [jnp.exp2(mp - mu) for mp, mu in zip(m_prev, m_use)]
    pv = [jnp.dot(vt, p.astype(BF16), preferred_element_type=F32) for p in ps]
    for n, i in enumerate(idxs):
        l_ref[i] = alpha[n] * l_ref[i] + jnp.sum(ps[n], axis=0, keepdims=True)
        acc_ref[i] = alpha[n] * acc_ref[i] + pv[n]
        m_ref[i] = m_next[n]


def _dattn_body(lam_ref, sub_ref, q_ref, k_ref, v_ref, o_ref, m_ref, l_ref, acc_ref, *, tq, tk, nh, lam_init):
    i = pl.program_id(2)
    lane = lax.broadcasted_iota(I32, (tq, LANES), 1)
    qs = []
    for h in range(nh):
        q = q_ref[:, h * LANES:(h + 1) * LANES]
        zero = jnp.zeros_like(q)
        qs += [jnp.where(lane < DA_HEAD_DIM, q, zero), jnp.where(lane >= DA_HEAD_DIM, q, zero)]
    m_ref[...] = jnp.full(m_ref.shape, -jnp.inf, F32)
    l_ref[...] = jnp.zeros(l_ref.shape, F32)
    acc_ref[...] = jnp.zeros(acc_ref.shape, F32)
    reps = tk // LANES

    def tile_step(j, masked):
        off = pl.multiple_of(j * tk, tk)
        ks = [k_ref[pl.ds(off, tk), h * LANES:(h + 1) * LANES] for h in range(nh)]
        vs = [v_ref[pl.ds(off, tk), h * LANES:(h + 1) * LANES] for h in range(nh)]
        if masked:
            row = i * tq + lax.broadcasted_iota(I32, (tq, tk), 0)
            col = off + lax.broadcasted_iota(I32, (tq, tk), 1)
            vis = col <= row
        cs = range(2 * nh)
        s = [_dot_nt(qs[c], ks[c // 2]) for c in cs]
        if masked:
            s = [jnp.where(vis, x, -jnp.inf) for x in s]
        m_prev = [m_ref[c] for c in cs]
        m_next = [jnp.maximum(m_prev[c], jnp.max(s[c], axis=-1, keepdims=True)) for c in cs]
        p = [jnp.exp2(s[c] - jnp.concatenate([m_next[c]] * reps, axis=1)) for c in cs]
        alpha = [jnp.exp2(m_prev[c] - m_next[c]) for c in cs]
        pv = [jnp.dot(p[c].astype(BF16), vs[c // 2], preferred_element_type=F32) for c in cs]
        for c in cs:
            l_ref[c] = alpha[c] * l_ref[c] + jnp.sum(p[c], axis=-1, keepdims=True)
            acc_ref[c] = alpha[c] * acc_ref[c] + pv[c]
            m_ref[c] = m_next[c]

    n_full = (i * tq + 1) // tk
    n_all = (i * tq + tq + tk - 1) // tk

    def full_body(j, carry):
        tile_step(j, False)
        return carry

    def diag_body(j, carry):
        tile_step(j, True)
        return carry

    lax.fori_loop(0, n_full, full_body, 0)
    lax.fori_loop(n_full, n_all, diag_body, 0)

    lp = lam_ref[...]
    lam = (jnp.exp(jnp.sum(lp[0:1] * lp[1:2], axis=-1, keepdims=True))
           - jnp.exp(jnp.sum(lp[2:3] * lp[3:4], axis=-1, keepdims=True)) + lam_init)
    for h in range(nh):
        c = 2 * h
        o = acc_ref[c] * (1.0 / l_ref[c]) - lam * (acc_ref[c + 1] * (1.0 / l_ref[c + 1]))
        ms = jnp.mean(o * o, axis=-1, keepdims=True)
        o = o * lax.rsqrt(ms + EPS) * sub_ref[...] * (1.0 - lam_init)
        o_ref[:, h * LANES:(h + 1) * LANES] = o.astype(o_ref.dtype)


def _diff_attention(qh, kh, vh, lam_p, subln, lam_init, B, T, *, tq=512, tk=512, nh=4):
    M = qh.shape[0]
    tq, tk = min(tq, T), min(tk, T)
    nq = T // tq
    W = nh * LANES
    return pl.pallas_call(
        functools.partial(_dattn_body, tq=tq, tk=tk, nh=nh, lam_init=lam_init),
        grid=(B, DA_HEADS // nh, nq),
        in_specs=[pl.BlockSpec((4, DA_HEAD_DIM), lambda b, h, i: (0, 0)),
                  pl.BlockSpec((1, LANES), lambda b, h, i: (0, 0)),
                  pl.BlockSpec((tq, W), lambda b, h, i: (b * nq + i, h)),
                  pl.BlockSpec((T, W), lambda b, h, i: (b, h)),
                  pl.BlockSpec((T, W), lambda b, h, i: (b, h))],
        out_specs=pl.BlockSpec((tq, W), lambda b, h, i: (b * nq + i, h)),
        out_shape=jax.ShapeDtypeStruct((M, DA_WIDTH), BF16),
        scratch_shapes=[pltpu.VMEM((2 * nh, tq, LANES), F32)] * 3,
        compiler_params=_cparams(("parallel", "parallel", "arbitrary")),
        name="diff_attention",
    )(lam_p.astype(F32), subln.astype(F32).reshape(1, LANES), qh, kh, vh)


def _conv_body(gb_ref, gc_ref, u_ref, w_ref, o_ref, sh_ref, *, T):
    cu = gc_ref[...] * u_ref[...]
    sh_ref[0:SUBLANES, :] = jnp.zeros((SUBLANES, LANES), F32)
    sh_ref[SUBLANES:SUBLANES + T, :] = cu
    w = w_ref[...]
    conv = (sh_ref[SUBLANES - 2:SUBLANES - 2 + T, :] * w[0:1]
            + sh_ref[SUBLANES - 1:SUBLANES - 1 + T, :] * w[1:2]
            + cu * w[2:3])
    o_ref[...] = (gb_ref[...] * conv).astype(o_ref.dtype)


def _short_conv(z, conv_w, B, T):
    M = z.shape[0]
    nc = SC_WIDTH // LANES
    base = 3 * DA_WIDTH // LANES
    return pl.pallas_call(
        functools.partial(_conv_body, T=T),
        grid=(B, nc),
        in_specs=[pl.BlockSpec((T, LANES), lambda b, c: (b, base + c)),
                  pl.BlockSpec((T, LANES), lambda b, c: (b, base + nc + c)),
                  pl.BlockSpec((T, LANES), lambda b, c: (b, base + 2 * nc + c)),
                  pl.BlockSpec((CONV_W, LANES), lambda b, c: (0, c))],
        out_specs=pl.BlockSpec((T, LANES), lambda b, c: (b, c)),
        out_shape=jax.ShapeDtypeStruct((M, SC_WIDTH), BF16),
        scratch_shapes=[pltpu.VMEM((T + SUBLANES, LANES), F32)],
        compiler_params=_cparams(("parallel", "parallel")),
        name="short_conv",
    )(z, z, z, conv_w.astype(F32))


def _sigmoid(x):
    return 1.0 / (1.0 + jnp.exp(-x))


def _rwprep_body(z_ref, zh_ref, vf_ref, mu_ref, w0_ref, a0_ref, v0_ref, w2_ref, a2_ref, v2_ref, g2_ref,
                 r_o, ld_o, k_o, v_o, a_o, g_o, sh_ref, *, tr, nrb):
    i = pl.program_id(0)
    first = (i % nrb) == 0
    sh_ref[SUBLANES - 1:SUBLANES, :] = jnp.where(first, 0.0, zh_ref[SUBLANES - 1:SUBLANES, :])
    sh_ref[SUBLANES:SUBLANES + tr, :] = z_ref[...]

    def shifted(lo, hi):
        zc = z_ref[:, lo:hi]
        zp = sh_ref[SUBLANES - 1:SUBLANES - 1 + tr, lo:hi]
        return zc + (zp - zc) * mu_ref[:, lo:hi]

    r_o[...] = shifted(OD_R, OD_K)
    k_o[...] = shifted(OD_K, OD_V)
    lora = shifted(OD_LORA, OD_RW_END)
    lora_b = lora.astype(BF16)
    lw = w0_ref[...] + jnp.dot(jnp.tanh(lora).astype(BF16), w2_ref[...], preferred_element_type=F32)
    nlw = -lw
    softplus = jnp.maximum(nlw, 0.0) + jnp.log(1.0 + jnp.exp(-jnp.abs(nlw)))
    ld_o[...] = -jnp.exp(-softplus - 0.5)
    a_o[...] = _sigmoid(a0_ref[...] + jnp.dot(lora_b, a2_ref[...], preferred_element_type=F32))
    v = shifted(OD_V, OD_LORA)
    vg = _sigmoid(v0_ref[...] + jnp.dot(lora_b, v2_ref[...], preferred_element_type=F32))
    v_o[...] = v + (vf_ref[...] - v) * vg
    g_o[...] = jnp.dot(_sigmoid(lora).astype(BF16), g2_ref[...], preferred_element_type=F32)


def _place_rows(w, start):
    return jnp.pad(w, ((start, LORA_W - start - w.shape[0]), (0, 0))).astype(BF16)


def _rwkv_prep(z, z_first, mu_p, w0, a0, v0, w2, a2, v2, g2, T, *, tr=128):
    M = z.shape[0]
    tr = min(tr, T)
    nrb = T // tr
    W = OD_RW_END
    hb = tr // SUBLANES
    row = lambda a: a.astype(F32).reshape(1, RW_WIDTH)
    full = lambda shape: pl.BlockSpec(shape, lambda i: (0, 0))
    out = jax.ShapeDtypeStruct((M, RW_WIDTH), F32)
    ospec = pl.BlockSpec((tr, RW_WIDTH), lambda i: (i, 0))
    return pl.pallas_call(
        functools.partial(_rwprep_body, tr=tr, nrb=nrb),
        grid=(M // tr,),
        in_specs=[pl.BlockSpec((tr, W), lambda i: (i, 0)),
                  pl.BlockSpec((SUBLANES, W), lambda i: (jnp.maximum(i * hb - 1, 0), 0)),
                  pl.BlockSpec((tr, RW_WIDTH), lambda i: (i, 2 * DA_WIDTH // RW_WIDTH)),
                  full((1, W)), full((1, RW_WIDTH)), full((1, RW_WIDTH)), full((1, RW_WIDTH)),
                  full((LORA_W, RW_WIDTH)), full((LORA_W, RW_WIDTH)), full((LORA_W, RW_WIDTH)),
                  full((LORA_W, RW_WIDTH))],
        out_specs=[ospec] * 6,
        out_shape=[out] * 6,
        scratch_shapes=[pltpu.VMEM((tr + SUBLANES, W), F32)],
        compiler_params=_cparams(("parallel",)),
        name="rwkv_prep",
    )(z, z, z_first, mu_p, row(w0), row(a0), row(v0),
      _place_rows(w2, 0), _place_rows(a2, W_LORA), _place_rows(v2, W_LORA + A_LORA),
      _place_rows(g2, W_LORA + A_LORA + V_LORA))


def _split3(x):
    hi = x.astype(BF16)
    r1 = x - hi.astype(F32)
    mid = r1.astype(BF16)
    lo = (r1 - mid.astype(F32)).astype(BF16)
    return hi, mid, lo


def _mm(a, b):
    return jnp.dot(a.astype(BF16), b.astype(BF16), preferred_element_type=F32)


def _mm_tn(a, b):
    return jnp.dot(a.T.astype(BF16), b.astype(BF16), preferred_element_type=F32)


def _rwkv_body(r_ref, ld_ref, k_ref, v_ref, a_ref, g_ref, kk_ref, ka_ref, rk_ref, lw_ref, lb_ref,
               o_ref, st_ref, *, ng, L):
    c = pl.program_id(2)

    @pl.when(c == 0)
    def _init():
        st_ref[...] = jnp.zeros_like(st_ref)

    N = RW_HEAD_DIM
    S = RW_GROUP * L
    ri = lax.broadcasted_iota(I32, (S, S), 0)
    ci = lax.broadcasted_iota(I32, (S, S), 1)
    same = (ri // L) == (ci // L)
    incl = same & (ci <= ri)
    strict = same & (ci < ri)
    eye_s = (ci == ri).astype(F32)
    blk16 = (ri // 16) == (ci // 16)
    hmask = (lax.broadcasted_iota(I32, (S, RW_GW), 0) // L) == (lax.broadcasted_iota(I32, (S, RW_GW), 1) // N)
    rl = lax.broadcasted_iota(I32, (L, L), 0)
    cl = lax.broadcasted_iota(I32, (L, L), 1)
    tri = (cl <= rl).astype(BF16)
    rn = lax.broadcasted_iota(I32, (RW_GW, RW_GW), 0)
    cn = lax.broadcasted_iota(I32, (RW_GW, RW_GW), 1)
    eye_g = rn == cn
    ones_g = ((rn // N) == (cn // N)).astype(BF16)

    def gsum(x):
        hi = x.astype(BF16)
        lo = (x - hi.astype(F32)).astype(BF16)
        return jnp.dot(hi, ones_g, preferred_element_type=F32) + jnp.dot(lo, ones_g, preferred_element_type=F32)

    def tile(x):
        return jnp.concatenate([x] * RW_GROUP, axis=0)

    def stack(x):
        return jnp.where(hmask, tile(x), 0.0)

    G = range(ng)
    sls = [slice(gi * RW_GW, (gi + 1) * RW_GW) for gi in G]
    each = lambda f, *lists: [f(*xs) for xs in zip(*lists)]
    r = [r_ref[:, sl] for sl in sls]
    ld = [ld_ref[:, sl] for sl in sls]
    k = [k_ref[:, sl] for sl in sls]
    v = [v_ref[:, sl] for sl in sls]
    a = [a_ref[:, sl] for sl in sls]
    kk = [k[gi] * kk_ref[:, sls[gi]] for gi in G]
    kk = each(lambda x: x / jnp.maximum(jnp.sqrt(gsum(x * x)), 1e-12), kk)
    k2 = [k[gi] * (1.0 + (a[gi] - 1.0) * ka_ref[:, sls[gi]]) for gi in G]
    bv = each(lambda x, y: x * y, kk, a)
    parts = each(_split3, ld)
    cum = each(lambda p: (jnp.dot(tri, p[0], preferred_element_type=F32) + jnp.dot(tri, p[1], preferred_element_type=F32)
                          + jnp.dot(tri, p[2], preferred_element_type=F32)), parts)
    clast = each(lambda c_: c_[L - 1:L, :], cum)
    e_neg = each(lambda c_: jnp.exp(-c_), cum)
    e_l = each(lambda cl_, c_: jnp.exp(cl_ - c_), clast, cum)
    p_l = each(jnp.exp, clast)
    at = each(lambda x, c_, l_: stack(-x * jnp.exp(c_ - l_)), kk, cum, ld)
    rt = each(lambda x, c_: stack(x * jnp.exp(c_)), r, cum)
    vs = each(stack, v)
    bh = each(lambda x, e: stack(x * e), bv, e_l)
    kh = each(lambda x, e: stack(x * e), k2, e_l)
    btb = each(lambda x, e: tile(x * e).astype(BF16), bv, e_neg)
    ktb = each(lambda x, e: tile(x * e).astype(BF16), k2, e_neg)
    atb = each(lambda x: x.astype(BF16), at)
    rtb = each(lambda x: x.astype(BF16), rt)
    mab = each(lambda x, y: jnp.where(strict, _dot_nt(x, y), 0.0), atb, btb)
    mak = each(lambda x, y: jnp.where(strict, _dot_nt(x, y), 0.0), atb, ktb)
    mrb = each(lambda x, y: jnp.where(incl, _dot_nt(x, y), 0.0), rtb, btb)
    mrk = each(lambda x, y: jnp.where(incl, _dot_nt(x, y), 0.0), rtb, ktb)
    nd = each(lambda m: jnp.where(blk16, m, 0.0), mab)
    n2 = each(_mm, nd, nd)
    mv = each(_mm, mak, vs)
    n4 = each(_mm, n2, n2)
    t = each(lambda n, n2_: eye_s + n + _mm(eye_s + n, n2_), nd, n2)
    n8 = each(_mm, n4, n4)
    t = each(lambda t_, n: t_ + _mm(t_, n), t, n4)
    t = each(lambda t_, n: t_ + _mm(t_, n), t, n8)
    size = 16
    while size < L:
        off = ((ri // size) == (ci // size) + 1) & ((ri // (2 * size)) == (ci // (2 * size)))
        u = each(lambda t_, m: _mm(t_, jnp.where(off, m, 0.0)), t, mab)
        t = each(lambda t_, u_: t_ + _mm(u_, t_), t, u)
        size *= 2
    wm = each(_mm, t, at)
    ul = each(_mm, t, mv)
    qe = each(lambda x, m, w: x + _mm(m, w), rt, mrb, wm)
    yl = each(lambda m, u_, m2, x: _mm(m, u_) + _mm(m2, x), mrb, ul, mrk, vs)
    gm = each(lambda b, w, p: _mm_tn(b, w) + jnp.where(eye_g, p, 0.0), bh, wm, p_l)
    hm = each(lambda b, u_, k_, x: _mm_tn(b, u_) + _mm_tn(k_, x), bh, ul, kh, vs)
    st = [st_ref[gi] for gi in G]
    ys = each(lambda q_, s_, y_: _mm(q_, s_) + y_, qe, st, yl)
    st_new = each(lambda g_, s_, h_: _mm(g_, s_) + h_, gm, st, hm)
    for gi in G:
        st_ref[gi] = st_new[gi]
        sl = sls[gi]
        y = ys[gi][0:L]
        for hh in range(1, RW_GROUP):
            y = y + ys[gi][hh * L:(hh + 1) * L]
        mean = gsum(y) * (1.0 / N)
        d = y - mean
        var = gsum(d * d) * (1.0 / N)
        yn = d * lax.rsqrt(var + LNX_EPS) * lw_ref[:, sl] + lb_ref[:, sl]
        yn = yn + gsum(r[gi] * k2[gi] * rk_ref[:, sl]) * v[gi]
        o_ref[:, sl] = (yn * g_ref[:, sl]).astype(o_ref.dtype)


def _rwkv(r, ld, k, v, a, g, k_k, k_a, r_k, lnx_w, lnx_b, B, T, *, ng=4, L=64):
    M = r.shape[0]
    L = min(L, T)
    nc = T // L
    W = ng * RW_GW
    blk = pl.BlockSpec((L, W), lambda b, hg, c: (b * nc + c, hg))
    par = pl.BlockSpec((1, W), lambda b, hg, c: (0, hg))
    row = lambda p: p.astype(F32).reshape(1, RW_WIDTH)
    return pl.pallas_call(
        functools.partial(_rwkv_body, ng=ng, L=L),
        grid=(B, RW_WIDTH // W, nc),
        in_specs=[blk] * 6 + [par] * 5,
        out_specs=blk,
        out_shape=jax.ShapeDtypeStruct((M, RW_WIDTH), BF16),
        scratch_shapes=[pltpu.VMEM((ng, RW_GW, RW_GW), F32)],
        compiler_params=_cparams(("parallel", "parallel", "arbitrary")),
        name="rwkv7_chunk",
    )(r, ld, k, v, a, g, row(k_k), row(k_a), row(r_k), row(lnx_w), row(lnx_b))


def _dsa_body(qd_ref, qi_ref, wi_ref, kd_ref, vd_ref, ki_ref, o_ref,
              keys_ref, bias_ref, qim_ref, cut_ref, m_ref, l_ref, acc_ref, *, tq, tk, ksel, T):
    i = pl.program_id(1)
    nkt = (i * tq + tq + tk - 1) // tk
    krow = lax.broadcasted_iota(I32, (tk, tq), 0)
    qpos = i * tq + lax.broadcasted_iota(I32, (tk, tq), 1)
    low_half = lax.broadcasted_iota(I32, (tq, LANES), 1) < IDX_DIM
    for h in range(IDX_HEADS):
        qt = qi_ref[:, (h // 2) * LANES:(h // 2 + 1) * LANES]
        qim_ref[h] = jnp.where(low_half if h % 2 == 0 else jnp.logical_not(low_half), qt,
                               jnp.zeros_like(qt)).astype(qim_ref.dtype)
    wit = wi_ref[...].T

    def score_tile(j, carry):
        off = pl.multiple_of(j * tk, tk)
        kt = ki_ref[pl.ds(off, tk), :]
        acc = jnp.zeros((tk, tq), F32)
        for h in range(IDX_HEADS):
            acc = acc + jnp.maximum(_dot_nt(kt, qim_ref[h]), 0.0) * wit[h:h + 1, :]
        acc = acc + 0.0
        sc = jnp.where((krow + off) <= qpos, acc, -jnp.inf)
        bits = pltpu.bitcast(sc, I32)
        keys_ref[pl.ds(off, tk), :] = bits ^ ((bits >> 31) & 0x7FFFFFFF)
        return carry

    lax.fori_loop(0, nkt, score_tile, 0)

    def count(pred):
        def body(j, c):
            off = pl.multiple_of(j * tk, tk)
            hit = jnp.where(pred(keys_ref[pl.ds(off, tk), :], krow + off), 1.0, 0.0)
            return c + jnp.sum(hit.reshape(tk // SUBLANES, SUBLANES, tq), axis=0)
        c = lax.fori_loop(0, nkt, body, jnp.zeros((SUBLANES, tq), F32))
        return jnp.sum(c, axis=0, keepdims=True)

    def bit_step(b, thr):
        cand = thr + jnp.left_shift(jnp.int32(1), 31 - b)
        cnt = count(lambda key, _: key >= cand)
        return jnp.where(cnt >= ksel, cand, thr)

    thr = lax.fori_loop(0, 32, bit_step, jnp.full((1, tq), INT_MIN, I32))

    n_gt = count(lambda key, _: key > thr)
    n_ge = count(lambda key, _: key >= thr)
    need = (n_ge > ksel) & (thr > NEG_INF_KEY)
    quota = ksel - n_gt
    cut_ref[...] = jnp.full((1, tq), T, I32)

    @pl.when(jnp.max(jnp.where(need, 1.0, 0.0)) > 0.0)
    def _ties():
        def pos_step(b, p):
            cand = p + jnp.left_shift(jnp.int32(1), (T.bit_length() - 1) - b)
            cnt = count(lambda key, pos: (key == thr) & (pos < cand))
            return jnp.where(cnt < quota, cand, p)
        p = lax.fori_loop(0, T.bit_length(), pos_step, jnp.zeros((1, tq), I32))
        cut_ref[...] = jnp.where(need, p, T)

    cut = cut_ref[...]
    thr_sel = jnp.maximum(thr, NEG_INF_KEY + 1)

    def bias_tile(j, carry):
        off = pl.multiple_of(j * tk, tk)
        key = keys_ref[pl.ds(off, tk), :]
        sel = (key > thr_sel) | ((key == thr_sel) & ((krow + off) <= cut))
        bias_ref[pl.ds(off, tk), :] = jnp.where(sel, 0.0, -jnp.inf)
        return carry

    lax.fori_loop(0, nkt, bias_tile, 0)

    m_ref[...] = jnp.full(m_ref.shape, -jnp.inf, F32)
    l_ref[...] = jnp.zeros(l_ref.shape, F32)
    acc_ref[...] = jnp.zeros(acc_ref.shape, F32)

    def attend(j, carry):
        off = pl.multiple_of(j * tk, tk)
        ks = kd_ref[pl.ds(off, tk), :]
        vt = vd_ref[pl.ds(off, tk), :].T
        bias = bias_ref[pl.ds(off, tk), :]
        heads = list(range(SA_HEADS))
        ss = [_dot_nt(ks, qd_ref[:, h * LANES:(h + 1) * LANES]) + bias for h in heads]
        _softmax_steps(ss, vt, m_ref, l_ref, acc_ref, heads, guard=True)
        return carry

    lax.fori_loop(0, nkt, attend, 0)
    for h in range(SA_HEADS):
        o_ref[:, h * LANES:(h + 1) * LANES] = (acc_ref[h] * (1.0 / l_ref[h])).T.astype(o_ref.dtype)


def _dsa(qd, qi, wi, kd, vd, ki, B, T, ksel, *, tq=512, tk=512):
    M = qd.shape[0]
    tq, tk = min(tq, T), min(tk, T)
    nq = T // tq
    qblk = lambda w: pl.BlockSpec((tq, w), lambda b, i: (b * nq + i, 0))
    kblk = pl.BlockSpec((T, LANES), lambda b, i: (b, 0))
    return pl.pallas_call(
        functools.partial(_dsa_body, tq=tq, tk=tk, ksel=ksel, T=T),
        grid=(B, nq),
        in_specs=[qblk(SA_WIDTH), qblk(IDX_HEADS * IDX_DIM), qblk(LANES), kblk, kblk, kblk],
        out_specs=qblk(SA_WIDTH),
        out_shape=jax.ShapeDtypeStruct((M, SA_WIDTH), BF16),
        scratch_shapes=[pltpu.VMEM((T, tq), I32), pltpu.VMEM((T, tq), F32),
                        pltpu.VMEM((IDX_HEADS, tq, LANES), BF16), pltpu.VMEM((1, tq), I32),
                        pltpu.VMEM((SA_HEADS, 1, tq), F32), pltpu.VMEM((SA_HEADS, 1, tq), F32),
                        pltpu.VMEM((SA_HEADS, LANES, tq), F32)],
        compiler_params=_cparams(("parallel", "arbitrary")),
        name="dsa_attention",
    )(qd, qi, wi, kd, vd, ki)


def _even_mixer(xf, h, w_in, w_out, e, q_norm, k_norm, lam_p, subln, conv_w, tabs64, lam_init, B, T):
    z = _matmul(h, _cast_pad(w_in, e), name="even_in")
    qh = _segment(z, 0, DA_WIDTH, q_norm, tabs64, T, gs=64, do_norm=True, do_rope=True,
                  scale=DA_HEAD_DIM ** -0.5 * LOG2E)
    kh = _segment(z, DA_WIDTH, DA_WIDTH, k_norm, tabs64, T, gs=64, do_norm=True, do_rope=True)
    vh = _segment(z, 2 * DA_WIDTH, DA_WIDTH, None, tabs64, T, gs=64, do_norm=False, do_rope=False)
    o = _diff_attention(qh, kh, vh, lam_p, subln, lam_init, B, T)
    y = _short_conv(z, conv_w, B, T)
    return _matmul(o, _cast_pad(w_out, e), a2=y, resid=xf, name="even_out"), z


def _odd_mixer(xf, h, w_in_t, w_out, o, mu_p, w0, w2, a0, a2, v0, v2, g2, k_k, k_a, r_k, lnx_w, lnx_b,
               q_norm, k_norm, idxk_norm, z_first, tabs64, tabs128, B, T, ksel):
    z = _matmul(h, w_in_t, b_transposed=True, name="odd_in")
    r, ld, k, v, a, g = _rwkv_prep(z, z_first, mu_p, w0, a0, v0, w2, a2, v2, g2, T)
    rw_out = _rwkv(r, ld, k, v, a, g, k_k, k_a, r_k.reshape(-1), lnx_w, lnx_b, B, T)
    qd = _segment(z, OD_Q, SA_WIDTH, q_norm, tabs128, T, gs=128, do_norm=True, do_rope=True,
                  scale=SA_HEAD_DIM ** -0.5 * LOG2E)
    kd = _segment(z, OD_KD, LANES, k_norm, tabs128, T, gs=128, do_norm=True, do_rope=True)
    vd = _segment(z, OD_VDD, LANES, None, tabs128, T, gs=128, do_norm=False, do_rope=False)
    qi = _segment(z, OD_QI, IDX_HEADS * IDX_DIM, None, tabs64, T, gs=64, do_norm=False, do_rope=True)
    ki = _segment(z, OD_KI, LANES, idxk_norm, tabs64, T, gs=64, do_norm=True, do_rope=True, pick="dup_low")
    wi = _segment(z, OD_KI, LANES, None, tabs64, T, gs=64, do_norm=False, do_rope=False,
                  scale=IDX_HEADS ** -0.5 * IDX_DIM ** -0.5, out_dtype=F32, pick="high16")
    sa_out = _dsa(qd, qi, wi, kd, vd, ki, B, T, ksel)
    return _matmul(rw_out, _cast_pad(w_out, o), a2=sa_out, resid=xf, name="odd_out")


def kernel(x, mix_norm, ffn_norm, ffn_gate, ffn_up, ffn_down, ev_w_in, ev_w_out, da_q_norm, da_k_norm, da_lambda, da_subln, sc_conv, od_w_in, od_w_out, rw_mu, rw_w0, rw_w2, rw_a0, rw_a2, rw_v0, rw_v2, rw_g2, rw_k_k, rw_k_a, rw_r_k, rw_lnx_w, rw_lnx_b, sa_q_norm, sa_k_norm, idx_k_norm):
    B, T, D = x.shape
    M = B * T
    ksel = min(TOPK_MAX, T // 4)
    xf = x.reshape(M, D)
    tabs64 = _rope_tables(T, 64)
    tabs128 = _rope_tables(T, 128)
    z_first = None
    for i in range(DEPTH):
        h = _rmsnorm(xf, mix_norm[i])
        if i % 2 == 0:
            e = i // 2
            lam_init = 0.8 - 0.6 * math.exp(-0.3 * i)
            xf, z = _even_mixer(xf, h, ev_w_in, ev_w_out, e, da_q_norm[e],
                                da_k_norm[e], da_lambda[e], da_subln[e], sc_conv[e], tabs64, lam_init, B, T)
            if z_first is None:
                z_first = z
        else:
            o = i // 2
            xf = _odd_mixer(xf, h, _cast_pad(jnp.swapaxes(od_w_in, 1, 2), o, rows_p=OD_PAD), od_w_out, o,
                            rw_mu[o].astype(F32).reshape(1, OD_RW_END),
                            rw_w0[o], rw_w2[o], rw_a0[o], rw_a2[o], rw_v0[o], rw_v2[o], rw_g2[o],
                            rw_k_k[o], rw_k_a[o], rw_r_k[o], rw_lnx_w[o], rw_lnx_b[o],
                            sa_q_norm[o], sa_k_norm[o], idx_k_norm[o], z_first, tabs64, tabs128, B, T, ksel)
        h = _rmsnorm(xf, ffn_norm[i])
        hid = _swiglu_proj(h, ffn_gate, ffn_up, i)
        wd = _cast_pad(ffn_down, i, tr=256, tc=2048)
        xf = _matmul(hid, wd, resid=xf, tm=1024, tn=256, single_buffer_a=True, name="ffn_out")
    return xf.reshape(B, T, D)
```

```python
import functools
import math

import jax
import jax.numpy as jnp
from jax import lax
from jax.experimental import pallas as pl
from jax.experimental.pallas import tpu as pltpu

F32 = jnp.float32
BF16 = jnp.bfloat16
I32 = jnp.int32

D_MODEL = 4096
DEPTH = 4
DA_WIDTH = 2048
DA_HEADS = 16
DA_HEAD_DIM = 64
SC_WIDTH = 2048
CONV_W = 3
RW_WIDTH = 2048
RW_HEAD_DIM = 64
RW_HEADS = 32
W_LORA, A_LORA, V_LORA, G_LORA = 96, 96, 64, 256
LNX_EPS = 64e-5
SA_WIDTH = 2048
SA_HEAD_DIM = 128
SA_HEADS = 16
IDX_HEADS = 16
IDX_DIM = 64
TOPK_MAX = 256
FFN_HIDDEN = 11008
ROPE_THETA = 10000.0
EPS = 1e-6

LANES = 128
SUBLANES = 8
V7X_VMEM_BYTES = 64 * 1024 * 1024
V7X_MXU_DEPTH = 256
VMEM_LIMIT = (V7X_VMEM_BYTES * 3) // 4

RW_GROUP = V7X_MXU_DEPTH // RW_HEAD_DIM
RW_GW = RW_GROUP * RW_HEAD_DIM
OD_R, OD_K, OD_V = 0, RW_WIDTH, 2 * RW_WIDTH
OD_LORA = 3 * RW_WIDTH
LORA_W = W_LORA + A_LORA + V_LORA + G_LORA
OD_RW_END = OD_LORA + LORA_W
OD_Q = OD_RW_END
OD_KD = OD_Q + SA_WIDTH
OD_VDD = OD_KD + SA_HEAD_DIM
OD_QI = OD_VDD + SA_HEAD_DIM
OD_KI = OD_QI + IDX_HEADS * IDX_DIM
OD_IN = OD_KI + IDX_DIM + IDX_HEADS
OD_PAD = -(-OD_IN // 512) * 512
LOG2E = math.log2(math.e)
INT_MIN = -2 ** 31
NEG_INF_KEY = -2139095041


def _cparams(sem):
    return pltpu.CompilerParams(dimension_semantics=sem, vmem_limit_bytes=VMEM_LIMIT)


def _mm_body(*refs, k1, bt, has_resid):
    refs = list(refs)
    a_ref = refs.pop(0)
    a2_ref = refs.pop(0) if k1 else None
    b_ref = refs.pop(0)
    r_ref = refs.pop(0) if has_resid else None
    o_ref = refs.pop(0)
    if bt:
        acc = _dot_nt(a_ref[...], b_ref[...])
    elif k1:
        acc = (jnp.dot(a_ref[...], b_ref[0:k1, :], preferred_element_type=F32)
               + jnp.dot(a2_ref[...], b_ref[k1:, :], preferred_element_type=F32))
    else:
        acc = jnp.dot(a_ref[...], b_ref[...], preferred_element_type=F32)
    o_ref[...] = (r_ref[...] + acc) if has_resid else acc.astype(o_ref.dtype)


def _matmul(a, b, *, a2=None, resid=None, b_transposed=False, tm=1024, tn=512, single_buffer_a=False, name="mm"):
    M, k1 = a.shape
    K = k1 + (a2.shape[1] if a2 is not None else 0)
    N = b.shape[0] if b_transposed else b.shape[1]
    assert (b.shape[1] if b_transposed else b.shape[0]) == K and not (b_transposed and a2 is not None)
    tm, tn = min(tm, M), min(tn, N)
    assert M % tm == 0 and N % tn == 0, (a.shape, b.shape, tm, tn)
    in_specs = [pl.BlockSpec((tm, k1), lambda i, j: (i, 0), pipeline_mode=pl.Buffered(1 if single_buffer_a else 2))]
    args = [a]
    if a2 is not None:
        in_specs.append(pl.BlockSpec((tm, K - k1), lambda i, j: (i, 0)))
        args.append(a2)
    in_specs.append(pl.BlockSpec((tn, K), lambda i, j: (j, 0)) if b_transposed
                    else pl.BlockSpec((K, tn), lambda i, j: (0, j)))
    args.append(b)
    if resid is not None:
        in_specs.append(pl.BlockSpec((tm, tn), lambda i, j: (i, j)))
        args.append(resid)
    return pl.pallas_call(
        functools.partial(_mm_body, k1=k1 if a2 is not None else 0, bt=b_transposed, has_resid=resid is not None),
        grid=(M // tm, N // tn),
        in_specs=in_specs,
        out_specs=pl.BlockSpec((tm, tn), lambda i, j: (i, j)),
        out_shape=jax.ShapeDtypeStruct((M, N), F32),
        compiler_params=_cparams(("parallel", "arbitrary")),
        name=name,
    )(*args)


def _swiglu_body(a_ref, wg_ref, wu_ref, o_ref, gb_ref, ub_ref):
    @pl.when(pl.program_id(1) == 0)
    def _cast():
        gb_ref[...] = wg_ref[...].astype(BF16)
        ub_ref[...] = wu_ref[...].astype(BF16)

    a = a_ref[...]
    g = jnp.dot(a, gb_ref[...], preferred_element_type=F32)
    u = jnp.dot(a, ub_ref[...], preferred_element_type=F32)
    o_ref[...] = (g * (1.0 / (1.0 + jnp.exp(-g))) * u).astype(o_ref.dtype)


def _swiglu_proj(a, w_gate, w_up, layer, *, tm=1024, tn=256, name="ffn_in"):
    M, K = a.shape
    N = w_gate.shape[2]
    tm, tn = min(tm, M), min(tn, N)
    assert M % tm == 0 and N % tn == 0 and w_gate.shape[1] == K, (a.shape, w_gate.shape, tm, tn)
    wspec = pl.BlockSpec((None, K, tn), lambda j, i: (layer, 0, j))
    return pl.pallas_call(
        _swiglu_body,
        grid=(N // tn, M // tm),
        in_specs=[pl.BlockSpec((tm, K), lambda j, i: (i, 0)), wspec, wspec],
        out_specs=pl.BlockSpec((tm, tn), lambda j, i: (i, j)),
        out_shape=jax.ShapeDtypeStruct((M, N), BF16),
        scratch_shapes=[pltpu.VMEM((K, tn), BF16), pltpu.VMEM((K, tn), BF16)],
        compiler_params=_cparams(("parallel", "arbitrary")),
        name=name,
    )(a, w_gate, w_up)


def _cast_body(x_ref, o_ref, *, rows, cols, tr, tc):
    x = x_ref[...]
    r = pl.program_id(0) * tr + lax.broadcasted_iota(I32, x.shape, 0)
    c = pl.program_id(1) * tc + lax.broadcasted_iota(I32, x.shape, 1)
    o_ref[...] = jnp.where((r < rows) & (c < cols), x, 0.0).astype(o_ref.dtype)


def _cast_pad(w, layer, rows_p=None, cols_p=None, *, tr=512, tc=1024):
    _, rows, cols = w.shape
    rows_p, cols_p = rows_p or rows, cols_p or cols
    tr, tc = min(tr, rows_p), min(tc, cols_p)
    assert rows_p % tr == 0 and cols_p % tc == 0, (w.shape, rows_p, cols_p)
    return pl.pallas_call(
        functools.partial(_cast_body, rows=rows, cols=cols, tr=tr, tc=tc),
        grid=(rows_p // tr, cols_p // tc),
        in_specs=[pl.BlockSpec((None, tr, tc), lambda i, j: (layer, i, j))],
        out_specs=pl.BlockSpec((tr, tc), lambda i, j: (i, j)),
        out_shape=jax.ShapeDtypeStruct((rows_p, cols_p), BF16),
        compiler_params=_cparams(("parallel", "parallel")),
        name="cast_pad",
    )(w)


def _rms_body(x_ref, g_ref, o_ref):
    x = x_ref[...]
    ms = jnp.mean(x * x, axis=-1, keepdims=True)
    o_ref[...] = (x * lax.rsqrt(ms + EPS) * g_ref[...]).astype(o_ref.dtype)


def _rmsnorm(x, g, *, tr=256):
    M, D = x.shape
    tr = min(tr, M)
    return pl.pallas_call(
        _rms_body,
        grid=(M // tr,),
        in_specs=[pl.BlockSpec((tr, D), lambda i: (i, 0)),
                  pl.BlockSpec((1, D), lambda i: (0, 0))],
        out_specs=pl.BlockSpec((tr, D), lambda i: (i, 0)),
        out_shape=jax.ShapeDtypeStruct((M, D), BF16),
        compiler_params=_cparams(("parallel",)),
        name="rmsnorm",
    )(x, g.reshape(1, D))


def _rope_tables(T, gs):
    half = gs // 2
    inv = ROPE_THETA ** (-jnp.arange(half, dtype=F32) / half)
    ang = jnp.arange(T, dtype=jnp.int32).astype(F32)[:, None] * inv[None, :]
    cos, sin = jnp.cos(ang), jnp.sin(ang)
    cosg = jnp.concatenate([cos, cos], axis=1)
    sing = jnp.concatenate([-sin, sin], axis=1)
    reps = LANES // gs
    return jnp.tile(cosg, (1, reps)), jnp.tile(sing, (1, reps))


def _group_ones(gs):
    r = jnp.arange(LANES)
    return (r[:, None] // gs == r[None, :] // gs).astype(BF16)


def _seg_body(x_ref, g_ref, cos_ref, sin_ref, bd_ref, o_ref, *, gs, do_norm, do_rope, scale, pick):
    half = gs // 2
    lane = lax.broadcasted_iota(I32, (x_ref.shape[0], LANES), 1)
    for t in range(x_ref.shape[1] // LANES):
        sl = slice(t * LANES, (t + 1) * LANES)
        x = x_ref[:, sl]
        if pick == "dup_low":
            x = jnp.where(lane < LANES // 2, x, pltpu.roll(x, LANES // 2, 1))
        elif pick == "high16":
            x = jnp.where(lane < IDX_HEADS, pltpu.roll(x, LANES // 2, 1), 0.0)
        if do_norm:
            x2 = x * x
            hi = x2.astype(BF16)
            lo = (x2 - hi.astype(F32)).astype(BF16)
            bd = bd_ref[...]
            ssum = jnp.dot(hi, bd, preferred_element_type=F32) + jnp.dot(lo, bd, preferred_element_type=F32)
            x = x * lax.rsqrt(ssum * (1.0 / gs) + EPS) * g_ref[...]
        if do_rope:
            if gs == LANES:
                rot = pltpu.roll(x, half, 1)
            else:
                rot = jnp.where((lane & (gs - 1)) < half, pltpu.roll(x, LANES - half, 1), pltpu.roll(x, half, 1))
            x = x * cos_ref[...] + rot * sin_ref[...]
        if scale != 1.0:
            x = x * scale
        o_ref[:, sl] = x.astype(o_ref.dtype)


def _segment(z, col_off, width, gain, tables, T, *, gs, do_norm, do_rope, scale=1.0, out_dtype=BF16, pick=None,
             tr=512, max_cw=1024):
    M = z.shape[0]
    tr = min(tr, T)
    nrb = T // tr
    cw = math.gcd(math.gcd(col_off, width), max_cw)
    assert cw % LANES == 0, (col_off, width)
    cb = col_off // cw
    cos, sin = tables
    if gain is None:
        gain = jnp.ones((gs,), F32)
    gt = jnp.tile(gain.astype(F32), LANES // gs).reshape(1, LANES)
    return pl.pallas_call(
        functools.partial(_seg_body, gs=gs, do_norm=do_norm, do_rope=do_rope, scale=scale, pick=pick),
        grid=(M // tr, width // cw),
        in_specs=[pl.BlockSpec((tr, cw), lambda i, j: (i, cb + j)),
                  pl.BlockSpec((1, LANES), lambda i, j: (0, 0)),
                  pl.BlockSpec((tr, LANES), lambda i, j: (i % nrb, 0)),
                  pl.BlockSpec((tr, LANES), lambda i, j: (i % nrb, 0)),
                  pl.BlockSpec((LANES, LANES), lambda i, j: (0, 0))],
        out_specs=pl.BlockSpec((tr, cw), lambda i, j: (i, j)),
        out_shape=jax.ShapeDtypeStruct((M, width), out_dtype),
        compiler_params=_cparams(("parallel", "parallel")),
        name="segment",
    )(z, gt, cos, sin, _group_ones(gs))


def _dot_nt(a, b):
    return lax.dot_general(a, b, (((1,), (1,)), ((), ())), preferred_element_type=F32)


def _softmax_steps(ss, vt, m_ref, l_ref, acc_ref, idxs, guard):
    m_prev = [m_ref[i] for i in idxs]
    m_next = [jnp.maximum(mp, jnp.max(s, axis=0, keepdims=True)) for mp, s in zip(m_prev, ss)]
    m_use = [jnp.where(mn == -jnp.inf, 0.0, mn) for mn in m_next] if guard else m_next
    ps = [jnp.exp2(s - mu) for s, mu in zip(ss, m_use)]
    alpha = [jnp.exp2(mp - mu) for mp, mu in zip(m_prev, m_use)]
    pv = [jnp.dot(vt, p.astype(BF16), preferred_element_type=F32) for p in ps]
    for n, i in enumerate(idxs):
        l_ref[i] = alpha[n] * l_ref[i] + jnp.sum(ps[n], axis=0, keepdims=True)
        acc_ref[i] = alpha[n] * acc_ref[i] + pv[n]
        m_ref[i] = m_next[n]


def _dattn_body(lam_ref, sub_ref, q_ref, k_ref, v_ref, o_ref, m_ref, l_ref, acc_ref, *, tq, tk, nh, lam_init):
    i = pl.program_id(2)
    lane = lax.broadcasted_iota(I32, (tq, LANES), 1)
    qs = []
    for h in range(nh):
        q = q_ref[:, h * LANES:(h + 1) * LANES]
        zero = jnp.zeros_like(q)
        qs += [jnp.where(lane < DA_HEAD_DIM, q, zero), jnp.where(lane >= DA_HEAD_DIM, q, zero)]
    m_ref[...] = jnp.full(m_ref.shape, -jnp.inf, F32)
    l_ref[...] = jnp.zeros(l_ref.shape, F32)
    acc_ref[...] = jnp.zeros(acc_ref.shape, F32)
    reps = tk // LANES

    def tile_step(j, masked):
        off = pl.multiple_of(j * tk, tk)
        ks = [k_ref[pl.ds(off, tk), h * LANES:(h + 1) * LANES] for h in range(nh)]
        vs = [v_ref[pl.ds(off, tk), h * LANES:(h + 1) * LANES] for h in range(nh)]
        if masked:
            row = i * tq + lax.broadcasted_iota(I32, (tq, tk), 0)
            col = off + lax.broadcasted_iota(I32, (tq, tk), 1)
            vis = col <= row
        cs = range(2 * nh)
        s = [_dot_nt(qs[c], ks[c // 2]) for c in cs]
        if masked:
            s = [jnp.where(vis, x, -jnp.inf) for x in s]
        m_prev = [m_ref[c] for c in cs]
        m_next = [jnp.maximum(m_prev[c], jnp.max(s[c], axis=-1, keepdims=True)) for c in cs]
        p = [jnp.exp2(s[c] - jnp.concatenate([m_next[c]] * reps, axis=1)) for c in cs]
        alpha = [jnp.exp2(m_prev[c] - m_next[c]) for c in cs]
        pv = [jnp.dot(p[c].astype(BF16), vs[c // 2], preferred_element_type=F32) for c in cs]
        for c in cs:
            l_ref[c] = alpha[c] * l_ref[c] + jnp.sum(p[c], axis=-1, keepdims=True)
            acc_ref[c] = alpha[c] * acc_ref[c] + pv[c]
            m_ref[c] = m_next[c]

    n_full = (i * tq + 1) // tk
    n_all = (i * tq + tq + tk - 1) // tk

    def full_body(j, carry):
        tile_step(j, False)
        return carry

    def diag_body(j, carry):
        tile_step(j, True)
        return carry

    lax.fori_loop(0, n_full, full_body, 0)
    lax.fori_loop(n_full, n_all, diag_body, 0)

    lp = lam_ref[...]
    lam = (jnp.exp(jnp.sum(lp[0:1] * lp[1:2], axis=-1, keepdims=True))
           - jnp.exp(jnp.sum(lp[2:3] * lp[3:4], axis=-1, keepdims=True)) + lam_init)
    for h in range(nh):
        c = 2 * h
        o = acc_ref[c] * (1.0 / l_ref[c]) - lam * (acc_ref[c + 1] * (1.0 / l_ref[c + 1]))
        ms = jnp.mean(o * o, axis=-1, keepdims=True)
        o = o * lax.rsqrt(ms + EPS) * sub_ref[...] * (1.0 - lam_init)
        o_ref[:, h * LANES:(h + 1) * LANES] = o.astype(o_ref.dtype)


def _diff_attention(qh, kh, vh, lam_p, subln, lam_init, B, T, *, tq=512, tk=512, nh=4):
    M = qh.shape[0]
    tq, tk = min(tq, T), min(tk, T)
    nq = T // tq
    W = nh * LANES
    return pl.pallas_call(
        functools.partial(_dattn_body, tq=tq, tk=tk, nh=nh, lam_init=lam_init),
        grid=(B, DA_HEADS // nh, nq),
        in_specs=[pl.BlockSpec((4, DA_HEAD_DIM), lambda b, h, i: (0, 0)),
                  pl.BlockSpec((1, LANES), lambda b, h, i: (0, 0)),
                  pl.BlockSpec((tq, W), lambda b, h, i: (b * nq + i, h)),
                  pl.BlockSpec((T, W), lambda b, h, i: (b, h)),
                  pl.BlockSpec((T, W), lambda b, h, i: (b, h))],
        out_specs=pl.BlockSpec((tq, W), lambda b, h, i: (b * nq + i, h)),
        out_shape=jax.ShapeDtypeStruct((M, DA_WIDTH), BF16),
        scratch_shapes=[pltpu.VMEM((2 * nh, tq, LANES), F32)] * 3,
        compiler_params=_cparams(("parallel", "parallel", "arbitrary")),
        name="diff_attention",
    )(lam_p.astype(F32), subln.astype(F32).reshape(1, LANES), qh, kh, vh)


def _conv_body(gb_ref, gc_ref, u_ref, w_ref, o_ref, sh_ref, *, T):
    cu = gc_ref[...] * u_ref[...]
    sh_ref[0:SUBLANES, :] = jnp.zeros((SUBLANES, LANES), F32)
    sh_ref[SUBLANES:SUBLANES + T, :] = cu
    w = w_ref[...]
    conv = (sh_ref[SUBLANES - 2:SUBLANES - 2 + T, :] * w[0:1]
            + sh_ref[SUBLANES - 1:SUBLANES - 1 + T, :] * w[1:2]
            + cu * w[2:3])
    o_ref[...] = (gb_ref[...] * conv).astype(o_ref.dtype)


def _short_conv(z, conv_w, B, T):
    M = z.shape[0]
    nc = SC_WIDTH // LANES
    base = 3 * DA_WIDTH // LANES
    return pl.pallas_call(
        functools.partial(_conv_body, T=T),
        grid=(B, nc),
        in_specs=[pl.BlockSpec((T, LANES), lambda b, c: (b, base + c)),
                  pl.BlockSpec((T, LANES), lambda b, c: (b, base + nc + c)),
                  pl.BlockSpec((T, LANES), lambda b, c: (b, base + 2 * nc + c)),
                  pl.BlockSpec((CONV_W, LANES), lambda b, c: (0, c))],
        out_specs=pl.BlockSpec((T, LANES), lambda b, c: (b, c)),
        out_shape=jax.ShapeDtypeStruct((M, SC_WIDTH), BF16),
        scratch_shapes=[pltpu.VMEM((T + SUBLANES, LANES), F32)],
        compiler_params=_cparams(("parallel", "parallel")),
        name="short_conv",
    )(z, z, z, conv_w.astype(F32))


def _sigmoid(x):
    return 1.0 / (1.0 + jnp.exp(-x))


def _rwprep_body(z_ref, zh_ref, vf_ref, mu_ref, w0_ref, a0_ref, v0_ref, w2_ref, a2_ref, v2_ref, g2_ref,
                 r_o, ld_o, k_o, v_o, a_o, g_o, sh_ref, *, tr, nrb):
    i = pl.program_id(0)
    first = (i % nrb) == 0
    sh_ref[SUBLANES - 1:SUBLANES, :] = jnp.where(first, 0.0, zh_ref[SUBLANES - 1:SUBLANES, :])
    sh_ref[SUBLANES:SUBLANES + tr, :] = z_ref[...]

    def shifted(lo, hi):
        zc = z_ref[:, lo:hi]
        zp = sh_ref[SUBLANES - 1:SUBLANES - 1 + tr, lo:hi]
        return zc + (zp - zc) * mu_ref[:, lo:hi]

    r_o[...] = shifted(OD_R, OD_K).astype(r_o.dtype)
    k_o[...] = shifted(OD_K, OD_V).astype(k_o.dtype)
    lora = shifted(OD_LORA, OD_RW_END)
    lora_b = lora.astype(BF16)
    lw = w0_ref[...] + jnp.dot(jnp.tanh(lora).astype(BF16), w2_ref[...], preferred_element_type=F32)
    nlw = -lw
    softplus = jnp.maximum(nlw, 0.0) + jnp.log(1.0 + jnp.exp(-jnp.abs(nlw)))
    ld_o[...] = -jnp.exp(-softplus - 0.5)
    a_o[...] = _sigmoid(a0_ref[...] + jnp.dot(lora_b, a2_ref[...], preferred_element_type=F32)).astype(a_o.dtype)
    v = shifted(OD_V, OD_LORA)
    vg = _sigmoid(v0_ref[...] + jnp.dot(lora_b, v2_ref[...], preferred_element_type=F32))
    v_o[...] = (v + (vf_ref[...] - v) * vg).astype(v_o.dtype)
    g_o[...] = jnp.dot(_sigmoid(lora).astype(BF16), g2_ref[...], preferred_element_type=F32).astype(g_o.dtype)


def _place_rows(w, start):
    return jnp.pad(w, ((start, LORA_W - start - w.shape[0]), (0, 0))).astype(BF16)


def _rwkv_prep(z, z_first, mu_p, w0, a0, v0, w2, a2, v2, g2, T, *, tr=128):
    M = z.shape[0]
    tr = min(tr, T)
    nrb = T // tr
    W = OD_RW_END
    hb = tr // SUBLANES
    row = lambda a: a.astype(F32).reshape(1, RW_WIDTH)
    full = lambda shape: pl.BlockSpec(shape, lambda i: (0, 0))
    outs = [jax.ShapeDtypeStruct((M, RW_WIDTH), F32 if n == 1 else BF16) for n in range(6)]
    ospec = pl.BlockSpec((tr, RW_WIDTH), lambda i: (i, 0))
    return pl.pallas_call(
        functools.partial(_rwprep_body, tr=tr, nrb=nrb),
        grid=(M // tr,),
        in_specs=[pl.BlockSpec((tr, W), lambda i: (i, 0)),
                  pl.BlockSpec((SUBLANES, W), lambda i: (jnp.maximum(i * hb - 1, 0), 0)),
                  pl.BlockSpec((tr, RW_WIDTH), lambda i: (i, 2 * DA_WIDTH // RW_WIDTH)),
                  full((1, W)), full((1, RW_WIDTH)), full((1, RW_WIDTH)), full((1, RW_WIDTH)),
                  full((LORA_W, RW_WIDTH)), full((LORA_W, RW_WIDTH)), full((LORA_W, RW_WIDTH)),
                  full((LORA_W, RW_WIDTH))],
        out_specs=[ospec] * 6,
        out_shape=outs,
        scratch_shapes=[pltpu.VMEM((tr + SUBLANES, W), F32)],
        compiler_params=_cparams(("parallel",)),
        name="rwkv_prep",
    )(z, z, z_first, mu_p, row(w0), row(a0), row(v0),
      _place_rows(w2, 0), _place_rows(a2, W_LORA), _place_rows(v2, W_LORA + A_LORA),
      _place_rows(g2, W_LORA + A_LORA + V_LORA))


def _split3(x):
    hi = x.astype(BF16)
    r1 = x - hi.astype(F32)
    mid = r1.astype(BF16)
    lo = (r1 - mid.astype(F32)).astype(BF16)
    return hi, mid, lo


def _mm(a, b):
    return jnp.dot(a.astype(BF16), b.astype(BF16), preferred_element_type=F32)


def _mm_tn(a, b):
    return jnp.dot(a.T.astype(BF16), b.astype(BF16), preferred_element_type=F32)


def _rwkv_body(r_ref, ld_ref, k_ref, v_ref, a_ref, g_ref, kk_ref, ka_ref, rk_ref, lw_ref, lb_ref,
               o_ref, st_ref, *, ng, L):
    c = pl.program_id(2)

    @pl.when(c == 0)
    def _init():
        st_ref[...] = jnp.zeros_like(st_ref)

    N = RW_HEAD_DIM
    S = RW_GROUP * L
    ri = lax.broadcasted_iota(I32, (S, S), 0)
    ci = lax.broadcasted_iota(I32, (S, S), 1)
    same = (ri // L) == (ci // L)
    incl = same & (ci <= ri)
    strict = same & (ci < ri)
    eye_s = (ci == ri).astype(F32)
    blk16 = (ri // 16) == (ci // 16)
    hmask = (lax.broadcasted_iota(I32, (S, RW_GW), 0) // L) == (lax.broadcasted_iota(I32, (S, RW_GW), 1) // N)
    rl = lax.broadcasted_iota(I32, (L, L), 0)
    cl = lax.broadcasted_iota(I32, (L, L), 1)
    tri = (cl <= rl).astype(BF16)
    rn = lax.broadcasted_iota(I32, (RW_GW, RW_GW), 0)
    cn = lax.broadcasted_iota(I32, (RW_GW, RW_GW), 1)
    eye_g = rn == cn
    ones_g = ((rn // N) == (cn // N)).astype(BF16)

    def gsum(x):
        hi = x.astype(BF16)
        lo = (x - hi.astype(F32)).astype(BF16)
        return jnp.dot(hi, ones_g, preferred_element_type=F32) + jnp.dot(lo, ones_g, preferred_element_type=F32)

    def tile(x):
        return jnp.concatenate([x] * RW_GROUP, axis=0)

    def stack(x):
        return jnp.where(hmask, tile(x), 0.0)

    G = range(ng)
    sls = [slice(gi * RW_GW, (gi + 1) * RW_GW) for gi in G]
    each = lambda f, *lists: [f(*xs) for xs in zip(*lists)]
    r = [r_ref[:, sl].astype(F32) for sl in sls]
    ld = [ld_ref[:, sl] for sl in sls]
    k = [k_ref[:, sl].astype(F32) for sl in sls]
    v = [v_ref[:, sl].astype(F32) for sl in sls]
    a = [a_ref[:, sl].astype(F32) for sl in sls]
    kk = [k[gi] * kk_ref[:, sls[gi]] for gi in G]
    kk = each(lambda x: x / jnp.maximum(jnp.sqrt(gsum(x * x)), 1e-12), kk)
    k2 = [k[gi] * (1.0 + (a[gi] - 1.0) * ka_ref[:, sls[gi]]) for gi in G]
    bv = each(lambda x, y: x * y, kk, a)
    parts = each(_split3, ld)
    cum = each(lambda p: (jnp.dot(tri, p[0], preferred_element_type=F32) + jnp.dot(tri, p[1], preferred_element_type=F32)
                          + jnp.dot(tri, p[2], preferred_element_type=F32)), parts)
    clast = each(lambda c_: c_[L - 1:L, :], cum)
    e_neg = each(lambda c_: jnp.exp(-c_), cum)
    e_l = each(lambda cl_, c_: jnp.exp(cl_ - c_), clast, cum)
    p_l = each(jnp.exp, clast)
    at = each(lambda x, c_, l_: stack(-x * jnp.exp(c_ - l_)), kk, cum, ld)
    rt = each(lambda x, c_: stack(x * jnp.exp(c_)), r, cum)
    vs = each(stack, v)
    bh = each(lambda x, e: stack(x * e), bv, e_l)
    kh = each(lambda x, e: stack(x * e), k2, e_l)
    btb = each(lambda x, e: tile(x * e).astype(BF16), bv, e_neg)
    ktb = each(lambda x, e: tile(x * e).astype(BF16), k2, e_neg)
    atb = each(lambda x: x.astype(BF16), at)
    rtb = each(lambda x: x.astype(BF16), rt)
    mab = each(lambda x, y: jnp.where(strict, _dot_nt(x, y), 0.0), atb, btb)
    mak = each(lambda x, y: jnp.where(strict, _dot_nt(x, y), 0.0), atb, ktb)
    mrb = each(lambda x, y: jnp.where(incl, _dot_nt(x, y), 0.0), rtb, btb)
    mrk = each(lambda x, y: jnp.where(incl, _dot_nt(x, y), 0.0), rtb, ktb)
    nd = each(lambda m: jnp.where(blk16, m, 0.0), mab)
    n2 = each(_mm, nd, nd)
    mv = each(_mm, mak, vs)
    n4 = each(_mm, n2, n2)
    t = each(lambda n, n2_: eye_s + n + _mm(eye_s + n, n2_), nd, n2)
    n8 = each(_mm, n4, n4)
    t = each(lambda t_, n: t_ + _mm(t_, n), t, n4)
    t = each(lambda t_, n: t_ + _mm(t_, n), t, n8)
    size = 16
    while size < L:
        off = ((ri // size) == (ci // size) + 1) & ((ri // (2 * size)) == (ci // (2 * size)))
        u = each(lambda t_, m: _mm(t_, jnp.where(off, m, 0.0)), t, mab)
        t = each(lambda t_, u_: t_ + _mm(u_, t_), t, u)
        size *= 2
    wm = each(_mm, t, at)
    ul = each(_mm, t, mv)
    qe = each(lambda x, m, w: x + _mm(m, w), rt, mrb, wm)
    yl = each(lambda m, u_, m2, x: _mm(m, u_) + _mm(m2, x), mrb, ul, mrk, vs)
    gm = each(lambda b, w, p: _mm_tn(b, w) + jnp.where(eye_g, p, 0.0), bh, wm, p_l)
    hm = each(lambda b, u_, k_, x: _mm_tn(b, u_) + _mm_tn(k_, x), bh, ul, kh, vs)
    st = [st_ref[gi] for gi in G]
    ys = each(lambda q_, s_, y_: _mm(q_, s_) + y_, qe, st, yl)
    st_new = each(lambda g_, s_, h_: _mm(g_, s_) + h_, gm, st, hm)
    for gi in G:
        st_ref[gi] = st_new[gi]
        sl = sls[gi]
        y = ys[gi][0:L]
        for hh in range(1, RW_GROUP):
            y = y + ys[gi][hh * L:(hh + 1) * L]
        mean = gsum(y) * (1.0 / N)
        d = y - mean
        var = gsum(d * d) * (1.0 / N)
        yn = d * lax.rsqrt(var + LNX_EPS) * lw_ref[:, sl] + lb_ref[:, sl]
        yn = yn + gsum(r[gi] * k2[gi] * rk_ref[:, sl]) * v[gi]
        o_ref[:, sl] = (yn * g_ref[:, sl].astype(F32)).astype(o_ref.dtype)


def _rwkv(r, ld, k, v, a, g, k_k, k_a, r_k, lnx_w, lnx_b, B, T, *, ng=4, L=64):
    M = r.shape[0]
    L = min(L, T)
    nc = T // L
    W = ng * RW_GW
    blk = pl.BlockSpec((L, W), lambda b, hg, c: (b * nc + c, hg))
    par = pl.BlockSpec((1, W), lambda b, hg, c: (0, hg))
    row = lambda p: p.astype(F32).reshape(1, RW_WIDTH)
    return pl.pallas_call(
        functools.partial(_rwkv_body, ng=ng, L=L),
        grid=(B, RW_WIDTH // W, nc),
        in_specs=[blk] * 6 + [par] * 5,
        out_specs=blk,
        out_shape=jax.ShapeDtypeStruct((M, RW_WIDTH), BF16),
        scratch_shapes=[pltpu.VMEM((ng, RW_GW, RW_GW), F32)],
        compiler_params=_cparams(("parallel", "parallel", "arbitrary")),
        name="rwkv7_chunk",
    )(r, ld, k, v, a, g, row(k_k), row(k_a), row(r_k), row(lnx_w), row(lnx_b))


def _dsa_body(qd_ref, qi_ref, wi_ref, kd_ref, vd_ref, ki_ref, o_ref,
              keys_ref, bias_ref, qim_ref, cut_ref, m_ref, l_ref, acc_ref, *, tq, tk, ksel, T):
    i = pl.program_id(1)
    nkt = (i * tq + tq + tk - 1) // tk
    krow = lax.broadcasted_iota(I32, (tk, tq), 0)
    qpos = i * tq + lax.broadcasted_iota(I32, (tk, tq), 1)
    low_half = lax.broadcasted_iota(I32, (tq, LANES), 1) < IDX_DIM
    for h in range(IDX_HEADS):
        qt = qi_ref[:, (h // 2) * LANES:(h // 2 + 1) * LANES]
        qim_ref[h] = jnp.where(low_half if h % 2 == 0 else jnp.logical_not(low_half), qt,
                               jnp.zeros_like(qt)).astype(qim_ref.dtype)
    wit = wi_ref[...].T

    def score_tile(j, carry):
        off = pl.multiple_of(j * tk, tk)
        kt = ki_ref[pl.ds(off, tk), :]
        acc = jnp.zeros((tk, tq), F32)
        for h in range(IDX_HEADS):
            acc = acc + jnp.maximum(_dot_nt(kt, qim_ref[h]), 0.0) * wit[h:h + 1, :]
        acc = acc + 0.0
        sc = jnp.where((krow + off) <= qpos, acc, -jnp.inf)
        bits = pltpu.bitcast(sc, I32)
        keys_ref[pl.ds(off, tk), :] = bits ^ ((bits >> 31) & 0x7FFFFFFF)
        return carry

    lax.fori_loop(0, nkt, score_tile, 0)

    def count(pred):
        def body(j, c):
            off = pl.multiple_of(j * tk, tk)
            hit = jnp.where(pred(keys_ref[pl.ds(off, tk), :], krow + off), 1.0, 0.0)
            return c + jnp.sum(hit.reshape(tk // SUBLANES, SUBLANES, tq), axis=0)
        c = lax.fori_loop(0, nkt, body, jnp.zeros((SUBLANES, tq), F32))
        return jnp.sum(c, axis=0, keepdims=True)

    def bit_step(b, thr):
        cand = thr + jnp.left_shift(jnp.int32(1), 31 - b)
        cnt = count(lambda key, _: key >= cand)
        return jnp.where(cnt >= ksel, cand, thr)

    thr = lax.fori_loop(0, 32, bit_step, jnp.full((1, tq), INT_MIN, I32))

    n_gt = count(lambda key, _: key > thr)
    n_ge = count(lambda key, _: key >= thr)
    need = (n_ge > ksel) & (thr > NEG_INF_KEY)
    quota = ksel - n_gt
    cut_ref[...] = jnp.full((1, tq), T, I32)

    @pl.when(jnp.max(jnp.where(need, 1.0, 0.0)) > 0.0)
    def _ties():
        def pos_step(b, p):
            cand = p + jnp.left_shift(jnp.int32(1), (T.bit_length() - 1) - b)
            cnt = count(lambda key, pos: (key == thr) & (pos < cand))
            return jnp.where(cnt < quota, cand, p)
        p = lax.fori_loop(0, T.bit_length(), pos_step, jnp.zeros((1, tq), I32))
        cut_ref[...] = jnp.where(need, p, T)

    cut = cut_ref[...]
    thr_sel = jnp.maximum(thr, NEG_INF_KEY + 1)

    def bias_tile(j, carry):
        off = pl.multiple_of(j * tk, tk)
        key = keys_ref[pl.ds(off, tk), :]
        sel = (key > thr_sel) | ((key == thr_sel) & ((krow + off) <= cut))
        bias_ref[pl.ds(off, tk), :] = jnp.where(sel, 0.0, -jnp.inf)
        return carry

    lax.fori_loop(0, nkt, bias_tile, 0)

    m_ref[...] = jnp.full(m_ref.shape, -jnp.inf, F32)
    l_ref[...] = jnp.zeros(l_ref.shape, F32)
    acc_ref[...] = jnp.zeros(acc_ref.shape, F32)

    def attend(j, carry):
        off = pl.multiple_of(j * tk, tk)
        ks = kd_ref[pl.ds(off, tk), :]
        vt = vd_ref[pl.ds(off, tk), :].T
        bias = bias_ref[pl.ds(off, tk), :]
        heads = list(range(SA_HEADS))
        ss = [_dot_nt(ks, qd_ref[:, h * LANES:(h + 1) * LANES]) + bias for h in heads]
        _softmax_steps(ss, vt, m_ref, l_ref, acc_ref, heads, guard=True)
        return carry

    lax.fori_loop(0, nkt, attend, 0)
    for h in range(SA_HEADS):
        o_ref[:, h * LANES:(h + 1) * LANES] = (acc_ref[h] * (1.0 / l_ref[h])).T.astype(o_ref.dtype)


def _dsa(qd, qi, wi, kd, vd, ki, B, T, ksel, *, tq=512, tk=512):
    M = qd.shape[0]
    tq, tk = min(tq, T), min(tk, T)
    nq = T // tq
    qblk = lambda w: pl.BlockSpec((tq, w), lambda b, i: (b * nq + i, 0))
    kblk = pl.BlockSpec((T, LANES), lambda b, i: (b, 0))
    return pl.pallas_call(
        functools.partial(_dsa_body, tq=tq, tk=tk, ksel=ksel, T=T),
        grid=(B, nq),
        in_specs=[qblk(SA_WIDTH), qblk(IDX_HEADS * IDX_DIM), qblk(LANES), kblk, kblk, kblk],
        out_specs=qblk(SA_WIDTH),
        out_shape=jax.ShapeDtypeStruct((M, SA_WIDTH), BF16),
        scratch_shapes=[pltpu.VMEM((T, tq), I32), pltpu.VMEM((T, tq), F32),
                        pltpu.VMEM((IDX_HEADS, tq, LANES), BF16), pltpu.VMEM((1, tq), I32),
                        pltpu.VMEM((SA_HEADS, 1, tq), F32), pltpu.VMEM((SA_HEADS, 1, tq), F32),
                        pltpu.VMEM((SA_HEADS, LANES, tq), F32)],
        compiler_params=_cparams(("parallel", "arbitrary")),
        name="dsa_attention",
    )(qd, qi, wi, kd, vd, ki)


def _even_mixer(xf, h, w_in, w_out, e, q_norm, k_norm, lam_p, subln, conv_w, tabs64, lam_init, B, T):
    z = _matmul(h, _cast_pad(w_in, e), name="even_in")
    qh = _segment(z, 0, DA_WIDTH, q_norm, tabs64, T, gs=64, do_norm=True, do_rope=True,
                  scale=DA_HEAD_DIM ** -0.5 * LOG2E)
    kh = _segment(z, DA_WIDTH, DA_WIDTH, k_norm, tabs64, T, gs=64, do_norm=True, do_rope=True)
    vh = _segment(z, 2 * DA_WIDTH, DA_WIDTH, None, tabs64, T, gs=64, do_norm=False, do_rope=False)
    o = _diff_attention(qh, kh, vh, lam_p, subln, lam_init, B, T)
    y = _short_conv(z, conv_w, B, T)
    return _matmul(o, _cast_pad(w_out, e), a2=y, resid=xf, name="even_out"), z


def _odd_mixer(xf, h, w_in_t, w_out, o, mu_p, w0, w2, a0, a2, v0, v2, g2, k_k, k_a, r_k, lnx_w, lnx_b,
               q_norm, k_norm, idxk_norm, z_first, tabs64, tabs128, B, T, ksel):
    z = _matmul(h, w_in_t, b_transposed=True, name="odd_in")
    r, ld, k, v, a, g = _rwkv_prep(z, z_first, mu_p, w0, a0, v0, w2, a2, v2, g2, T)
    rw_out = _rwkv(r, ld, k, v, a, g, k_k, k_a, r_k.reshape(-1), lnx_w, lnx_b, B, T)
    qd = _segment(z, OD_Q, SA_WIDTH, q_norm, tabs128, T, gs=128, do_norm=True, do_rope=True,
                  scale=SA_HEAD_DIM ** -0.5 * LOG2E)
    kd = _segment(z, OD_KD, LANES, k_norm, tabs128, T, gs=128, do_norm=True, do_rope=True)
    vd = _segment(z, OD_VDD, LANES, None, tabs128, T, gs=128, do_norm=False, do_rope=False)
    qi = _segment(z, OD_QI, IDX_HEADS * IDX_DIM, None, tabs64, T, gs=64, do_norm=False, do_rope=True)
    ki = _segment(z, OD_KI, LANES, idxk_norm, tabs64, T, gs=64, do_norm=True, do_rope=True, pick="dup_low")
    wi = _segment(z, OD_KI, LANES, None, tabs64, T, gs=64, do_norm=False, do_rope=False,
                  scale=IDX_HEADS ** -0.5 * IDX_DIM ** -0.5, out_dtype=F32, pick="high16")
    sa_out = _dsa(qd, qi, wi, kd, vd, ki, B, T, ksel)
    return _matmul(rw_out, _cast_pad(w_out, o), a2=sa_out, resid=xf, name="odd_out")


def kernel(x, mix_norm, ffn_norm, ffn_gate, ffn_up, ffn_down, ev_w_in, ev_w_out, da_q_norm, da_k_norm, da_lambda, da_subln, sc_conv, od_w_in, od_w_out, rw_mu, rw_w0, rw_w2, rw_a0, rw_a2, rw_v0, rw_v2, rw_g2, rw_k_k, rw_k_a, rw_r_k, rw_lnx_w, rw_lnx_b, sa_q_norm, sa_k_norm, idx_k_norm):
    B, T, D = x.shape
    M = B * T
    ksel = min(TOPK_MAX, T // 4)
    xf = x.reshape(M, D)
    tabs64 = _rope_tables(T, 64)
    tabs128 = _rope_tables(T, 128)
    z_first = None
    for i in range(DEPTH):
        h = _rmsnorm(xf, mix_norm[i])
        if i % 2 == 0:
            e = i // 2
            lam_init = 0.8 - 0.6 * math.exp(-0.3 * i)
            xf, z = _even_mixer(xf, h, ev_w_in, ev_w_out, e, da_q_norm[e],
                                da_k_norm[e], da_lambda[e], da_subln[e], sc_conv[e], tabs64, lam_init, B, T)
            if z_first is None:
                z_first = z
        else:
            o = i // 2
            xf = _odd_mixer(xf, h, _cast_pad(jnp.swapaxes(od_w_in, 1, 2), o, rows_p=OD_PAD), od_w_out, o,
                            rw_mu[o].astype(F32).reshape(1, OD_RW_END),
                            rw_w0[o], rw_w2[o], rw_a0[o], rw_a2[o], rw_v0[o], rw_v2[o], rw_g2[o],
                            rw_k_k[o], rw_k_a[o], rw_r_k[o], rw_lnx_w[o], rw_lnx_b[o],
                            sa_q_norm[o], sa_k_norm[o], idx_k_norm[o], z_first, tabs64, tabs128, B, T, ksel)
        h = _rmsnorm(xf, ffn_norm[i])
        hid = _swiglu_proj(h, ffn_gate, ffn_up, i)
        wd = _cast_pad(ffn_down, i, tr=256, tc=2048)
        xf = _matmul(hid, wd, resid=xf, tm=1024, tn=256, single_buffer_a=True, name="ffn_out")
    return xf.reshape(B, T, D)
```
